```python
import math
import jax, jax.numpy as jnp
from jax import lax
import numpy as np

D_MODEL = 1024
BATCH = 32
SEQ = 256
DEPTH = 4
DEC_BATCH = 4
DEC_SEQ = 2048
PAST_LEN = 512

GRID_W = 64
N_S5_LAYERS = (DEPTH + 1) // 2
N_RET_LAYERS = DEPTH // 2
S5_WIDTH = D_MODEL // 2
CONV_WIDTH = D_MODEL - S5_WIDTH
S5_GROUP_CH = 16
S5_GROUPS = S5_WIDTH // S5_GROUP_CH
S5_STATE = 64
CONV_K = 3
HY_IN = S5_WIDTH + 3 * CONV_WIDTH
HY_MIX = S5_WIDTH + CONV_WIDTH
RET_HEADS = 8
RET_DK = D_MODEL // RET_HEADS
RET_DV = 2 * RET_DK
RET_QK = RET_HEADS * RET_DK
RET_V = RET_HEADS * RET_DV
RET_IN = 2 * RET_QK + 2 * RET_V
RET_CHUNK = 128
ROPE_BASE = 10000.0
MLP_HIDDEN = 4 * D_MODEL
N_MOD = 6
EPS = 1e-6

kernel_name = 'hybrid_s5_conv_retention_diffusion_step'

F32 = jnp.float32


def rms_norm(x, w):
    xf = x.astype(F32)
    y = xf * lax.rsqrt(jnp.mean(xf * xf, axis=-1, keepdims=True) + EPS)
    return (y * w.astype(F32)).astype(x.dtype)


def _cmul(ar, ai, br, bi):
    return ar * br - ai * bi, ar * bi + ai * br


def s5_direction(ug, lam_re, lam_im, log_step, b_re, b_im, c_re, c_im, h0, reverse):
    dt = jnp.exp(log_step.astype(F32))[:, None]
    lam_re = lam_re.astype(F32)
    lam_im = lam_im.astype(F32)
    mag = jnp.exp(lam_re * dt)
    ab_re, ab_im = mag * jnp.cos(lam_im * dt), mag * jnp.sin(lam_im * dt)
    den = lam_re * lam_re + lam_im * lam_im
    f_re = ((ab_re - 1.0) * lam_re + ab_im * lam_im) / den
    f_im = (ab_im * lam_re - (ab_re - 1.0) * lam_im) / den
    bb_re, bb_im = _cmul(f_re[..., None], f_im[..., None], b_re.astype(F32), b_im.astype(F32))
    bu_re = jnp.einsum('blgc,gpc->blgp', ug, bb_re)
    bu_im = jnp.einsum('blgc,gpc->blgp', ug, bb_im)
    if h0 is not None:
        h_re, h_im = _cmul(ab_re, ab_im, h0[0], h0[1])
        first = -1 if reverse else 0
        bu_re = bu_re.at[:, first].add(h_re)
        bu_im = bu_im.at[:, first].add(h_im)
    a_re = jnp.broadcast_to(ab_re, bu_re.shape)
    a_im = jnp.broadcast_to(ab_im, bu_im.shape)

    def combine(e1, e2):
        a1r, a1i, b1r, b1i = e1
        a2r, a2i, b2r, b2i = e2
        ar, ai = _cmul(a2r, a2i, a1r, a1i)
        br, bi = _cmul(a2r, a2i, b1r, b1i)
        return ar, ai, br + b2r, bi + b2i

    _, _, x_re, x_im = lax.associative_scan(combine, (a_re, a_im, bu_re, bu_im), axis=1, reverse=reverse)
    y = (jnp.einsum('blgp,gcp->blgc', x_re, c_re.astype(F32))
         - jnp.einsum('blgp,gcp->blgc', x_im, c_im.astype(F32)))
    last = 0 if reverse else -1
    return y, x_re[:, last], x_im[:, last]


def s5_mixer(u, lam_re, lam_im, log_step, b_re, b_im, c_re, c_im, d, h0_re, h0_im):
    bsz, length, _ = u.shape
    uf = u.astype(F32)
    ug = uf.reshape(bsz, length, S5_GROUPS, S5_GROUP_CH)
    y = d.astype(F32) * uf
    fin_re, fin_im = [], []
    for direction in range(2):
        h0 = None if h0_re is None else (h0_re[:, direction].astype(F32), h0_im[:, direction].astype(F32))
        yd, fr, fi = s5_direction(ug, lam_re[direction], lam_im[direction], log_step[direction],
                                  b_re[direction], b_im[direction], c_re[direction], c_im[direction],
                                  h0, reverse=(direction == 1))
        y = y + yd.reshape(bsz, length, S5_WIDTH)
        fin_re.append(fr)
        fin_im.append(fi)
    return y, jnp.stack(fin_re, 1), jnp.stack(fin_im, 1)


def short_conv(h, w, b):
    out = lax.conv_general_dilated(h, w[:, None, :].astype(h.dtype), window_strides=(1,),
                                   padding=((CONV_K // 2, CONV_K // 2),),
                                   dimension_numbers=('NWC', 'WIO', 'NWC'),
                                   feature_group_count=h.shape[-1])
    return out + b.astype(h.dtype)


def hybrid_mixer(h, in_w, out_w, lam_re, lam_im, log_step, b_re, b_im, c_re, c_im, d,
                 glu_w, glu_b, conv_w, conv_b, h0_re, h0_im):
    proj = h @ in_w
    u, bg, cg, v = jnp.split(proj, [S5_WIDTH, S5_WIDTH + CONV_WIDTH, S5_WIDTH + 2 * CONV_WIDTH], axis=-1)
    y, s_re, s_im = s5_mixer(u, lam_re, lam_im, log_step, b_re, b_im, c_re, c_im, d, h0_re, h0_im)
    z = jax.nn.gelu(y)
    a_out = z * jax.nn.sigmoid(z @ glu_w.astype(F32) + glu_b.astype(F32))
    b_out = bg * short_conv(cg * v, conv_w, conv_b)
    mixed = jnp.concatenate([a_out.astype(b_out.dtype), b_out], axis=-1)
    return mixed @ out_w, s_re, s_im


def axial_rotary(t):
    length = t.shape[1]
    rows = length // GRID_W
    row = jnp.repeat(jnp.arange(rows, dtype=F32), GRID_W)
    col = jnp.tile(jnp.arange(GRID_W, dtype=F32), rows)
    n_freq = RET_DK // 4
    inv_freq = jnp.power(ROPE_BASE, -jnp.arange(n_freq, dtype=F32) / n_freq)

    def rot(x, pos):
        ang = pos[:, None] * inv_freq
        cos = jnp.cos(ang)[:, None, :]
        sin = jnp.sin(ang)[:, None, :]
        x1, x2 = x[..., :n_freq], x[..., n_freq:]
        return jnp.concatenate([x1 * cos - x2 * sin, x1 * sin + x2 * cos], axis=-1)

    half = RET_DK // 2
    return jnp.concatenate([rot(t[..., :half], row), rot(t[..., half:], col)], axis=-1)


def retention_chunked(q, k, v, log_g, s0):
    bsz, length, _, _ = q.shape
    n_chunks = length // RET_CHUNK
    idx = jnp.arange(RET_CHUNK, dtype=F32)
    diff = idx[:, None] - idx[None, :]
    intra = jnp.where(diff >= 0, jnp.exp(log_g[:, None, None] * jnp.maximum(diff, 0.0)), 0.0)
    q_decay = jnp.exp(log_g[None, :] * (idx[:, None] + 1.0))[None, :, :, None]
    k_decay = jnp.exp(log_g[None, :] * (RET_CHUNK - 1.0 - idx[:, None]))[None, :, :, None]
    chunk_decay = jnp.exp(log_g * RET_CHUNK)[None, :, None, None]
    if s0 is None:
        s0 = jnp.zeros((bsz, RET_HEADS, RET_DK, RET_DV), F32)

    def to_chunks(t):
        return jnp.moveaxis(t.reshape(bsz, n_chunks, RET_CHUNK, t.shape[2], t.shape[3]), 1, 0)

    def step(s, qkv):
        qc, kc, vc = qkv
        scores = jnp.einsum('bihd,bjhd->bhij', qc, kc) * intra
        o = (jnp.einsum('bhij,bjhe->bihe', scores, vc)
             + jnp.einsum('bihd,bhde->bihe', qc, s) * q_decay)
        s = s * chunk_decay + jnp.einsum('bjhd,bjhe->bhde', kc * k_decay, vc)
        return s, o

    s_fin, o = lax.scan(step, s0, (to_chunks(q), to_chunks(k), to_chunks(v)))
    o = jnp.moveaxis(o, 0, 1).reshape(bsz, length, RET_HEADS, RET_DV)
    return o, s_fin


def retention_mixer(h, in_w, out_w, gamma_logit, gn_w, s0):
    latent = s0 is not None
    bsz, length, _ = h.shape
    proj = h @ in_w
    q, k, v, g = jnp.split(proj, [RET_QK, 2 * RET_QK, 2 * RET_QK + RET_V], axis=-1)
    q = q.astype(F32).reshape(bsz, length, RET_HEADS, RET_DK)
    k = k.astype(F32).reshape(bsz, length, RET_HEADS, RET_DK)
    v = v.astype(F32).reshape(bsz, length, RET_HEADS, RET_DV)
    if latent:
        q = axial_rotary(q)
        k = axial_rotary(k)
    q = q * (RET_DK ** -0.5)
    log_g = jax.nn.log_sigmoid(gamma_logit.astype(F32))
    s0_f = s0[:, 0].astype(F32) if latent else None
    s0_b = s0[:, 1].astype(F32) if latent else None
    o_f, s_f = retention_chunked(q, k, v, log_g[0], s0_f)
    o_b, s_b = retention_chunked(jnp.flip(q, 1), jnp.flip(k, 1), jnp.flip(v, 1), log_g[1], s0_b)
    o = o_f + jnp.flip(o_b, 1)
    o = o * lax.rsqrt(jnp.mean(o * o, axis=-1, keepdims=True) + EPS) * gn_w.astype(F32).reshape(RET_HEADS, RET_DV)
    o = jax.nn.silu(g.astype(F32)) * o.reshape(bsz, length, RET_V)
    return o.astype(h.dtype) @ out_w, jnp.stack([s_f, s_b], axis=1)


def squared_relu_mlp(h, w1, w2):
    a = jax.nn.relu(h @ w1)
    return (a * a) @ w2


def run_trunk(x, cond, prm, s5_re0, s5_im0, ret0):
    latent = ret0 is not None
    sc = jax.nn.silu(cond)
    new_re, new_im, new_ret = [], [], []
    for i in range(DEPTH):
        j = i // 2
        mod = sc @ prm['ada_w'][i] + prm['ada_b'][i]
        sh1, sc1, g1, sh2, sc2, g2 = jnp.split(mod[:, None, :], N_MOD, axis=-1)
        h = rms_norm(x, prm['norm1_w'][i]) * (1.0 + sc1) + sh1
        if i % 2 == 0:
            out, s_re, s_im = hybrid_mixer(
                h, prm['hy_in_w'][j], prm['hy_out_w'][j],
                prm['s5_lam_re'][j], prm['s5_lam_im'][j], prm['s5_log_step'][j],
                prm['s5_b_re'][j], prm['s5_b_im'][j], prm['s5_c_re'][j], prm['s5_c_im'][j],
                prm['s5_d'][j], prm['s5_glu_w'][j], prm['s5_glu_b'][j],
                prm['conv_w'][j], prm['conv_b'][j],
                s5_re0[:, j] if latent else None, s5_im0[:, j] if latent else None)
            new_re.append(s_re)
            new_im.append(s_im)
        else:
            out, s_ret = retention_mixer(h, prm['ret_in_w'][j], prm['ret_out_w'][j],
                                         prm['ret_gamma_logit'][j], prm['ret_gn_w'][j],
                                         ret0[:, j] if latent else None)
            new_ret.append(s_ret)
        x = x + g1 * out
        h = rms_norm(x, prm['norm2_w'][i]) * (1.0 + sc2) + sh2
        x = x + g2 * squared_relu_mlp(h, prm['mlp_w1'][i], prm['mlp_w2'][i])
    y = rms_norm(x, prm['final_norm_w'])
    if latent:
        return y
    return y, jnp.stack(new_re, 1), jnp.stack(new_im, 1), jnp.stack(new_ret, 1)


def setup_inputs(seed: int = 0) -> dict:
    key = jax.random.key(seed)
    ks = jax.random.split(key, 32)
    D = D_MODEL

    def nrm(k, shape, s):
        return jax.random.normal(k, shape, F32) * s

    gam = 1.0 - jnp.power(2.0, -5.0 - jnp.arange(RET_HEADS, dtype=F32))
    ret_scale = jnp.sqrt((1.0 - gam ** (2 * PAST_LEN)) / (1.0 - gam * gam))
    s5_shape = (N_S5_LAYERS, 2, S5_GROUPS, S5_STATE)
    return {
        'x_prompt': nrm(ks[0], (BATCH, SEQ, D), 1.0),
        'x_sample': nrm(ks[1], (DEC_BATCH, DEC_SEQ, D), 1.0),
        'state_s5_re': nrm(ks[2], (DEC_BATCH,) + s5_shape, 1.0),
        'state_s5_im': nrm(ks[3], (DEC_BATCH,) + s5_shape, 1.0),
        'state_ret': nrm(ks[4], (DEC_BATCH, N_RET_LAYERS, 2, RET_HEADS, RET_DK, RET_DV), 1.0) * ret_scale[:, None, None],
        'c': nrm(ks[5], (DEC_BATCH, D), 1.0),
        'c_ctx': nrm(ks[6], (D,), 1.0),
        'norm1_w': 1.0 + nrm(ks[7], (DEPTH, D), 0.02),
        'norm2_w': 1.0 + nrm(ks[8], (DEPTH, D), 0.02),
        'ada_w': nrm(ks[9], (DEPTH, D, N_MOD * D), 0.5 * D ** -0.5),
        'ada_b': nrm(ks[10], (DEPTH, N_MOD * D), 0.02),
        'hy_in_w': nrm(ks[11], (N_S5_LAYERS, D, HY_IN), D ** -0.5),
        'hy_out_w': nrm(ks[12], (N_S5_LAYERS, HY_MIX, D), HY_MIX ** -0.5),
        's5_lam_re': -0.5 + nrm(ks[13], s5_shape, 0.01),
        's5_lam_im': math.pi * jnp.arange(S5_STATE, dtype=F32) + nrm(ks[14], s5_shape, 0.01),
        's5_log_step': jax.random.uniform(ks[15], (N_S5_LAYERS, 2, S5_GROUPS), F32, math.log(1e-3), math.log(1e-1)),
        's5_b_re': nrm(ks[16], s5_shape + (S5_GROUP_CH,), (2 * S5_GROUP_CH) ** -0.5),
        's5_b_im': nrm(ks[17], s5_shape + (S5_GROUP_CH,), (2 * S5_GROUP_CH) ** -0.5),
        's5_c_re': nrm(ks[18], (N_S5_LAYERS, 2, S5_GROUPS, S5_GROUP_CH, S5_STATE), (2 * S5_STATE) ** -0.5),
        's5_c_im': nrm(ks[19], (N_S5_LAYERS, 2, S5_GROUPS, S5_GROUP_CH, S5_STATE), (2 * S5_STATE) ** -0.5),
        's5_d': nrm(ks[20], (N_S5_LAYERS, S5_WIDTH), 1.0),
        's5_glu_w': nrm(ks[21], (N_S5_LAYERS, S5_WIDTH, S5_WIDTH), S5_WIDTH ** -0.5),
        's5_glu_b': nrm(ks[22], (N_S5_LAYERS, S5_WIDTH), 0.02),
        'conv_w': nrm(ks[23], (N_S5_LAYERS, CONV_K, CONV_WIDTH), CONV_K ** -0.5),
        'conv_b': nrm(ks[24], (N_S5_LAYERS, CONV_WIDTH), 0.02),
        'ret_in_w': nrm(ks[25], (N_RET_LAYERS, D, RET_IN), D ** -0.5),
        'ret_out_w': nrm(ks[26], (N_RET_LAYERS, RET_V, D), RET_V ** -0.5),
        'ret_gamma_logit': jnp.log(gam / (1.0 - gam)) + nrm(ks[27], (N_RET_LAYERS, 2, RET_HEADS), 0.05),
        'ret_gn_w': 1.0 + nrm(ks[28], (N_RET_LAYERS, RET_V), 0.02),
        'mlp_w1': nrm(ks[29], (DEPTH, D, MLP_HIDDEN), D ** -0.5),
        'mlp_w2': nrm(ks[30], (DEPTH, MLP_HIDDEN, D), MLP_HIDDEN ** -0.5),
        'final_norm_w': 1.0 + nrm(ks[31], (D,), 0.02),
    }


def reference(x_prompt, x_sample, state_s5_re, state_s5_im, state_ret, c, c_ctx,
              norm1_w, norm2_w, ada_w, ada_b, hy_in_w, hy_out_w,
              s5_lam_re, s5_lam_im, s5_log_step, s5_b_re, s5_b_im, s5_c_re, s5_c_im,
              s5_d, s5_glu_w, s5_glu_b, conv_w, conv_b,
              ret_in_w, ret_out_w, ret_gamma_logit, ret_gn_w,
              mlp_w1, mlp_w2, final_norm_w):
    prm = dict(norm1_w=norm1_w, norm2_w=norm2_w, ada_w=ada_w, ada_b=ada_b,
               hy_in_w=hy_in_w, hy_out_w=hy_out_w,
               s5_lam_re=s5_lam_re, s5_lam_im=s5_lam_im, s5_log_step=s5_log_step,
               s5_b_re=s5_b_re, s5_b_im=s5_b_im, s5_c_re=s5_c_re, s5_c_im=s5_c_im,
               s5_d=s5_d, s5_glu_w=s5_glu_w, s5_glu_b=s5_glu_b,
               conv_w=conv_w, conv_b=conv_b,
               ret_in_w=ret_in_w, ret_out_w=ret_out_w, ret_gamma_logit=ret_gamma_logit, ret_gn_w=ret_gn_w,
               mlp_w1=mlp_w1, mlp_w2=mlp_w2, final_norm_w=final_norm_w)
    y_prompt, new_s5_re, new_s5_im, new_ret = run_trunk(x_prompt, c_ctx[None, :], prm, None, None, None)
    y_sample = run_trunk(x_sample, c, prm, state_s5_re, state_s5_im, state_ret)
    return (y_prompt, y_sample, new_s5_re, new_s5_im, new_ret)
```

```python
import functools
import math

import jax
import jax.numpy as jnp
from jax import lax
from jax.experimental import pallas as pl
from jax.experimental.pallas import tpu as pltpu

F32 = jnp.float32
BF16 = jnp.bfloat16

D_MODEL = 1024
DEPTH = 4
N_CTX, L_CTX = 32, 256
N_LAT, L_LAT = 4, 2048
TOK_CTX = N_CTX * L_CTX
TOK = TOK_CTX + N_LAT * L_LAT
N_COND = 1 + N_LAT
GRID_W = 64
EPS = 1e-6

S5_WIDTH = 512
S5_GROUP_CH = 16
S5_GROUPS = 32
S5_STATE = 64
S5_KBLK = 4
S5_GPB = S5_GROUPS // S5_KBLK
S5_BLK = S5_GPB * S5_STATE
S5_SEQ = 8
S5_T = 64
S5_W = 256
CONV_W = 512
HY_IN = S5_WIDTH + 3 * CONV_W

RET_H = 8
RET_DK = 128
RET_DV = 256
RET_QK = RET_H * RET_DK
RET_V = RET_H * RET_DV
RET_IN = 2 * RET_QK + 2 * RET_V
RET_C = 128
ROPE_BASE = 10000.0
MLP_H = 4 * D_MODEL
N_MOD = 6
MOD_ROWS = 8

VMEM_BIG = 56 * 1024 * 1024
VMEM_MID = 40 * 1024 * 1024

TM_DENSE = 1024
TN_DENSE = 1024
TH_MLP = 1024
TM_MIX = 512
HALO = 8


def _cond_row(i, tm):
    row0 = i * tm
    return jnp.where(row0 < TOK_CTX, 0, 1 + (row0 - TOK_CTX) // L_LAT)


def _norm_mod(x, nw, scale, shift):
    y = x * lax.rsqrt(jnp.mean(x * x, axis=-1, keepdims=True) + EPS)
    return (y * nw) * (1.0 + scale) + shift


def _silu(x):
    return x * jax.nn.sigmoid(x)


def _ada_kernel(c_ref, w_ref, b_ref, o_ref):
    sc = _silu(c_ref[...]).astype(BF16)
    o_ref[0] = jnp.dot(sc, w_ref[0].astype(BF16), preferred_element_type=F32) + b_ref[0]


def _ada_call(cond, ada_w, ada_b):
    tn = 1536
    return pl.pallas_call(
        _ada_kernel,
        grid=(DEPTH, N_MOD * D_MODEL // tn),
        in_specs=[
            pl.BlockSpec((MOD_ROWS, D_MODEL), lambda l, j: (0, 0)),
            pl.BlockSpec((1, D_MODEL, tn), lambda l, j: (l, 0, j)),
            pl.BlockSpec((1, 1, tn), lambda l, j: (l, 0, j)),
        ],
        out_specs=pl.BlockSpec((1, MOD_ROWS, tn), lambda l, j: (l, 0, j)),
        out_shape=jax.ShapeDtypeStruct((DEPTH, MOD_ROWS, N_MOD * D_MODEL), F32),
        compiler_params=pltpu.CompilerParams(
            dimension_semantics=("arbitrary", "arbitrary"), vmem_limit_bytes=VMEM_MID),
        name="ada_mod",
    )(cond, ada_w, ada_b.reshape(DEPTH, 1, N_MOD * D_MODEL))


def _in_kernel(x_ref, mod_ref, nw_ref, w_ref, o_ref, h_ref):
    @pl.when(pl.program_id(1) == 0)
    def _():
        h = _norm_mod(x_ref[...], nw_ref[...], mod_ref[1:2, :], mod_ref[0:1, :])
        h_ref[...] = h.astype(BF16)

    o_ref[...] = jnp.dot(h_ref[...], w_ref[...], preferred_element_type=F32)


def _in_call(x, mod, nw, w):
    n = w.shape[1]
    tm, tn = TM_DENSE, TN_DENSE
    return pl.pallas_call(
        _in_kernel,
        grid=(TOK // tm, n // tn),
        in_specs=[
            pl.BlockSpec((tm, D_MODEL), lambda i, j: (i, 0)),
            pl.BlockSpec((None, MOD_ROWS, D_MODEL), lambda i, j: (_cond_row(i, tm), 0, 0)),
            pl.BlockSpec((1, D_MODEL), lambda i, j: (0, 0)),
            pl.BlockSpec((D_MODEL, tn), lambda i, j: (0, j)),
        ],
        out_specs=pl.BlockSpec((tm, tn), lambda i, j: (i, j)),
        out_shape=jax.ShapeDtypeStruct((TOK, n), F32),
        scratch_shapes=[pltpu.VMEM((tm, D_MODEL), BF16)],
        compiler_params=pltpu.CompilerParams(
            dimension_semantics=("parallel", "arbitrary"), vmem_limit_bytes=VMEM_MID),
        name="norm_proj",
    )(x, mod, nw.reshape(1, D_MODEL), w)


def _mlp_kernel(x_ref, mod_ref, nw_ref, w1_ref, w2_ref, fw_ref, o_ref, h_ref, acc_ref, *, final):
    j = pl.program_id(1)

    @pl.when(j == 0)
    def _():
        h = _norm_mod(x_ref[...], nw_ref[...], mod_ref[4:5, :], mod_ref[3:4, :])
        h_ref[...] = h.astype(BF16)
        acc_ref[...] = jnp.zeros_like(acc_ref)

    a = jnp.maximum(jnp.dot(h_ref[...], w1_ref[...], preferred_element_type=F32), 0.0)
    acc_ref[...] += jnp.dot((a * a).astype(BF16), w2_ref[...], preferred_element_type=F32)

    @pl.when(j == pl.num_programs(1) - 1)
    def _():
        y = x_ref[...] + mod_ref[5:6, :] * acc_ref[...]
        if final:
            y = y * lax.rsqrt(jnp.mean(y * y, axis=-1, keepdims=True) + EPS) * fw_ref[...]
        o_ref[...] = y


def _mlp_call(x, mod, nw, w1, w2, fw, final):
    tm, th = TM_DENSE, TH_MLP
    return pl.pallas_call(
        functools.partial(_mlp_kernel, final=final),
        grid=(TOK // tm, MLP_H // th),
        in_specs=[
            pl.BlockSpec((tm, D_MODEL), lambda i, j: (i, 0)),
            pl.BlockSpec((None, MOD_ROWS, D_MODEL), lambda i, j: (_cond_row(i, tm), 0, 0)),
            pl.BlockSpec((1, D_MODEL), lambda i, j: (0, 0)),
            pl.BlockSpec((D_MODEL, th), lambda i, j: (0, j)),
            pl.BlockSpec((th, D_MODEL), lambda i, j: (j, 0)),
            pl.BlockSpec((1, D_MODEL), lambda i, j: (0, 0)),
        ],
        out_specs=pl.BlockSpec((tm, D_MODEL), lambda i, j: (i, 0)),
        out_shape=jax.ShapeDtypeStruct((TOK, D_MODEL), F32),
        scratch_shapes=[pltpu.VMEM((tm, D_MODEL), BF16), pltpu.VMEM((tm, D_MODEL), F32)],
        compiler_params=pltpu.CompilerParams(
            dimension_semantics=("parallel", "arbitrary"), vmem_limit_bytes=VMEM_BIG),
        name="mlp",
    )(x, mod, nw.reshape(1, D_MODEL), w1, w2, fw.reshape(1, D_MODEL))


def _resid_kernel(a_ref, w_ref, x_ref, mod_ref, o_ref):
    y = jnp.dot(a_ref[...], w_ref[...], preferred_element_type=F32)
    o_ref[...] = x_ref[...] + mod_ref[2:3, :] * y


def _resid_call(a, w, x, mod):
    tm = TM_DENSE
    k = a.shape[1]
    return pl.pallas_call(
        _resid_kernel,
        grid=(TOK // tm,),
        in_specs=[
            pl.BlockSpec((tm, k), lambda i: (i, 0)),
            pl.BlockSpec((k, D_MODEL), lambda i: (0, 0)),
            pl.BlockSpec((tm, D_MODEL), lambda i: (i, 0)),
            pl.BlockSpec((None, MOD_ROWS, D_MODEL), lambda i: (_cond_row(i, tm), 0, 0)),
        ],
        out_specs=pl.BlockSpec((tm, D_MODEL), lambda i: (i, 0)),
        out_shape=jax.ShapeDtypeStruct((TOK, D_MODEL), F32),
        compiler_params=pltpu.CompilerParams(
            dimension_semantics=("parallel",), vmem_limit_bytes=VMEM_MID),
        name="out_proj_resid",
    )(a, w, x, mod)


def _disc_kernel(lr_ref, li_ref, ls_ref, br_ref, bi_ref, abr_ref, abi_ref, bbr_ref, bbi_ref):
    lr, li = lr_ref[...], li_ref[...]
    dt = jnp.exp(ls_ref[...])
    mag = jnp.exp(lr * dt)
    abr = mag * jnp.cos(li * dt)
    abi = mag * jnp.sin(li * dt)
    den = lr * lr + li * li
    fr = ((abr - 1.0) * lr + abi * li) / den
    fi = (abi * lr - (abr - 1.0) * li) / den
    abr_ref[...] = abr
    abi_ref[...] = abi
    for c in range(S5_GROUP_CH):
        br, bi = br_ref[c], bi_ref[c]
        bbr_ref[c] = fr * br - fi * bi
        bbi_ref[c] = fr * bi + fi * br


def _disc_call(lam_re, lam_im, log_step, b_re, b_im):
    r = lam_re.shape[0]
    small = jax.ShapeDtypeStruct((r, S5_STATE), F32)
    big = jax.ShapeDtypeStruct((S5_GROUP_CH, r, S5_STATE), F32)
    return pl.pallas_call(_disc_kernel, out_shape=(small, small, big, big), name="s5_discretise")(
        lam_re, lam_im, log_step, b_re, b_im)


def _s5_kernel(uf_ref, ub_ref, wb_ref, wc_ref, a_ref, h0_ref, yf_ref, yb_ref, hfin_ref, bu_ref, st_ref):
    c = pl.program_id(1)
    rows = S5_T * S5_SEQ

    @pl.when(c == 0)
    def _():
        st_ref[...] = h0_ref[...]

    for d, u_ref in enumerate((uf_ref, ub_ref)):
        u = u_ref[...].reshape(rows, S5_WIDTH).astype(BF16)
        for k in range(S5_KBLK):
            uk = u[:, k * 128:(k + 1) * 128]
            bu_ref[d, :, k * 2 * S5_BLK:(k + 1) * 2 * S5_BLK] = jnp.dot(
                uk, wb_ref[d, k], preferred_element_type=F32)

    for k in range(S5_KBLK):
        for w0 in range(0, S5_BLK, S5_W):
            re0 = k * 2 * S5_BLK + w0
            im0 = re0 + S5_BLK
            a = [(a_ref[d, k, :, w0:w0 + S5_W], a_ref[d, k, :, S5_BLK + w0:S5_BLK + w0 + S5_W])
                 for d in range(2)]
            init = tuple(st_ref[d, k, :, o:o + S5_W] for d in range(2) for o in (w0, S5_BLK + w0))

            def body(i, carry, re0=re0, im0=im0, a=a):
                out = []
                for d in range(2):
                    t = i if d == 0 else S5_T - 1 - i
                    r0 = pl.multiple_of(t * S5_SEQ, S5_SEQ)
                    xr, xi = carry[2 * d], carry[2 * d + 1]
                    ar, ai = a[d]
                    nr = ar * xr - ai * xi + bu_ref[d, pl.ds(r0, S5_SEQ), re0:re0 + S5_W]
                    ni = ar * xi + ai * xr + bu_ref[d, pl.ds(r0, S5_SEQ), im0:im0 + S5_W]
                    bu_ref[d, pl.ds(r0, S5_SEQ), re0:re0 + S5_W] = nr
                    bu_ref[d, pl.ds(r0, S5_SEQ), im0:im0 + S5_W] = ni
                    out += [nr, ni]
                return tuple(out)

            fin = lax.fori_loop(0, S5_T, body, init, unroll=2)
            for d in range(2):
                st_ref[d, k, :, w0:w0 + S5_W] = fin[2 * d]
                st_ref[d, k, :, S5_BLK + w0:S5_BLK + w0 + S5_W] = fin[2 * d + 1]

    for d, y_ref in enumerate((yf_ref, yb_ref)):
        for k in range(S5_KBLK):
            xk = bu_ref[d, :, k * 2 * S5_BLK:(k + 1) * 2 * S5_BLK].astype(BF16)
            y = jnp.dot(xk, wc_ref[d, k], preferred_element_type=F32)
            y_ref[:, :, k * 128:(k + 1) * 128] = y.reshape(S5_T, S5_SEQ, 128)

    @pl.when(c == pl.num_programs(1) - 1)
    def _():
        hfin_ref[...] = st_ref[...]


def _s5_call(u_tm, wb, wc, a_bar, h0):
    length, nseq, _ = u_tm.shape
    ng, nc = nseq // S5_SEQ, length // S5_T
    ublk = (S5_T, S5_SEQ, S5_WIDTH)
    stblk = (None, 2, S5_KBLK, S5_SEQ, 2 * S5_BLK)
    y_shape = jax.ShapeDtypeStruct((length, nseq, S5_WIDTH), F32)
    return pl.pallas_call(
        _s5_kernel,
        grid=(ng, nc),
        in_specs=[
            pl.BlockSpec(ublk, lambda g, c: (c, g, 0)),
            pl.BlockSpec(ublk, lambda g, c: (nc - 1 - c, g, 0)),
            pl.BlockSpec((2, S5_KBLK, 128, 2 * S5_BLK), lambda g, c: (0, 0, 0, 0)),
            pl.BlockSpec((2, S5_KBLK, 2 * S5_BLK, 128), lambda g, c: (0, 0, 0, 0)),
            pl.BlockSpec((2, S5_KBLK, S5_SEQ, 2 * S5_BLK), lambda g, c: (0, 0, 0, 0)),
            pl.BlockSpec(stblk, lambda g, c: (g, 0, 0, 0, 0)),
        ],
        out_specs=[
            pl.BlockSpec(ublk, lambda g, c: (c, g, 0)),
            pl.BlockSpec(ublk, lambda g, c: (nc - 1 - c, g, 0)),
            pl.BlockSpec(stblk, lambda g, c: (g, 0, 0, 0, 0)),
        ],
        out_shape=(y_shape, y_shape, jax.ShapeDtypeStruct(h0.shape, F32)),
        scratch_shapes=[
            pltpu.VMEM((2, S5_T * S5_SEQ, S5_KBLK * 2 * S5_BLK), F32),
            pltpu.VMEM((2, S5_KBLK, S5_SEQ, 2 * S5_BLK), F32),
        ],
        compiler_params=pltpu.CompilerParams(
            dimension_semantics=("parallel", "arbitrary"), vmem_limit_bytes=VMEM_MID),
        name="s5_scan",
    )(u_tm, u_tm, wb, wc, a_bar, h0)


def _s5_weights(ab_re, ab_im, bb_re, bb_im, c_re, c_im):
    eye = jnp.eye(S5_GPB, dtype=F32)

    def in_side(bb):
        t = bb.reshape(S5_GROUP_CH, 2, S5_KBLK, S5_GPB, S5_STATE)
        t = jnp.einsum('cdkgp,gh->dkgchp', t, eye)
        return t.reshape(2, S5_KBLK, S5_GPB * S5_GROUP_CH, S5_BLK)

    def out_side(cc):
        t = cc.reshape(2, S5_KBLK, S5_GPB, S5_GROUP_CH, S5_STATE)
        t = jnp.einsum('dkgcp,gh->dkgphc', t, eye)
        return t.reshape(2, S5_KBLK, S5_BLK, S5_GPB * S5_GROUP_CH)

    wb = jnp.concatenate([in_side(bb_re), in_side(bb_im)], axis=-1).astype(BF16)
    wc = jnp.concatenate([out_side(c_re), -out_side(c_im)], axis=-2).astype(BF16)
    a = jnp.concatenate([ab_re.reshape(2, S5_KBLK, 1, S5_BLK), ab_im.reshape(2, S5_KBLK, 1, S5_BLK)], axis=-1)
    a = jnp.broadcast_to(a, (2, S5_KBLK, S5_SEQ, 2 * S5_BLK))
    return wb, wc, a


def _s5_state_in(s_re, s_im):
    def blk(s):
        return s.reshape(-1, 2, S5_KBLK, S5_BLK).transpose(1, 2, 0, 3)
    h = jnp.concatenate([blk(s_re), blk(s_im)], axis=-1)
    h = jnp.pad(h, ((0, 0), (0, 0), (0, S5_SEQ - h.shape[2]), (0, 0)))
    return h[None]


def _s5_state_out(h):
    def unblk(s):
        return s.transpose(0, 3, 1, 2, 4).reshape(-1, 2, S5_GROUPS, S5_STATE)
    return unblk(h[..., :S5_BLK]), unblk(h[..., S5_BLK:])


def _gelu_tanh(x):
    return x * (0.5 * (1.0 + jnp.tanh(math.sqrt(2.0 / math.pi) * (x + 0.044715 * (x * x * x)))))


def _mix_kernel(yf_ref, yb_ref, u_ref, bg_ref, cg_ref, v_ref, cgp_ref, vp_ref, cgn_ref, vn_ref,
                d_ref, gw_ref, gb_ref, cw_ref, cb_ref, ow_ref, x_ref, mod_ref, o_ref):
    tm = TM_MIX
    y = yf_ref[...] + yb_ref[...] + d_ref[...] * u_ref[...]
    z = _gelu_tanh(y)
    gate = jax.nn.sigmoid(jnp.dot(z.astype(BF16), gw_ref[...], preferred_element_type=F32) + gb_ref[...])
    a_out = z * gate

    p = cg_ref[...] * v_ref[...]
    local = lax.broadcasted_iota(jnp.int32, (tm, 1), 0)
    row = pl.program_id(0) * tm + local
    seq_len = jnp.where(row < TOK_CTX, L_CTX, L_LAT)
    pos = jnp.bitwise_and(row, seq_len - 1)
    p_prev = jnp.where(local == 0, cgp_ref[HALO - 1:HALO, :] * vp_ref[HALO - 1:HALO, :], pltpu.roll(p, 1, 0))
    p_prev = jnp.where(pos == 0, 0.0, p_prev)
    p_next = jnp.where(local == tm - 1, cgn_ref[0:1, :] * vn_ref[0:1, :], pltpu.roll(p, tm - 1, 0))
    p_next = jnp.where(pos == seq_len - 1, 0.0, p_next)
    conv = cw_ref[0:1, :] * p_prev + cw_ref[1:2, :] * p + cw_ref[2:3, :] * p_next + cb_ref[...]
    b_out = bg_ref[...] * conv

    out = (jnp.dot(a_out.astype(BF16), ow_ref[0:S5_WIDTH, :], preferred_element_type=F32)
           + jnp.dot(b_out.astype(BF16), ow_ref[S5_WIDTH:, :], preferred_element_type=F32))
    o_ref[...] = x_ref[...] + mod_ref[2:3, :] * out


def _mix_call(yf, yb, proj, s5_d, glu_w, glu_b, conv_w, conv_b, out_w, x, mod):
    tm = TM_MIX
    hb = tm // HALO
    last_halo = TOK // HALO - 1
    blk = lambda col: pl.BlockSpec((tm, CONV_W), lambda i: (i, col))
    row1 = pl.BlockSpec((1, CONV_W), lambda i: (0, 0))
    return pl.pallas_call(
        _mix_kernel,
        grid=(TOK // tm,),
        in_specs=[
            blk(0), blk(0), blk(0), blk(1), blk(2), blk(3),
            pl.BlockSpec((HALO, CONV_W), lambda i: (jnp.maximum(i * hb - 1, 0), 2)),
            pl.BlockSpec((HALO, CONV_W), lambda i: (jnp.maximum(i * hb - 1, 0), 3)),
            pl.BlockSpec((HALO, CONV_W), lambda i: (jnp.minimum((i + 1) * hb, last_halo), 2)),
            pl.BlockSpec((HALO, CONV_W), lambda i: (jnp.minimum((i + 1) * hb, last_halo), 3)),
            row1,
            pl.BlockSpec((S5_WIDTH, S5_WIDTH), lambda i: (0, 0)),
            row1,
            pl.BlockSpec((MOD_ROWS, CONV_W), lambda i: (0, 0)),
            row1,
            pl.BlockSpec((D_MODEL, D_MODEL), lambda i: (0, 0)),
            pl.BlockSpec((tm, D_MODEL), lambda i: (i, 0)),
            pl.BlockSpec((None, MOD_ROWS, D_MODEL), lambda i: (_cond_row(i, tm), 0, 0)),
        ],
        out_specs=pl.BlockSpec((tm, D_MODEL), lambda i: (i, 0)),
        out_shape=jax.ShapeDtypeStruct((TOK, D_MODEL), F32),
        compiler_params=pltpu.CompilerParams(
            dimension_semantics=("parallel",), vmem_limit_bytes=VMEM_MID),
        name="glu_conv_out",
    )(yf, yb, proj, proj, proj, proj, proj, proj, proj, proj,
      s5_d.reshape(1, -1), glu_w, glu_b.reshape(1, -1),
      jnp.pad(conv_w, ((0, MOD_ROWS - conv_w.shape[0]), (0, 0))), conv_b.reshape(1, -1), out_w, x, mod)


def _log_sigmoid(x):
    return jnp.minimum(x, 0.0) - jnp.log1p(jnp.exp(-jnp.abs(x)))


def _ret_kernel(*refs, n_chunks, latent):
    if latent:
        (q_ref, k_ref, v_ref, g_ref, gam_ref, gn_ref, cos_ref, sin_ref, s0_ref, _,
         o_ref, sfin_ref, qs_ref, kp_ref, kf_ref, kb_ref, sbk_ref, sf_ref, sb_ref) = refs
    else:
        (q_ref, k_ref, v_ref, g_ref, gam_ref, gn_ref,
         o_ref, sfin_ref, qs_ref, kp_ref, kf_ref, kb_ref, sbk_ref, sf_ref, sb_ref) = refs

    lg_f = _log_sigmoid(gam_ref[0, 0:1, :])
    lg_b = _log_sigmoid(gam_ref[1, 0:1, :])
    ii = lax.broadcasted_iota(jnp.int32, (RET_C, RET_C), 0)
    jj = lax.broadcasted_iota(jnp.int32, (RET_C, RET_C), 1)
    diff = (ii - jj).astype(F32)
    lgf_k, lgb_k = lg_f[:, :RET_DK], lg_b[:, :RET_DK]
    intra = (jnp.where(diff >= 0, jnp.exp(lgf_k * jnp.maximum(diff, 0.0)), 0.0)
             + jnp.where(diff <= 0, jnp.exp(lgb_k * jnp.maximum(-diff, 0.0)), 0.0))
    tk = lax.broadcasted_iota(jnp.int32, (RET_C, RET_DK), 0).astype(F32)
    tv = lax.broadcasted_iota(jnp.int32, (RET_C, RET_DV), 0).astype(F32)
    kdec_f = jnp.exp(lgf_k * (RET_C - 1.0 - tk))
    kdec_b = jnp.exp(lgb_k * tk)
    qdec_f = jnp.exp(lg_f * (tv + 1.0))
    qdec_b = jnp.exp(lg_b * (RET_C - tv))
    cd_f = jnp.exp(lg_f * RET_C)
    cd_b = jnp.exp(lg_b * RET_C)

    lane = lax.broadcasted_iota(jnp.int32, (RET_C, RET_DK), 1)
    first_half = jnp.bitwise_and(lane, 2 * (RET_DK // 4) - 1) < RET_DK // 4

    def rotate(x, cos, sin):
        swapped = jnp.where(first_half, pltpu.roll(x, RET_DK - RET_DK // 4, 1), pltpu.roll(x, RET_DK // 4, 1))
        return x * cos + swapped * sin

    def prep(c, _):
        r = pl.ds(pl.multiple_of(c * RET_C, RET_C), RET_C)
        q, k = q_ref[r, :], k_ref[r, :]
        if latent:
            cos, sin = cos_ref[r, :], sin_ref[r, :]
            q, k = rotate(q, cos, sin), rotate(k, cos, sin)
        qs_ref[r, :] = (q * (RET_DK ** -0.5)).astype(BF16)
        kp_ref[r, :] = k.astype(BF16)
        kf_ref[r, :] = (k * kdec_f).astype(BF16)
        kb_ref[r, :] = (k * kdec_b).astype(BF16)
        return 0

    lax.fori_loop(0, n_chunks, prep, 0)

    if latent:
        sf_ref[...] = s0_ref[0]
        sb_ref[...] = s0_ref[1]
    else:
        sf_ref[...] = jnp.zeros_like(sf_ref)
        sb_ref[...] = jnp.zeros_like(sb_ref)

    tn_dims = (((0,), (0,)), ((), ()))
    nt_dims = (((1,), (1,)), ((), ()))

    def back(i, _):
        c = n_chunks - 1 - i
        r = pl.ds(pl.multiple_of(c * RET_C, RET_C), RET_C)
        s = sb_ref[...]
        sbk_ref[c] = s.astype(BF16)
        kv = lax.dot_general(kb_ref[r, :], v_ref[r, :].astype(BF16), tn_dims, preferred_element_type=F32)
        sb_ref[...] = cd_b * s + kv
        return 0

    lax.fori_loop(0, n_chunks, back, 0)

    def fwd(c, _):
        r = pl.ds(pl.multiple_of(c * RET_C, RET_C), RET_C)
        qc, vc = qs_ref[r, :], v_ref[r, :].astype(BF16)
        scores = lax.dot_general(qc, kp_ref[r, :], nt_dims, preferred_element_type=F32) * intra
        s = sf_ref[...]
        o = (jnp.dot(scores.astype(BF16), vc, preferred_element_type=F32)
             + jnp.dot(qc, s.astype(BF16), preferred_element_type=F32) * qdec_f
             + jnp.dot(qc, sbk_ref[c], preferred_element_type=F32) * qdec_b)
        sf_ref[...] = cd_f * s + lax.dot_general(kf_ref[r, :], vc, tn_dims, preferred_element_type=F32)
        o = o * lax.rsqrt(jnp.mean(o * o, axis=-1, keepdims=True) + EPS) * gn_ref[...]
        o_ref[r, :] = (_silu(g_ref[r, :]) * o).astype(o_ref.dtype)
        return 0

    lax.fori_loop(0, n_chunks, fwd, 0)
    sfin_ref[0] = sf_ref[...]
    sfin_ref[1] = sb_ref[...]


def _ret_call(proj, gam, gn_w, latent, rope=None, s0=None, o_ctx=None):
    if latent:
        nseq, length, blk0 = N_LAT, L_LAT, TOK_CTX // L_LAT
    else:
        nseq, length, blk0 = N_CTX, L_CTX, 0
    n_chunks = length // RET_C
    kcol, vcol, gcol = RET_QK // RET_DK, 2 * RET_QK // RET_DV, (2 * RET_QK + RET_V) // RET_DV
    in_specs = [
        pl.BlockSpec((length, RET_DK), lambda b, h: (blk0 + b, h)),
        pl.BlockSpec((length, RET_DK), lambda b, h: (blk0 + b, kcol + h)),
        pl.BlockSpec((length, RET_DV), lambda b, h: (blk0 + b, vcol + h)),
        pl.BlockSpec((length, RET_DV), lambda b, h: (blk0 + b, gcol + h)),
        pl.BlockSpec((None, 2, 8, RET_DV), lambda b, h: (h, 0, 0, 0)),
        pl.BlockSpec((1, RET_DV), lambda b, h: (0, h)),
    ]
    args = [proj, proj, proj, proj, gam, gn_w.reshape(1, RET_V)]
    if latent:
        in_specs += [
            pl.BlockSpec((length, RET_DK), lambda b, h: (0, 0)),
            pl.BlockSpec((length, RET_DK), lambda b, h: (0, 0)),
            pl.BlockSpec((None, 2, None, RET_DK, RET_DV), lambda b, h: (b, 0, h, 0, 0)),
            pl.BlockSpec(memory_space=pl.ANY),
        ]
        args += [rope[0], rope[1], s0, o_ctx]
    return pl.pallas_call(
        functools.partial(_ret_kernel, n_chunks=n_chunks, latent=latent),
        grid=(nseq, RET_H),
        in_specs=in_specs,
        out_specs=[
            pl.BlockSpec((length, RET_DV), lambda b, h: (blk0 + b, h)),
            pl.BlockSpec((None, 2, None, RET_DK, RET_DV), lambda b, h: (b, 0, h, 0, 0)),
        ],
        out_shape=(jax.ShapeDtypeStruct((TOK, RET_V), BF16),
                   jax.ShapeDtypeStruct((nseq, 2, RET_H, RET_DK, RET_DV), F32)),
        scratch_shapes=[
            pltpu.VMEM((length, RET_DK), BF16),
            pltpu.VMEM((length, RET_DK), BF16),
            pltpu.VMEM((length, RET_DK), BF16),
            pltpu.VMEM((length, RET_DK), BF16),
            pltpu.VMEM((n_chunks, RET_DK, RET_DV), BF16),
            pltpu.VMEM((RET_DK, RET_DV), F32),
            pltpu.VMEM((RET_DK, RET_DV), F32),
        ],
        input_output_aliases={len(args) - 1: 0} if latent else {},
        compiler_params=pltpu.CompilerParams(
            dimension_semantics=("parallel", "arbitrary"), vmem_limit_bytes=VMEM_MID),
        name="retention_latent" if latent else "retention_context",
    )(*args)


def _rope_tables():
    n_freq = RET_DK // 4
    t = jnp.arange(L_LAT)
    row = (t // GRID_W).astype(F32)
    col = (t % GRID_W).astype(F32)
    inv_freq = jnp.power(ROPE_BASE, -jnp.arange(n_freq, dtype=F32) / n_freq)
    ar, ac = row[:, None] * inv_freq, col[:, None] * inv_freq
    cos = jnp.concatenate([jnp.cos(ar), jnp.cos(ar), jnp.cos(ac), jnp.cos(ac)], axis=-1)
    sin = jnp.concatenate([-jnp.sin(ar), jnp.sin(ar), -jnp.sin(ac), jnp.sin(ac)], axis=-1)
    return cos, sin


def kernel(x_prompt, x_sample, state_s5_re, state_s5_im, state_ret, c, c_ctx, norm1_w, norm2_w, ada_w, ada_b,
           hy_in_w, hy_out_w, s5_lam_re, s5_lam_im, s5_log_step, s5_b_re, s5_b_im, s5_c_re, s5_c_im,
           s5_d, s5_glu_w, s5_glu_b, conv_w, conv_b, ret_in_w, ret_out_w, ret_gamma_logit, ret_gn_w,
           mlp_w1, mlp_w2, final_norm_w):
    x = jnp.concatenate([x_prompt.reshape(TOK_CTX, D_MODEL), x_sample.reshape(TOK - TOK_CTX, D_MODEL)], axis=0)

    cond = jnp.concatenate([c_ctx[None, :], c], axis=0)
    cond = jnp.pad(cond, ((0, MOD_ROWS - N_COND), (0, 0)))
    mod_all = _ada_call(cond, ada_w, ada_b)
    mod_all = mod_all[:, :N_COND].reshape(DEPTH, N_COND, N_MOD, D_MODEL)
    mod_all = jnp.pad(mod_all, ((0, 0), (0, 0), (0, MOD_ROWS - N_MOD), (0, 0)))

    n_s5 = s5_lam_re.shape[0]
    rows = n_s5 * 2 * S5_GROUPS
    ab_re, ab_im, bb_re, bb_im = _disc_call(
        s5_lam_re.reshape(rows, S5_STATE), s5_lam_im.reshape(rows, S5_STATE),
        jnp.broadcast_to(s5_log_step.reshape(rows, 1), (rows, S5_STATE)),
        jnp.moveaxis(s5_b_re, -1, 0).reshape(S5_GROUP_CH, rows, S5_STATE),
        jnp.moveaxis(s5_b_im, -1, 0).reshape(S5_GROUP_CH, rows, S5_STATE))
    ab_re = ab_re.reshape(n_s5, 2, S5_GROUPS, S5_STATE)
    ab_im = ab_im.reshape(n_s5, 2, S5_GROUPS, S5_STATE)
    bb_re = bb_re.reshape(S5_GROUP_CH, n_s5, 2, S5_GROUPS, S5_STATE)
    bb_im = bb_im.reshape(S5_GROUP_CH, n_s5, 2, S5_GROUPS, S5_STATE)

    rope = _rope_tables()
    zero_state = jnp.zeros((N_CTX // S5_SEQ, 2, S5_KBLK, S5_SEQ, 2 * S5_BLK), F32)

    new_re, new_im, new_ret = [], [], []
    for i in range(DEPTH):
        j = i // 2
        mod = mod_all[i]
        if i % 2 == 0:
            proj = _in_call(x, mod, norm1_w[i], hy_in_w[j].astype(BF16))
            wb, wc, a_bar = _s5_weights(ab_re[j], ab_im[j], bb_re[:, j], bb_im[:, j], s5_c_re[j], s5_c_im[j])
            u_ctx = proj[:TOK_CTX, :S5_WIDTH].reshape(N_CTX, L_CTX, S5_WIDTH).transpose(1, 0, 2)
            u_lat = proj[TOK_CTX:, :S5_WIDTH].reshape(N_LAT, L_LAT, S5_WIDTH).transpose(1, 0, 2)
            u_lat = jnp.pad(u_lat, ((0, 0), (0, S5_SEQ - N_LAT), (0, 0)))
            yf_c, yb_c, h_ctx = _s5_call(u_ctx, wb, wc, a_bar, zero_state)
            yf_l, yb_l, _ = _s5_call(u_lat, wb, wc, a_bar,
                                     _s5_state_in(state_s5_re[:, j], state_s5_im[:, j]))

            def token_major(y_ctx, y_lat):
                return jnp.concatenate([
                    y_ctx.transpose(1, 0, 2).reshape(TOK_CTX, S5_WIDTH),
                    y_lat[:, :N_LAT].transpose(1, 0, 2).reshape(TOK - TOK_CTX, S5_WIDTH)], axis=0)

            x = _mix_call(token_major(yf_c, yf_l), token_major(yb_c, yb_l), proj, s5_d[j],
                          s5_glu_w[j].astype(BF16), s5_glu_b[j], conv_w[j], conv_b[j],
                          hy_out_w[j].astype(BF16), x, mod)
            s_re, s_im = _s5_state_out(h_ctx)
            new_re.append(s_re)
            new_im.append(s_im)
        else:
            proj = _in_call(x, mod, norm1_w[i], ret_in_w[j].astype(BF16))
            gam = jnp.broadcast_to(ret_gamma_logit[j].T[:, :, None, None], (RET_H, 2, 8, RET_DV))
            o_ctx, s_ctx = _ret_call(proj, gam, ret_gn_w[j], latent=False)
            o, _ = _ret_call(proj, gam, ret_gn_w[j], latent=True, rope=rope, s0=state_ret[:, j], o_ctx=o_ctx)
            x = _resid_call(o, ret_out_w[j].astype(BF16), x, mod)
            new_ret.append(s_ctx)
        x = _mlp_call(x, mod, norm2_w[i], mlp_w1[i].astype(BF16), mlp_w2[i].astype(BF16),
                      final_norm_w, final=(i == DEPTH - 1))

    y_prompt = x[:TOK_CTX].reshape(N_CTX, L_CTX, D_MODEL)
    y_sample = x[TOK_CTX:].reshape(N_LAT, L_LAT, D_MODEL)
    return (y_prompt, y_sample, jnp.stack(new_re, 1), jnp.stack(new_im, 1), jnp.stack(new_ret, 1))
```

```python
import functools
import math

import jax
import jax.numpy as jnp
from jax import lax
from jax.experimental import pallas as pl
from jax.experimental.pallas import tpu as pltpu

F32 = jnp.float32
BF16 = jnp.bfloat16

D_MODEL = 1024
DEPTH = 4
N_CTX, L_CTX = 32, 256
N_LAT, L_LAT = 4, 2048
TOK_CTX = N_CTX * L_CTX
TOK = TOK_CTX + N_LAT * L_LAT
N_COND = 1 + N_LAT
GRID_W = 64
EPS = 1e-6

S5_WIDTH = 512
S5_GROUP_CH = 16
S5_GROUPS = 32
S5_STATE = 64
S5_KBLK = 4
S5_GPB = S5_GROUPS // S5_KBLK
S5_BLK = S5_GPB * S5_STATE
S5_SLAB = 128
S5_QUARTERS = S5_BLK // S5_SLAB
S5_ROWS = 512
S5_UNROLL = 4
SUBLANES = 8
CONV_W = 512
HY_IN = S5_WIDTH + 3 * CONV_W

RET_H = 8
RET_DK = 128
RET_DV = 256
RET_QK = RET_H * RET_DK
RET_V = RET_H * RET_DV
RET_IN = 2 * RET_QK + 2 * RET_V
RET_C = 128
RET_ROWS = 2048
RET_NC = RET_ROWS // RET_C
RET_UNROLL = 2
ROPE_BASE = 10000.0
MLP_H = 4 * D_MODEL
N_MOD = 6
MOD_ROWS = 8

VMEM_BIG = 56 * 1024 * 1024
VMEM_MID = 40 * 1024 * 1024

TM_DENSE = 1024
TN_DENSE = 1024
TH_MLP = 1024
TM_MIX = 512
HALO = 8


def _cond_row(i, tm):
    row0 = i * tm
    return jnp.where(row0 < TOK_CTX, 0, 1 + (row0 - TOK_CTX) // L_LAT)


def _norm_mod(x, nw, scale, shift):
    y = x * lax.rsqrt(jnp.mean(x * x, axis=-1, keepdims=True) + EPS)
    return (y * nw) * (1.0 + scale) + shift


def _silu(x):
    return x * jax.nn.sigmoid(x)


def _ada_kernel(c_ref, w_ref, b_ref, o_ref):
    sc = _silu(c_ref[...]).astype(BF16)
    o_ref[0] = jnp.dot(sc, w_ref[0].astype(BF16), preferred_element_type=F32) + b_ref[0]


def _ada_call(cond, ada_w, ada_b):
    tn = 1536
    return pl.pallas_call(
        _ada_kernel,
        grid=(DEPTH, N_MOD * D_MODEL // tn),
        in_specs=[
            pl.BlockSpec((MOD_ROWS, D_MODEL), lambda l, j: (0, 0)),
            pl.BlockSpec((1, D_MODEL, tn), lambda l, j: (l, 0, j)),
            pl.BlockSpec((1, 1, tn), lambda l, j: (l, 0, j)),
        ],
        out_specs=pl.BlockSpec((1, MOD_ROWS, tn), lambda l, j: (l, 0, j)),
        out_shape=jax.ShapeDtypeStruct((DEPTH, MOD_ROWS, N_MOD * D_MODEL), F32),
        compiler_params=pltpu.CompilerParams(
            dimension_semantics=("arbitrary", "arbitrary"), vmem_limit_bytes=VMEM_MID),
        name="ada_mod",
    )(cond, ada_w, ada_b.reshape(DEPTH, 1, N_MOD * D_MODEL))


def _in_kernel(x_ref, mod_ref, nw_ref, w_ref, o_ref, h_ref, wb_ref):
    i, j = pl.program_id(0), pl.program_id(1)

    @pl.when(j == 0)
    def _():
        h = _norm_mod(x_ref[...], nw_ref[...], mod_ref[1:2, :], mod_ref[0:1, :])
        h_ref[...] = h.astype(BF16)

    @pl.when(i == 0)
    def _():
        wb_ref[j] = w_ref[...].astype(BF16)

    o_ref[...] = jnp.dot(h_ref[...], wb_ref[j], preferred_element_type=F32)


def _in_call(x, mod, nw, w):
    n = w.shape[1]
    tm, tn = TM_DENSE, TN_DENSE
    nt = n // tn
    return pl.pallas_call(
        _in_kernel,
        grid=(TOK // tm, nt),
        in_specs=[
            pl.BlockSpec((tm, D_MODEL), lambda i, j: (i, 0)),
            pl.BlockSpec((None, MOD_ROWS, D_MODEL), lambda i, j: (_cond_row(i, tm), 0, 0)),
            pl.BlockSpec((1, D_MODEL), lambda i, j: (0, 0)),
            pl.BlockSpec((D_MODEL, tn), lambda i, j: (0, jnp.where(i == 0, j, nt - 1))),
        ],
        out_specs=pl.BlockSpec((tm, tn), lambda i, j: (i, j)),
        out_shape=jax.ShapeDtypeStruct((TOK, n), F32),
        scratch_shapes=[pltpu.VMEM((tm, D_MODEL), BF16), pltpu.VMEM((nt, D_MODEL, tn), BF16)],
        compiler_params=pltpu.CompilerParams(
            dimension_semantics=("arbitrary", "arbitrary"), vmem_limit_bytes=VMEM_BIG),
        name="norm_proj",
    )(x, mod, nw.reshape(1, D_MODEL), w)


def _mlp_body(x_ref, mod_ref, nw_ref, w1_ref, w2_ref, h_ref, acc_ref):
    j = pl.program_id(1)

    @pl.when(j == 0)
    def _():
        h = _norm_mod(x_ref[...], nw_ref[...], mod_ref[4:5, :], mod_ref[3:4, :])
        h_ref[...] = h.astype(BF16)
        acc_ref[...] = jnp.zeros_like(acc_ref)

    a = jnp.maximum(jnp.dot(h_ref[...], w1_ref[...].astype(BF16), preferred_element_type=F32), 0.0)
    acc_ref[...] += jnp.dot((a * a).astype(BF16), w2_ref[...].astype(BF16), preferred_element_type=F32)


def _mlp_kernel(x_ref, mod_ref, nw_ref, w1_ref, w2_ref, o_ref, h_ref, acc_ref):
    _mlp_body(x_ref, mod_ref, nw_ref, w1_ref, w2_ref, h_ref, acc_ref)

    @pl.when(pl.program_id(1) == pl.num_programs(1) - 1)
    def _():
        o_ref[...] = x_ref[...] + mod_ref[5:6, :] * acc_ref[...]


def _mlp_final_kernel(x_ref, mod_ref, nw_ref, w1_ref, w2_ref, fw_ref, oc_ref, ol_ref, h_ref, acc_ref, *, tm):
    _mlp_body(x_ref, mod_ref, nw_ref, w1_ref, w2_ref, h_ref, acc_ref)
    i = pl.program_id(0)
    last = pl.program_id(1) == pl.num_programs(1) - 1

    def result():
        y = x_ref[...] + mod_ref[5:6, :] * acc_ref[...]
        return y * lax.rsqrt(jnp.mean(y * y, axis=-1, keepdims=True) + EPS) * fw_ref[...]

    @pl.when(jnp.logical_and(last, i < TOK_CTX // tm))
    def _():
        oc_ref[...] = result()

    @pl.when(jnp.logical_and(last, i >= TOK_CTX // tm))
    def _():
        ol_ref[...] = result()


def _mlp_call(x, mod, nw, w1, w2, fw=None):
    tm = TM_DENSE
    th = TH_MLP if fw is None else TH_MLP // 2
    n_ctx_tiles = TOK_CTX // tm
    in_specs = [
        pl.BlockSpec((tm, D_MODEL), lambda i, j: (i, 0)),
        pl.BlockSpec((None, MOD_ROWS, D_MODEL), lambda i, j: (_cond_row(i, tm), 0, 0)),
        pl.BlockSpec((1, D_MODEL), lambda i, j: (0, 0)),
        pl.BlockSpec((D_MODEL, th), lambda i, j: (0, j)),
        pl.BlockSpec((th, D_MODEL), lambda i, j: (j, 0)),
    ]
    scratch = [pltpu.VMEM((tm, D_MODEL), BF16), pltpu.VMEM((tm, D_MODEL), F32)]
    args = [x, mod, nw.reshape(1, D_MODEL), w1, w2]
    if fw is None:
        return pl.pallas_call(
            _mlp_kernel,
            grid=(TOK // tm, MLP_H // th),
            in_specs=in_specs,
            out_specs=pl.BlockSpec((tm, D_MODEL), lambda i, j: (i, 0)),
            out_shape=jax.ShapeDtypeStruct((TOK, D_MODEL), F32),
            scratch_shapes=scratch,
            compiler_params=pltpu.CompilerParams(
                dimension_semantics=("parallel", "arbitrary"), vmem_limit_bytes=VMEM_BIG),
            name="mlp",
        )(*args)
    return pl.pallas_call(
        functools.partial(_mlp_final_kernel, tm=tm),
        grid=(TOK // tm, MLP_H // th),
        in_specs=in_specs + [pl.BlockSpec((1, D_MODEL), lambda i, j: (0, 0))],
        out_specs=[
            pl.BlockSpec((tm, D_MODEL), lambda i, j: (jnp.minimum(i, n_ctx_tiles - 1), 0)),
            pl.BlockSpec((tm, D_MODEL), lambda i, j: (jnp.maximum(i - n_ctx_tiles, 0), 0)),
        ],
        out_shape=(jax.ShapeDtypeStruct((TOK_CTX, D_MODEL), F32),
                   jax.ShapeDtypeStruct((TOK - TOK_CTX, D_MODEL), F32)),
        scratch_shapes=scratch,
        compiler_params=pltpu.CompilerParams(
            dimension_semantics=("arbitrary", "arbitrary"), vmem_limit_bytes=VMEM_BIG),
        name="mlp_final",
    )(*args, fw.reshape(1, D_MODEL))


def _resid_kernel(a_ref, w_ref, x_ref, mod_ref, o_ref, wb_ref):
    @pl.when(pl.program_id(0) == 0)
    def _():
        wb_ref[...] = w_ref[...].astype(BF16)

    y = jnp.dot(a_ref[...], wb_ref[...], preferred_element_type=F32)
    o_ref[...] = x_ref[...] + mod_ref[2:3, :] * y


def _resid_call(a, w, x, mod):
    tm = TM_DENSE
    k = a.shape[1]
    return pl.pallas_call(
        _resid_kernel,
        grid=(TOK // tm,),
        in_specs=[
            pl.BlockSpec((tm, k), lambda i: (i, 0)),
            pl.BlockSpec((k, D_MODEL), lambda i: (0, 0)),
            pl.BlockSpec((tm, D_MODEL), lambda i: (i, 0)),
            pl.BlockSpec((None, MOD_ROWS, D_MODEL), lambda i: (_cond_row(i, tm), 0, 0)),
        ],
        out_specs=pl.BlockSpec((tm, D_MODEL), lambda i: (i, 0)),
        out_shape=jax.ShapeDtypeStruct((TOK, D_MODEL), F32),
        scratch_shapes=[pltpu.VMEM((k, D_MODEL), BF16)],
        compiler_params=pltpu.CompilerParams(
            dimension_semantics=("arbitrary",), vmem_limit_bytes=VMEM_BIG),
        name="out_proj_resid",
    )(a, w, x, mod)


def _disc_kernel(lr_ref, li_ref, ls_ref, br_ref, bi_ref, abr_ref, abi_ref, bbr_ref, bbi_ref):
    lr, li = lr_ref[...], li_ref[...]
    dt = jnp.exp(ls_ref[...])
    mag = jnp.exp(lr * dt)
    abr = mag * jnp.cos(li * dt)
    abi = mag * jnp.sin(li * dt)
    den = lr * lr + li * li
    fr = ((abr - 1.0) * lr + abi * li) / den
    fi = (abi * lr - (abr - 1.0) * li) / den
    abr_ref[...] = abr
    abi_ref[...] = abi
    for c in range(S5_GROUP_CH):
        br, bi = br_ref[c], bi_ref[c]
        bbr_ref[c] = fr * br - fi * bi
        bbi_ref[c] = fr * bi + fi * br


def _disc_call(lam_re, lam_im, log_step, b_re, b_im):
    r = lam_re.shape[0]
    small = jax.ShapeDtypeStruct((r, S5_STATE), F32)
    big = jax.ShapeDtypeStruct((S5_GROUP_CH, r, S5_STATE), F32)
    return pl.pallas_call(_disc_kernel, out_shape=(small, small, big, big), name="s5_discretise")(
        lam_re, lam_im, log_step, b_re, b_im)


def _s5_kernel(*refs, ns, steps, aliased):
    uf_ref, ub_ref, wb_ref, wc_ref, a_ref, h0_ref = refs[:6]
    yf_ref, yb_ref, hfin_ref, bu_ref, xs_ref, st_ref = refs[6 + (2 if aliased else 0):]
    c = pl.program_id(1)
    rows = ns * steps
    slabs_per_k = 2 * S5_QUARTERS

    def seq_rows(b):
        return pl.ds(b, steps, stride=SUBLANES)

    @pl.when(c == 0)
    def _():
        st_ref[...] = h0_ref[...]

    u_bf = [u_ref[...].reshape(rows, S5_WIDTH).astype(BF16) for u_ref in (uf_ref, ub_ref)]
    y_refs = (yf_ref, yb_ref)

    for k in range(S5_KBLK):
        for d in range(2):
            res = jnp.dot(u_bf[d][:, k * 128:(k + 1) * 128], wb_ref[d, k], preferred_element_type=F32)
            for b in range(ns):
                for s in range(slabs_per_k):
                    bu_ref[d, s, seq_rows(b), :] = res[b * steps:(b + 1) * steps, s * S5_SLAB:(s + 1) * S5_SLAB]

        a = [[(a_ref[d, k, 0:ns, q * S5_SLAB:(q + 1) * S5_SLAB],
               a_ref[d, k, 0:ns, S5_BLK + q * S5_SLAB:S5_BLK + (q + 1) * S5_SLAB])
              for q in range(S5_QUARTERS)] for d in range(2)]
        init = tuple(st_ref[d, k, 0:ns, o:o + S5_SLAB]
                     for d in range(2) for q in range(S5_QUARTERS)
                     for o in (q * S5_SLAB, S5_BLK + q * S5_SLAB))

        def body(i, carry, a=a):
            out = []
            for d in range(2):
                t = i if d == 0 else steps - 1 - i
                tile = pl.ds(pl.multiple_of(t * SUBLANES, SUBLANES), ns)
                for q in range(S5_QUARTERS):
                    xr, xi = carry[2 * (d * S5_QUARTERS + q)], carry[2 * (d * S5_QUARTERS + q) + 1]
                    ar, ai = a[d][q]
                    nr = ar * xr - ai * xi + bu_ref[d, q, tile, :]
                    ni = ar * xi + ai * xr + bu_ref[d, S5_QUARTERS + q, tile, :]
                    xs_ref[d, q, tile, :] = nr
                    xs_ref[d, S5_QUARTERS + q, tile, :] = ni
                    out += [nr, ni]
            return tuple(out)

        fin = lax.fori_loop(0, steps, body, init, unroll=S5_UNROLL)
        for d in range(2):
            for q in range(S5_QUARTERS):
                st_ref[d, k, 0:ns, q * S5_SLAB:(q + 1) * S5_SLAB] = fin[2 * (d * S5_QUARTERS + q)]
                st_ref[d, k, 0:ns, S5_BLK + q * S5_SLAB:S5_BLK + (q + 1) * S5_SLAB] = (
                    fin[2 * (d * S5_QUARTERS + q) + 1])

        for d in range(2):
            xk = jnp.concatenate([
                jnp.concatenate([xs_ref[d, s, seq_rows(b), :] for s in range(slabs_per_k)], axis=1)
                for b in range(ns)], axis=0).astype(BF16)
            y = jnp.dot(xk, wc_ref[d, k], preferred_element_type=F32)
            y_refs[d][:, :, k * 128:(k + 1) * 128] = y.reshape(ns, steps, 128)

    @pl.when(c == pl.num_programs(1) - 1)
    def _():
        hfin_ref[...] = st_ref[...]


def _s5_call(proj, wb, wc, a_bar, h0, latent, y_prev=None):
    if latent:
        length, ns, blk0 = L_LAT, N_LAT, TOK_CTX // (N_LAT * L_LAT)
    else:
        length, ns, blk0 = L_CTX, SUBLANES, 0
    steps = S5_ROWS // ns
    ng, nc = h0.shape[0], length // steps
    ublk = (ns, steps, S5_WIDTH)
    stblk = (None, 2, S5_KBLK, SUBLANES, 2 * S5_BLK)
    fwd = lambda g, c: (blk0 + g, c, 0)
    bwd = lambda g, c: (blk0 + g, nc - 1 - c, 0)
    const4 = lambda g, c: (0, 0, 0, 0)
    in_specs = [
        pl.BlockSpec(ublk, fwd),
        pl.BlockSpec(ublk, bwd),
        pl.BlockSpec((2, S5_KBLK, 128, 2 * S5_BLK), const4),
        pl.BlockSpec((2, S5_KBLK, 2 * S5_BLK, 128), const4),
        pl.BlockSpec((2, S5_KBLK, SUBLANES, 2 * S5_BLK), const4),
        pl.BlockSpec(stblk, lambda g, c: (g, 0, 0, 0, 0)),
    ]
    proj3 = proj.reshape(TOK // length, length, HY_IN)
    args = [proj3, proj3, wb, wc, a_bar, h0]
    aliases = {}
    if y_prev is not None:
        in_specs += [pl.BlockSpec(memory_space=pl.ANY)] * 2
        args += [y.reshape(TOK // length, length, S5_WIDTH) for y in y_prev]
        aliases = {6: 0, 7: 1}
    y_shape = jax.ShapeDtypeStruct((TOK // length, length, S5_WIDTH), F32)
    yf, yb, hfin = pl.pallas_call(
        functools.partial(_s5_kernel, ns=ns, steps=steps, aliased=y_prev is not None),
        grid=(ng, nc),
        in_specs=in_specs,
        out_specs=[
            pl.BlockSpec(ublk, fwd),
            pl.BlockSpec(ublk, bwd),
            pl.BlockSpec(stblk, lambda g, c: (g, 0, 0, 0, 0)),
        ],
        out_shape=(y_shape, y_shape, jax.ShapeDtypeStruct(h0.shape, F32)),
        scratch_shapes=[
            pltpu.VMEM((2, 2 * S5_QUARTERS, steps * SUBLANES, S5_SLAB), F32),
            pltpu.VMEM((2, 2 * S5_QUARTERS, steps * SUBLANES, S5_SLAB), F32),
            pltpu.VMEM((2, S5_KBLK, SUBLANES, 2 * S5_BLK), F32),
        ],
        input_output_aliases=aliases,
        compiler_params=pltpu.CompilerParams(
            dimension_semantics=("parallel", "arbitrary"), vmem_limit_bytes=VMEM_BIG),
        name="s5_scan_latent" if latent else "s5_scan_context",
    )(*args)
    return yf.reshape(TOK, S5_WIDTH), yb.reshape(TOK, S5_WIDTH), hfin


def _s5_weights(ab_re, ab_im, bb_re, bb_im, c_re, c_im):
    eye = jnp.eye(S5_GPB, dtype=F32)

    def in_side(bb):
        t = bb.reshape(S5_GROUP_CH, 2, S5_KBLK, S5_GPB, S5_STATE)
        t = jnp.einsum('cdkgp,gh->dkgchp', t, eye)
        return t.reshape(2, S5_KBLK, S5_GPB * S5_GROUP_CH, S5_BLK)

    def out_side(cc):
        t = cc.reshape(2, S5_KBLK, S5_GPB, S5_GROUP_CH, S5_STATE)
        t = jnp.einsum('dkgcp,gh->dkgphc', t, eye)
        return t.reshape(2, S5_KBLK, S5_BLK, S5_GPB * S5_GROUP_CH)

    wb = jnp.concatenate([in_side(bb_re), in_side(bb_im)], axis=-1).astype(BF16)
    wc = jnp.concatenate([out_side(c_re), -out_side(c_im)], axis=-2).astype(BF16)
    a = jnp.concatenate([ab_re.reshape(2, S5_KBLK, 1, S5_BLK), ab_im.reshape(2, S5_KBLK, 1, S5_BLK)], axis=-1)
    a = jnp.broadcast_to(a, (2, S5_KBLK, SUBLANES, 2 * S5_BLK))
    return wb, wc, a


def _s5_state_in(s_re, s_im):
    def blk(s):
        return s.reshape(-1, 2, S5_KBLK, S5_BLK).transpose(1, 2, 0, 3)
    h = jnp.concatenate([blk(s_re), blk(s_im)], axis=-1)
    h = jnp.pad(h, ((0, 0), (0, 0), (0, SUBLANES - h.shape[2]), (0, 0)))
    return h[None]


def _s5_state_out(h):
    def unblk(s):
        return s.transpose(0, 3, 1, 2, 4).reshape(-1, 2, S5_GROUPS, S5_STATE)
    return unblk(h[..., :S5_BLK]), unblk(h[..., S5_BLK:])


def _gelu_tanh(x):
    return x * (0.5 * (1.0 + jnp.tanh(math.sqrt(2.0 / math.pi) * (x + 0.044715 * (x * x * x)))))


def _mix_kernel(yf_ref, yb_ref, u_ref, bg_ref, cg_ref, v_ref, cgp_ref, vp_ref, cgn_ref, vn_ref,
                d_ref, gw_ref, gb_ref, cw_ref, cb_ref, ow_ref, x_ref, mod_ref, o_ref, gwb_ref, owb_ref):
    tm = TM_MIX

    @pl.when(pl.program_id(0) == 0)
    def _():
        gwb_ref[...] = gw_ref[...].astype(BF16)
        owb_ref[...] = ow_ref[...].astype(BF16)

    y = yf_ref[...] + yb_ref[...] + d_ref[...] * u_ref[...]
    z = _gelu_tanh(y)
    gate = jax.nn.sigmoid(jnp.dot(z.astype(BF16), gwb_ref[...], preferred_element_type=F32) + gb_ref[...])
    a_out = z * gate

    p = cg_ref[...] * v_ref[...]
    local = lax.broadcasted_iota(jnp.int32, (tm, 1), 0)
    row = pl.program_id(0) * tm + local
    seq_len = jnp.where(row < TOK_CTX, L_CTX, L_LAT)
    pos = jnp.bitwise_and(row, seq_len - 1)
    p_prev = jnp.where(local == 0, cgp_ref[HALO - 1:HALO, :] * vp_ref[HALO - 1:HALO, :], pltpu.roll(p, 1, 0))
    p_prev = jnp.where(pos == 0, 0.0, p_prev)
    p_next = jnp.where(local == tm - 1, cgn_ref[0:1, :] * vn_ref[0:1, :], pltpu.roll(p, tm - 1, 0))
    p_next = jnp.where(pos == seq_len - 1, 0.0, p_next)
    conv = cw_ref[0:1, :] * p_prev + cw_ref[1:2, :] * p + cw_ref[2:3, :] * p_next + cb_ref[...]
    b_out = bg_ref[...] * conv

    out = (jnp.dot(a_out.astype(BF16), owb_ref[0:S5_WIDTH, :], preferred_element_type=F32)
           + jnp.dot(b_out.astype(BF16), owb_ref[S5_WIDTH:, :], preferred_element_type=F32))
    o_ref[...] = x_ref[...] + mod_ref[2:3, :] * out


def _mix_call(yf, yb, proj, s5_d, glu_w, glu_b, conv_w, conv_b, out_w, x, mod):
    tm = TM_MIX
    hb = tm // HALO
    last_halo = TOK // HALO - 1
    blk = lambda col: pl.BlockSpec((tm, CONV_W), lambda i: (i, col))
    row1 = pl.BlockSpec((1, CONV_W), lambda i: (0, 0))
    return pl.pallas_call(
        _mix_kernel,
        grid=(TOK // tm,),
        in_specs=[
            blk(0), blk(0), blk(0), blk(1), blk(2), blk(3),
            pl.BlockSpec((HALO, CONV_W), lambda i: (jnp.maximum(i * hb - 1, 0), 2)),
            pl.BlockSpec((HALO, CONV_W), lambda i: (jnp.maximum(i * hb - 1, 0), 3)),
            pl.BlockSpec((HALO, CONV_W), lambda i: (jnp.minimum((i + 1) * hb, last_halo), 2)),
            pl.BlockSpec((HALO, CONV_W), lambda i: (jnp.minimum((i + 1) * hb, last_halo), 3)),
            row1,
            pl.BlockSpec((S5_WIDTH, S5_WIDTH), lambda i: (0, 0)),
            row1,
            pl.BlockSpec((MOD_ROWS, CONV_W), lambda i: (0, 0)),
            row1,
            pl.BlockSpec((D_MODEL, D_MODEL), lambda i: (0, 0)),
            pl.BlockSpec((tm, D_MODEL), lambda i: (i, 0)),
            pl.BlockSpec((None, MOD_ROWS, D_MODEL), lambda i: (_cond_row(i, tm), 0, 0)),
        ],
        out_specs=pl.BlockSpec((tm, D_MODEL), lambda i: (i, 0)),
        out_shape=jax.ShapeDtypeStruct((TOK, D_MODEL), F32),
        scratch_shapes=[pltpu.VMEM((S5_WIDTH, S5_WIDTH), BF16), pltpu.VMEM((D_MODEL, D_MODEL), BF16)],
        compiler_params=pltpu.CompilerParams(
            dimension_semantics=("arbitrary",), vmem_limit_bytes=VMEM_MID),
        name="glu_conv_out",
    )(yf, yb, proj, proj, proj, proj, proj, proj, proj, proj,
      s5_d.reshape(1, -1), glu_w, glu_b.reshape(1, -1),
      jnp.pad(conv_w, ((0, MOD_ROWS - conv_w.shape[0]), (0, 0))), conv_b.reshape(1, -1), out_w, x, mod)


def _log_sigmoid(x):
    return jnp.minimum(x, 0.0) - jnp.log1p(jnp.exp(-jnp.abs(x)))


def _ret_kernel(*refs, latent, aliased):
    q_ref, k_ref, v_ref, g_ref, gam_ref, gn_ref = refs[:6]
    n_in = 6
    if latent:
        cos_ref, sin_ref, s0_ref = refs[6:9]
        n_in = 9
    n_in += aliased
    o_ref, sfin_ref, qs_ref, kp_ref, vb_ref, kv_ref, sk_ref = refs[n_in:]
    seq_chunks = RET_NC if latent else L_CTX // RET_C

    lg_f = _log_sigmoid(gam_ref[0, 0:1, :])
    lg_b = _log_sigmoid(gam_ref[1, 0:1, :])
    ii = lax.broadcasted_iota(jnp.int32, (RET_C, RET_C), 0)
    jj = lax.broadcasted_iota(jnp.int32, (RET_C, RET_C), 1)
    diff = (ii - jj).astype(F32)
    lgf_k, lgb_k = lg_f[:, :RET_DK], lg_b[:, :RET_DK]
    intra = (jnp.where(diff >= 0, jnp.exp(lgf_k * jnp.maximum(diff, 0.0)), 0.0)
             + jnp.where(diff <= 0, jnp.exp(lgb_k * jnp.maximum(-diff, 0.0)), 0.0))
    tk = lax.broadcasted_iota(jnp.int32, (RET_C, RET_DK), 0).astype(F32)
    tv = lax.broadcasted_iota(jnp.int32, (RET_C, RET_DV), 0).astype(F32)
    kdec_f = jnp.exp(lgf_k * (RET_C - 1.0 - tk))
    kdec_b = jnp.exp(lgb_k * tk)
    qdec_f = jnp.exp(lg_f * (tv + 1.0))
    qdec_b = jnp.exp(lg_b * (RET_C - tv))
    cd_f = jnp.exp(lg_f * RET_C)
    cd_b = jnp.exp(lg_b * RET_C)

    lane = lax.broadcasted_iota(jnp.int32, (RET_C, RET_DK), 1)
    first_half = jnp.bitwise_and(lane, 2 * (RET_DK // 4) - 1) < RET_DK // 4

    def rotate(x, cos, sin):
        swapped = jnp.where(first_half, pltpu.roll(x, RET_DK - RET_DK // 4, 1), pltpu.roll(x, RET_DK // 4, 1))
        return x * cos + swapped * sin

    tn_dims = (((0,), (0,)), ((), ()))
    nt_dims = (((1,), (1,)), ((), ()))

    def prep(c, _):
        r = pl.ds(pl.multiple_of(c * RET_C, RET_C), RET_C)
        q, k = q_ref[r, :], k_ref[r, :]
        if latent:
            cos, sin = cos_ref[r, :], sin_ref[r, :]
            q, k = rotate(q, cos, sin), rotate(k, cos, sin)
        qs_ref[r, :] = (q * (RET_DK ** -0.5)).astype(BF16)
        kp_ref[r, :] = k.astype(BF16)
        kd = jnp.concatenate([(k * kdec_f).astype(BF16), (k * kdec_b).astype(BF16)], axis=1)
        vb = v_ref[r, :].astype(BF16)
        vb_ref[r, :] = vb
        kv_ref[c] = lax.dot_general(kd, vb, tn_dims, preferred_element_type=F32)
        return 0

    lax.fori_loop(0, RET_NC, prep, 0, unroll=RET_UNROLL)

    zeros = jnp.zeros((RET_DK, RET_DV), F32)
    s = s0_ref[0] if latent else zeros
    for c in range(RET_NC):
        if c % seq_chunks == 0 and not (latent and c == 0):
            s = zeros
        sk_ref[c, :, 0:RET_DV] = s.astype(BF16)
        s = cd_f * s + kv_ref[c, 0:RET_DK, :]
        if (c + 1) % seq_chunks == 0:
            sfin_ref[c // seq_chunks, 0] = s
    s = s0_ref[1] if latent else zeros
    for c in reversed(range(RET_NC)):
        if (c + 1) % seq_chunks == 0 and not (latent and c == RET_NC - 1):
            s = zeros
        sk_ref[c, :, RET_DV:2 * RET_DV] = s.astype(BF16)
        s = cd_b * s + kv_ref[c, RET_DK:2 * RET_DK, :]
        if c % seq_chunks == 0:
            sfin_ref[c // seq_chunks, 1] = s

    def emit(c, _):
        r = pl.ds(pl.multiple_of(c * RET_C, RET_C), RET_C)
        qc = qs_ref[r, :]
        scores = lax.dot_general(qc, kp_ref[r, :], nt_dims, preferred_element_type=F32) * intra
        cross = jnp.dot(qc, sk_ref[c], preferred_element_type=F32)
        o = (jnp.dot(scores.astype(BF16), vb_ref[r, :], preferred_element_type=F32)
             + cross[:, 0:RET_DV] * qdec_f + cross[:, RET_DV:2 * RET_DV] * qdec_b)
        o = o * lax.rsqrt(jnp.mean(o * o, axis=-1, keepdims=True) + EPS) * gn_ref[...]
        o_ref[r, :] = (_silu(g_ref[r, :]) * o).astype(o_ref.dtype)
        return 0

    lax.fori_loop(0, RET_NC, emit, 0, unroll=RET_UNROLL)


def _ret_call(proj, gam, gn_w, latent, layer, rope=None, s0=None, o_prev=None, s_prev=None):
    nblk = TOK_CTX // RET_ROWS
    blk0 = nblk if latent else 0
    seqs = 1 if latent else RET_ROWS // L_CTX
    kcol, vcol, gcol = RET_QK // RET_DK, 2 * RET_QK // RET_DV, (2 * RET_QK + RET_V) // RET_DV
    in_specs = [
        pl.BlockSpec((RET_ROWS, RET_DK), lambda b, h: (blk0 + b, h)),
        pl.BlockSpec((RET_ROWS, RET_DK), lambda b, h: (blk0 + b, kcol + h)),
        pl.BlockSpec((RET_ROWS, RET_DV), lambda b, h: (blk0 + b, vcol + h)),
        pl.BlockSpec((RET_ROWS, RET_DV), lambda b, h: (blk0 + b, gcol + h)),
        pl.BlockSpec((None, 2, SUBLANES, RET_DV), lambda b, h: (h, 0, 0, 0)),
        pl.BlockSpec((1, RET_DV), lambda b, h: (0, h)),
    ]
    args = [proj, proj, proj, proj, gam, gn_w.reshape(1, RET_V)]
    aliases = {}
    if latent:
        in_specs += [
            pl.BlockSpec((RET_ROWS, RET_DK), lambda b, h: (0, 0)),
            pl.BlockSpec((RET_ROWS, RET_DK), lambda b, h: (0, 0)),
            pl.BlockSpec((None, 2, None, RET_DK, RET_DV), lambda b, h: (b, 0, h, 0, 0)),
            pl.BlockSpec(memory_space=pl.ANY),
        ]
        args += [rope[0], rope[1], s0, o_prev]
        aliases = {len(args) - 1: 0}
        s_shape = jax.ShapeDtypeStruct((N_LAT, 1, 2, RET_H, RET_DK, RET_DV), F32)
        s_layer = 0
    else:
        s_shape = jax.ShapeDtypeStruct((N_CTX, DEPTH // 2, 2, RET_H, RET_DK, RET_DV), F32)
        s_layer = layer
        if s_prev is not None:
            in_specs += [pl.BlockSpec(memory_space=pl.ANY)]
            args += [s_prev]
            aliases = {len(args) - 1: 1}
    return pl.pallas_call(
        functools.partial(_ret_kernel, latent=latent, aliased=len(aliases)),
        grid=(nblk, RET_H),
        in_specs=in_specs,
        out_specs=[
            pl.BlockSpec((RET_ROWS, RET_DV), lambda b, h: (blk0 + b, h)),
            pl.BlockSpec((seqs, None, 2, None, RET_DK, RET_DV), lambda b, h: (b, s_layer, 0, h, 0, 0)),
        ],
        out_shape=(jax.ShapeDtypeStruct((TOK, RET_V), BF16), s_shape),
        scratch_shapes=[
            pltpu.VMEM((RET_ROWS, RET_DK), BF16),
            pltpu.VMEM((RET_ROWS, RET_DK), BF16),
            pltpu.VMEM((RET_ROWS, RET_DV), BF16),
            pltpu.VMEM((RET_NC, 2 * RET_DK, RET_DV), F32),
            pltpu.VMEM((RET_NC, RET_DK, 2 * RET_DV), BF16),
        ],
        input_output_aliases=aliases,
        compiler_params=pltpu.CompilerParams(
            dimension_semantics=("parallel", "arbitrary"), vmem_limit_bytes=VMEM_MID),
        name="retention_latent" if latent else "retention_context",
    )(*args)


def _rope_tables():
    n_freq = RET_DK // 4
    t = jnp.arange(L_LAT)
    row = (t // GRID_W).astype(F32)
    col = (t % GRID_W).astype(F32)
    inv_freq = jnp.power(ROPE_BASE, -jnp.arange(n_freq, dtype=F32) / n_freq)
    ar, ac = row[:, None] * inv_freq, col[:, None] * inv_freq
    cos = jnp.concatenate([jnp.cos(ar), jnp.cos(ar), jnp.cos(ac), jnp.cos(ac)], axis=-1)
    sin = jnp.concatenate([-jnp.sin(ar), jnp.sin(ar), -jnp.sin(ac), jnp.sin(ac)], axis=-1)
    return cos, sin


def kernel(x_prompt, x_sample, state_s5_re, state_s5_im, state_ret, c, c_ctx, norm1_w, norm2_w, ada_w, ada_b,
           hy_in_w, hy_out_w, s5_lam_re, s5_lam_im, s5_log_step, s5_b_re, s5_b_im, s5_c_re, s5_c_im,
           s5_d, s5_glu_w, s5_glu_b, conv_w, conv_b, ret_in_w, ret_out_w, ret_gamma_logit, ret_gn_w,
           mlp_w1, mlp_w2, final_norm_w):
    x = jnp.concatenate([x_prompt.reshape(TOK_CTX, D_MODEL), x_sample.reshape(TOK - TOK_CTX, D_MODEL)], axis=0)

    cond = jnp.concatenate([c_ctx[None, :], c], axis=0)
    cond = jnp.pad(cond, ((0, MOD_ROWS - N_COND), (0, 0)))
    mod_all = _ada_call(cond, ada_w, ada_b)
    mod_all = mod_all[:, :N_COND].reshape(DEPTH, N_COND, N_MOD, D_MODEL)
    mod_all = jnp.pad(mod_all, ((0, 0), (0, 0), (0, MOD_ROWS - N_MOD), (0, 0)))

    n_s5 = s5_lam_re.shape[0]
    rows = n_s5 * 2 * S5_GROUPS
    ab_re, ab_im, bb_re, bb_im = _disc_call(
        s5_lam_re.reshape(rows, S5_STATE), s5_lam_im.reshape(rows, S5_STATE),
        jnp.broadcast_to(s5_log_step.reshape(rows, 1), (rows, S5_STATE)),
        jnp.moveaxis(s5_b_re, -1, 0).reshape(S5_GROUP_CH, rows, S5_STATE),
        jnp.moveaxis(s5_b_im, -1, 0).reshape(S5_GROUP_CH, rows, S5_STATE))
    ab_re = ab_re.reshape(n_s5, 2, S5_GROUPS, S5_STATE)
    ab_im = ab_im.reshape(n_s5, 2, S5_GROUPS, S5_STATE)
    bb_re = bb_re.reshape(S5_GROUP_CH, n_s5, 2, S5_GROUPS, S5_STATE)
    bb_im = bb_im.reshape(S5_GROUP_CH, n_s5, 2, S5_GROUPS, S5_STATE)

    rope = _rope_tables()
    zero_state = jnp.zeros((N_CTX // SUBLANES, 2, S5_KBLK, SUBLANES, 2 * S5_BLK), F32)

    new_re, new_im, new_ret = [], [], None
    y_prompt = y_sample = None
    for i in range(DEPTH):
        j = i // 2
        mod = mod_all[i]
        if i % 2 == 0:
            proj = _in_call(x, mod, norm1_w[i], hy_in_w[j])
            wb, wc, a_bar = _s5_weights(ab_re[j], ab_im[j], bb_re[:, j], bb_im[:, j], s5_c_re[j], s5_c_im[j])
            yf, yb, h_ctx = _s5_call(proj, wb, wc, a_bar, zero_state, latent=False)
            yf, yb, _ = _s5_call(proj, wb, wc, a_bar, _s5_state_in(state_s5_re[:, j], state_s5_im[:, j]),
                                 latent=True, y_prev=(yf, yb))
            x = _mix_call(yf, yb, proj, s5_d[j], s5_glu_w[j], s5_glu_b[j], conv_w[j], conv_b[j],
                          hy_out_w[j], x, mod)
            s_re, s_im = _s5_state_out(h_ctx)
            new_re.append(s_re)
            new_im.append(s_im)
        else:
            proj = _in_call(x, mod, norm1_w[i], ret_in_w[j])
            gam = jnp.broadcast_to(ret_gamma_logit[j].T[:, :, None, None], (RET_H, 2, SUBLANES, RET_DV))
            o, new_ret = _ret_call(proj, gam, ret_gn_w[j], latent=False, layer=j, s_prev=new_ret)
            o, _ = _ret_call(proj, gam, ret_gn_w[j], latent=True, layer=j, rope=rope, s0=state_ret[:, j], o_prev=o)
            x = _resid_call(o, ret_out_w[j], x, mod)
        if i < DEPTH - 1:
            x = _mlp_call(x, mod, norm2_w[i], mlp_w1[i], mlp_w2[i])
        else:
            y_prompt, y_sample = _mlp_call(x, mod, norm2_w[i], mlp_w1[i], mlp_w2[i], fw=final_norm_w)

    return (y_prompt.reshape(N_CTX, L_CTX, D_MODEL), y_sample.reshape(N_LAT, L_LAT, D_MODEL),
            jnp.stack(new_re, 1), jnp.stack(new_im, 1), new_ret)
```

```python
import functools
import math

import jax
import jax.numpy as jnp
from jax import lax
from jax.experimental import pallas as pl
from jax.experimental.pallas import tpu as pltpu

F32 = jnp.float32
BF16 = jnp.bfloat16

D_MODEL = 1024
DEPTH = 4
N_CTX, L_CTX = 32, 256
N_LAT, L_LAT = 4, 2048
TOK_CTX = N_CTX * L_CTX
TOK = TOK_CTX + N_LAT * L_LAT
N_COND = 1 + N_LAT
GRID_W = 64
EPS = 1e-6

S5_WIDTH = 512
S5_GROUP_CH = 16
S5_GROUPS = 32
S5_STATE = 64
S5_KBLK = 4
S5_GPB = S5_GROUPS // S5_KBLK
S5_BLK = S5_GPB * S5_STATE
S5_SLAB = 128
S5_QUARTERS = S5_BLK // S5_SLAB
S5_ROWS = 512
S5_UNROLL = 4
SUBLANES = 8
CONV_W = 512
HY_IN = S5_WIDTH + 3 * CONV_W

RET_H = 8
RET_DK = 128
RET_DV = 256
RET_QK = RET_H * RET_DK
RET_V = RET_H * RET_DV
RET_IN = 2 * RET_QK + 2 * RET_V
RET_C = 128
RET_ROWS = 2048
RET_NC = RET_ROWS // RET_C
RET_UNROLL = 8
ROPE_BASE = 10000.0
MLP_H = 4 * D_MODEL
N_MOD = 6
MOD_ROWS = 8

VMEM_BIG = 56 * 1024 * 1024
VMEM_MID = 40 * 1024 * 1024

TM_DENSE = 1024
TN_DENSE = 1024
TH_MLP = 1024
TM_MIX = 512
HALO = 8


def _cond_row(i, tm):
    row0 = i * tm
    return jnp.where(row0 < TOK_CTX, 0, 1 + (row0 - TOK_CTX) // L_LAT)


def _norm_mod(x, nw, scale, shift):
    y = x * lax.rsqrt(jnp.mean(x * x, axis=-1, keepdims=True) + EPS)
    return (y * nw) * (1.0 + scale) + shift


def _silu(x):
    return x * jax.nn.sigmoid(x)


def _ada_kernel(c_ref, w_ref, b_ref, o_ref):
    sc = _silu(c_ref[...]).astype(BF16)
    o_ref[0] = jnp.dot(sc, w_ref[0].astype(BF16), preferred_element_type=F32) + b_ref[0]


def _ada_call(cond, ada_w, ada_b):
    tn = 1536
    return pl.pallas_call(
        _ada_kernel,
        grid=(DEPTH, N_MOD * D_MODEL // tn),
        in_specs=[
            pl.BlockSpec((MOD_ROWS, D_MODEL), lambda l, j: (0, 0)),
            pl.BlockSpec((1, D_MODEL, tn), lambda l, j: (l, 0, j)),
            pl.BlockSpec((1, 1, tn), lambda l, j: (l, 0, j)),
        ],
        out_specs=pl.BlockSpec((1, MOD_ROWS, tn), lambda l, j: (l, 0, j)),
        out_shape=jax.ShapeDtypeStruct((DEPTH, MOD_ROWS, N_MOD * D_MODEL), F32),
        compiler_params=pltpu.CompilerParams(
            dimension_semantics=("arbitrary", "arbitrary"), vmem_limit_bytes=VMEM_MID),
        name="ada_mod",
    )(cond, ada_w, ada_b.reshape(DEPTH, 1, N_MOD * D_MODEL))


def _in_kernel(x_ref, mod_ref, nw_ref, w_ref, o_ref, h_ref, wb_ref):
    i, j = pl.program_id(0), pl.program_id(1)

    @pl.when(j == 0)
    def _():
        h = _norm_mod(x_ref[...], nw_ref[...], mod_ref[1:2, :], mod_ref[0:1, :])
        h_ref[...] = h.astype(BF16)

    @pl.when(i == 0)
    def _():
        wb_ref[j] = w_ref[...].astype(BF16)

    o_ref[...] = jnp.dot(h_ref[...], wb_ref[j], preferred_element_type=F32)


def _in_call(x, mod, nw, w, layer):
    n = w.shape[2]
    tm, tn = TM_DENSE, TN_DENSE
    nt = n // tn
    return pl.pallas_call(
        _in_kernel,
        grid=(TOK // tm, nt),
        in_specs=[
            pl.BlockSpec((tm, D_MODEL), lambda i, j: (i, 0)),
            pl.BlockSpec((None, MOD_ROWS, D_MODEL), lambda i, j: (_cond_row(i, tm), 0, 0)),
            pl.BlockSpec((1, D_MODEL), lambda i, j: (0, 0)),
            pl.BlockSpec((None, D_MODEL, tn), lambda i, j: (layer, 0, jnp.where(i == 0, j, nt - 1))),
        ],
        out_specs=pl.BlockSpec((tm, tn), lambda i, j: (i, j)),
        out_shape=jax.ShapeDtypeStruct((TOK, n), F32),
        scratch_shapes=[pltpu.VMEM((tm, D_MODEL), BF16), pltpu.VMEM((nt, D_MODEL, tn), BF16)],
        compiler_params=pltpu.CompilerParams(
            dimension_semantics=("arbitrary", "arbitrary"), vmem_limit_bytes=VMEM_BIG),
        name="norm_proj",
    )(x, mod, nw.reshape(1, D_MODEL), w)


def _mlp_body(x_ref, mod_ref, nw_ref, w1_ref, w2_ref, h_ref, acc_ref):
    j = pl.program_id(1)

    @pl.when(j == 0)
    def _():
        h = _norm_mod(x_ref[...], nw_ref[...], mod_ref[4:5, :], mod_ref[3:4, :])
        h_ref[...] = h.astype(BF16)
        acc_ref[...] = jnp.zeros_like(acc_ref)

    a = jnp.maximum(jnp.dot(h_ref[...], w1_ref[...].astype(BF16), preferred_element_type=F32), 0.0)
    acc_ref[...] += jnp.dot((a * a).astype(BF16), w2_ref[...].astype(BF16), preferred_element_type=F32)


def _mlp_kernel(x_ref, mod_ref, nw_ref, w1_ref, w2_ref, o_ref, h_ref, acc_ref):
    _mlp_body(x_ref, mod_ref, nw_ref, w1_ref, w2_ref, h_ref, acc_ref)

    @pl.when(pl.program_id(1) == pl.num_programs(1) - 1)
    def _():
        o_ref[...] = x_ref[...] + mod_ref[5:6, :] * acc_ref[...]


def _mlp_final_kernel(x_ref, mod_ref, nw_ref, w1_ref, w2_ref, fw_ref, oc_ref, ol_ref, h_ref, acc_ref, *, tm):
    _mlp_body(x_ref, mod_ref, nw_ref, w1_ref, w2_ref, h_ref, acc_ref)
    i = pl.program_id(0)
    last = pl.program_id(1) == pl.num_programs(1) - 1

    def result():
        y = x_ref[...] + mod_ref[5:6, :] * acc_ref[...]
        return y * lax.rsqrt(jnp.mean(y * y, axis=-1, keepdims=True) + EPS) * fw_ref[...]

    @pl.when(jnp.logical_and(last, i < TOK_CTX // tm))
    def _():
        oc_ref[...] = result()

    @pl.when(jnp.logical_and(last, i >= TOK_CTX // tm))
    def _():
        ol_ref[...] = result()


def _mlp_call(x, mod, nw, w1, w2, layer, fw=None):
    tm = TM_DENSE
    th = TH_MLP if fw is None else TH_MLP // 2
    n_ctx_tiles = TOK_CTX // tm
    in_specs = [
        pl.BlockSpec((tm, D_MODEL), lambda i, j: (i, 0)),
        pl.BlockSpec((None, MOD_ROWS, D_MODEL), lambda i, j: (_cond_row(i, tm), 0, 0)),
        pl.BlockSpec((1, D_MODEL), lambda i, j: (0, 0)),
        pl.BlockSpec((None, D_MODEL, th), lambda i, j: (layer, 0, j)),
        pl.BlockSpec((None, th, D_MODEL), lambda i, j: (layer, j, 0)),
    ]
    scratch = [pltpu.VMEM((tm, D_MODEL), BF16), pltpu.VMEM((tm, D_MODEL), F32)]
    args = [x, mod, nw.reshape(1, D_MODEL), w1, w2]
    if fw is None:
        return pl.pallas_call(
            _mlp_kernel,
            grid=(TOK // tm, MLP_H // th),
            in_specs=in_specs,
            out_specs=pl.BlockSpec((tm, D_MODEL), lambda i, j: (i, 0)),
            out_shape=jax.ShapeDtypeStruct((TOK, D_MODEL), F32),
            scratch_shapes=scratch,
            compiler_params=pltpu.CompilerParams(
                dimension_semantics=("parallel", "arbitrary"), vmem_limit_bytes=VMEM_BIG),
            name="mlp",
        )(*args)
    return pl.pallas_call(
        functools.partial(_mlp_final_kernel, tm=tm),
        grid=(TOK // tm, MLP_H // th),
        in_specs=in_specs + [pl.BlockSpec((1, D_MODEL), lambda i, j: (0, 0))],
        out_specs=[
            pl.BlockSpec((tm, D_MODEL), lambda i, j: (jnp.minimum(i, n_ctx_tiles - 1), 0)),
            pl.BlockSpec((tm, D_MODEL), lambda i, j: (jnp.maximum(i - n_ctx_tiles, 0), 0)),
        ],
        out_shape=(jax.ShapeDtypeStruct((TOK_CTX, D_MODEL), F32),
                   jax.ShapeDtypeStruct((TOK - TOK_CTX, D_MODEL), F32)),
        scratch_shapes=scratch,
        compiler_params=pltpu.CompilerParams(
            dimension_semantics=("arbitrary", "arbitrary"), vmem_limit_bytes=VMEM_BIG),
        name="mlp_final",
    )(*args, fw.reshape(1, D_MODEL))


def _resid_kernel(a_ref, w_ref, x_ref, mod_ref, o_ref, wb_ref):
    @pl.when(pl.program_id(0) == 0)
    def _():
        wb_ref[...] = w_ref[...].astype(BF16)

    y = jnp.dot(a_ref[...], wb_ref[...], preferred_element_type=F32)
    o_ref[...] = x_ref[...] + mod_ref[2:3, :] * y


def _resid_call(a, w, x, mod, layer):
    tm = TM_DENSE
    k = a.shape[1]
    return pl.pallas_call(
        _resid_kernel,
        grid=(TOK // tm,),
        in_specs=[
            pl.BlockSpec((tm, k), lambda i: (i, 0)),
            pl.BlockSpec((None, k, D_MODEL), lambda i: (layer, 0, 0)),
            pl.BlockSpec((tm, D_MODEL), lambda i: (i, 0)),
            pl.BlockSpec((None, MOD_ROWS, D_MODEL), lambda i: (_cond_row(i, tm), 0, 0)),
        ],
        out_specs=pl.BlockSpec((tm, D_MODEL), lambda i: (i, 0)),
        out_shape=jax.ShapeDtypeStruct((TOK, D_MODEL), F32),
        scratch_shapes=[pltpu.VMEM((k, D_MODEL), BF16)],
        compiler_params=pltpu.CompilerParams(
            dimension_semantics=("arbitrary",), vmem_limit_bytes=VMEM_BIG),
        name="out_proj_resid",
    )(a, w, x, mod)


def _disc_kernel(lr_ref, li_ref, ls_ref, br_ref, bi_ref, abr_ref, abi_ref, bbr_ref, bbi_ref):
    lr, li = lr_ref[...], li_ref[...]
    dt = jnp.exp(ls_ref[...])
    mag = jnp.exp(lr * dt)
    abr = mag * jnp.cos(li * dt)
    abi = mag * jnp.sin(li * dt)
    den = lr * lr + li * li
    fr = ((abr - 1.0) * lr + abi * li) / den
    fi = (abi * lr - (abr - 1.0) * li) / den
    abr_ref[...] = abr
    abi_ref[...] = abi
    for c in range(S5_GROUP_CH):
        br, bi = br_ref[c], bi_ref[c]
        bbr_ref[c] = fr * br - fi * bi
        bbi_ref[c] = fr * bi + fi * br


def _disc_call(lam_re, lam_im, log_step, b_re, b_im):
    r = lam_re.shape[0]
    small = jax.ShapeDtypeStruct((r, S5_STATE), F32)
    big = jax.ShapeDtypeStruct((S5_GROUP_CH, r, S5_STATE), F32)
    return pl.pallas_call(_disc_kernel, out_shape=(small, small, big, big), name="s5_discretise")(
        lam_re, lam_im, log_step, b_re, b_im)


def _s5_kernel(*refs, ns, steps, aliased):
    uf_ref, ub_ref, wb_ref, wc_ref, a_ref, h0_ref = refs[:6]
    yf_ref, yb_ref, hfin_ref, utm_ref, ytm_ref, bu_ref, xs_ref, st_ref = refs[6 + (2 if aliased else 0):]
    c = pl.program_id(1)
    slabs_per_k = 2 * S5_QUARTERS
    u_refs = (uf_ref, ub_ref)
    y_refs = (yf_ref, yb_ref)

    def seq_rows(b):
        return pl.ds(b, steps, stride=ns)

    @pl.when(c == 0)
    def _():
        st_ref[...] = h0_ref[...]

    for d in range(2):
        for k in range(S5_KBLK):
            for b in range(ns):
                utm_ref[d, k, seq_rows(b), :] = u_refs[d][b, :, k * 128:(k + 1) * 128]

    for k in range(S5_KBLK):
        for d in range(2):
            res = jnp.dot(utm_ref[d, k].astype(BF16), wb_ref[d, k], preferred_element_type=F32)
            for s in range(slabs_per_k):
                bu_ref[d, s] = res[:, s * S5_SLAB:(s + 1) * S5_SLAB]

        a = [[(a_ref[d, k, 0:ns, q * S5_SLAB:(q + 1) * S5_SLAB],
               a_ref[d, k, 0:ns, S5_BLK + q * S5_SLAB:S5_BLK + (q + 1) * S5_SLAB])
              for q in range(S5_QUARTERS)] for d in range(2)]
        init = tuple(st_ref[d, k, 0:ns, o:o + S5_SLAB]
                     for d in range(2) for q in range(S5_QUARTERS)
                     for o in (q * S5_SLAB, S5_BLK + q * S5_SLAB))

        def body(i, carry, a=a):
            out = []
            for d in range(2):
                t = i if d == 0 else steps - 1 - i
                tile = pl.ds(pl.multiple_of(t * ns, ns), ns)
                for q in range(S5_QUARTERS):
                    xr, xi = carry[2 * (d * S5_QUARTERS + q)], carry[2 * (d * S5_QUARTERS + q) + 1]
                    ar, ai = a[d][q]
                    nr = ar * xr - ai * xi + bu_ref[d, q, tile, :]
                    ni = ar * xi + ai * xr + bu_ref[d, S5_QUARTERS + q, tile, :]
                    xs_ref[d, q, tile, :] = nr
                    xs_ref[d, S5_QUARTERS + q, tile, :] = ni
                    out += [nr, ni]
            return tuple(out)

        fin = lax.fori_loop(0, steps, body, init, unroll=S5_UNROLL)
        for d in range(2):
            for q in range(S5_QUARTERS):
                st_ref[d, k, 0:ns, q * S5_SLAB:(q + 1) * S5_SLAB] = fin[2 * (d * S5_QUARTERS + q)]
                st_ref[d, k, 0:ns, S5_BLK + q * S5_SLAB:S5_BLK + (q + 1) * S5_SLAB] = (
                    fin[2 * (d * S5_QUARTERS + q) + 1])

        for d in range(2):
            xk = jnp.concatenate([xs_ref[d, s] for s in range(slabs_per_k)], axis=1).astype(BF16)
            ytm_ref[d, k] = jnp.dot(xk, wc_ref[d, k], preferred_element_type=F32)

    for d in range(2):
        for k in range(S5_KBLK):
            for b in range(ns):
                y_refs[d][b, :, k * 128:(k + 1) * 128] = ytm_ref[d, k, seq_rows(b), :]

    @pl.when(c == pl.num_programs(1) - 1)
    def _():
        hfin_ref[...] = st_ref[...]


def _s5_call(proj, wb, wc, a_bar, h0, latent, y_prev=None):
    if latent:
        length, ns, blk0 = L_LAT, N_LAT, TOK_CTX // (N_LAT * L_LAT)
    else:
        length, ns, blk0 = L_CTX, SUBLANES, 0
    steps = S5_ROWS // ns
    ng, nc = h0.shape[0], length // steps
    ublk = (ns, steps, S5_WIDTH)
    stblk = (None, 2, S5_KBLK, SUBLANES, 2 * S5_BLK)
    fwd = lambda g, c: (blk0 + g, c, 0)
    bwd = lambda g, c: (blk0 + g, nc - 1 - c, 0)
    const4 = lambda g, c: (0, 0, 0, 0)
    in_specs = [
        pl.BlockSpec(ublk, fwd),
        pl.BlockSpec(ublk, bwd),
        pl.BlockSpec((2, S5_KBLK, 128, 2 * S5_BLK), const4),
        pl.BlockSpec((2, S5_KBLK, 2 * S5_BLK, 128), const4),
        pl.BlockSpec((2, S5_KBLK, SUBLANES, 2 * S5_BLK), const4),
        pl.BlockSpec(stblk, lambda g, c: (g, 0, 0, 0, 0)),
    ]
    proj3 = proj.reshape(TOK // length, length, HY_IN)
    args = [proj3, proj3, wb, wc, a_bar, h0]
    aliases = {}
    if y_prev is not None:
        in_specs += [pl.BlockSpec(memory_space=pl.ANY)] * 2
        args += [y.reshape(TOK // length, length, S5_WIDTH) for y in y_prev]
        aliases = {6: 0, 7: 1}
    y_shape = jax.ShapeDtypeStruct((TOK // length, length, S5_WIDTH), F32)
    yf, yb, hfin = pl.pallas_call(
        functools.partial(_s5_kernel, ns=ns, steps=steps, aliased=y_prev is not None),
        grid=(ng, nc),
        in_specs=in_specs,
        out_specs=[
            pl.BlockSpec(ublk, fwd),
            pl.BlockSpec(ublk, bwd),
            pl.BlockSpec(stblk, lambda g, c: (g, 0, 0, 0, 0)),
        ],
        out_shape=(y_shape, y_shape, jax.ShapeDtypeStruct(h0.shape, F32)),
        scratch_shapes=[
            pltpu.VMEM((2, S5_KBLK, S5_ROWS, S5_SLAB), F32),
            pltpu.VMEM((2, S5_KBLK, S5_ROWS, S5_SLAB), F32),
            pltpu.VMEM((2, 2 * S5_QUARTERS, S5_ROWS, S5_SLAB), F32),
            pltpu.VMEM((2, 2 * S5_QUARTERS, S5_ROWS, S5_SLAB), F32),
            pltpu.VMEM((2, S5_KBLK, SUBLANES, 2 * S5_BLK), F32),
        ],
        input_output_aliases=aliases,
        compiler_params=pltpu.CompilerParams(
            dimension_semantics=("parallel", "arbitrary"), vmem_limit_bytes=VMEM_BIG),
        name="s5_scan_latent" if latent else "s5_scan_context",
    )(*args)
    return yf.reshape(TOK, S5_WIDTH), yb.reshape(TOK, S5_WIDTH), hfin


def _s5_weights(ab_re, ab_im, bb_re, bb_im, c_re, c_im):
    eye = jnp.eye(S5_GPB, dtype=F32)

    def in_side(bb):
        t = bb.reshape(S5_GROUP_CH, 2, S5_KBLK, S5_GPB, S5_STATE)
        t = jnp.einsum('cdkgp,gh->dkgchp', t, eye)
        return t.reshape(2, S5_KBLK, S5_GPB * S5_GROUP_CH, S5_BLK)

    def out_side(cc):
        t = cc.reshape(2, S5_KBLK, S5_GPB, S5_GROUP_CH, S5_STATE)
        t = jnp.einsum('dkgcp,gh->dkgphc', t, eye)
        return t.reshape(2, S5_KBLK, S5_BLK, S5_GPB * S5_GROUP_CH)

    wb = jnp.concatenate([in_side(bb_re), in_side(bb_im)], axis=-1).astype(BF16)
    wc = jnp.concatenate([out_side(c_re), -out_side(c_im)], axis=-2).astype(BF16)
    a = jnp.concatenate([ab_re.reshape(2, S5_KBLK, 1, S5_BLK), ab_im.reshape(2, S5_KBLK, 1, S5_BLK)], axis=-1)
    a = jnp.broadcast_to(a, (2, S5_KBLK, SUBLANES, 2 * S5_BLK))
    return wb, wc, a


def _s5_state_in(s_re, s_im):
    def blk(s):
        return s.reshape(-1, 2, S5_KBLK, S5_BLK).transpose(1, 2, 0, 3)
    h = jnp.concatenate([blk(s_re), blk(s_im)], axis=-1)
    h = jnp.pad(h, ((0, 0), (0, 0), (0, SUBLANES - h.shape[2]), (0, 0)))
    return h[None]


def _s5_state_out(h):
    def unblk(s):
        return s.transpose(0, 3, 1, 2, 4).reshape(-1, 2, S5_GROUPS, S5_STATE)
    return unblk(h[..., :S5_BLK]), unblk(h[..., S5_BLK:])


def _gelu_tanh(x):
    return x * (0.5 * (1.0 + jnp.tanh(math.sqrt(2.0 / math.pi) * (x + 0.044715 * (x * x * x)))))


def _mix_kernel(yf_ref, yb_ref, u_ref, bg_ref, cg_ref, v_ref, cgp_ref, vp_ref, cgn_ref, vn_ref,
                d_ref, gw_ref, gb_ref, cw_ref, cb_ref, ow_ref, x_ref, mod_ref, o_ref, gwb_ref, owb_ref):
    tm = TM_MIX

    @pl.when(pl.program_id(0) == 0)
    def _():
        gwb_ref[...] = gw_ref[...].astype(BF16)
        owb_ref[...] = ow_ref[...].astype(BF16)

    y = yf_ref[...] + yb_ref[...] + d_ref[...] * u_ref[...]
    z = _gelu_tanh(y)
    gate = jax.nn.sigmoid(jnp.dot(z.astype(BF16), gwb_ref[...], preferred_element_type=F32) + gb_ref[...])
    a_out = z * gate

    p = cg_ref[...] * v_ref[...]
    local = lax.broadcasted_iota(jnp.int32, (tm, 1), 0)
    row = pl.program_id(0) * tm + local
    seq_len = jnp.where(row < TOK_CTX, L_CTX, L_LAT)
    pos = jnp.bitwise_and(row, seq_len - 1)
    p_prev = jnp.where(local == 0, cgp_ref[HALO - 1:HALO, :] * vp_ref[HALO - 1:HALO, :], pltpu.roll(p, 1, 0))
    p_prev = jnp.where(pos == 0, 0.0, p_prev)
    p_next = jnp.where(local == tm - 1, cgn_ref[0:1, :] * vn_ref[0:1, :], pltpu.roll(p, tm - 1, 0))
    p_next = jnp.where(pos == seq_len - 1, 0.0, p_next)
    conv = cw_ref[0:1, :] * p_prev + cw_ref[1:2, :] * p + cw_ref[2:3, :] * p_next + cb_ref[...]
    b_out = bg_ref[...] * conv

    out = (jnp.dot(a_out.astype(BF16), owb_ref[0:S5_WIDTH, :], preferred_element_type=F32)
           + jnp.dot(b_out.astype(BF16), owb_ref[S5_WIDTH:, :], preferred_element_type=F32))
    o_ref[...] = x_ref[...] + mod_ref[2:3, :] * out


def _mix_call(yf, yb, proj, s5_d, glu_w, glu_b, conv_w, conv_b, out_w, x, mod, layer):
    tm = TM_MIX
    hb = tm // HALO
    last_halo = TOK // HALO - 1
    blk = lambda col: pl.BlockSpec((tm, CONV_W), lambda i: (i, col))
    row1 = pl.BlockSpec((1, CONV_W), lambda i: (0, 0))
    return pl.pallas_call(
        _mix_kernel,
        grid=(TOK // tm,),
        in_specs=[
            blk(0), blk(0), blk(0), blk(1), blk(2), blk(3),
            pl.BlockSpec((HALO, CONV_W), lambda i: (jnp.maximum(i * hb - 1, 0), 2)),
            pl.BlockSpec((HALO, CONV_W), lambda i: (jnp.maximum(i * hb - 1, 0), 3)),
            pl.BlockSpec((HALO, CONV_W), lambda i: (jnp.minimum((i + 1) * hb, last_halo), 2)),
            pl.BlockSpec((HALO, CONV_W), lambda i: (jnp.minimum((i + 1) * hb, last_halo), 3)),
            row1,
            pl.BlockSpec((None, S5_WIDTH, S5_WIDTH), lambda i: (layer, 0, 0)),
            row1,
            pl.BlockSpec((MOD_ROWS, CONV_W), lambda i: (0, 0)),
            row1,
            pl.BlockSpec((None, D_MODEL, D_MODEL), lambda i: (layer, 0, 0)),
            pl.BlockSpec((tm, D_MODEL), lambda i: (i, 0)),
            pl.BlockSpec((None, MOD_ROWS, D_MODEL), lambda i: (_cond_row(i, tm), 0, 0)),
        ],
        out_specs=pl.BlockSpec((tm, D_MODEL), lambda i: (i, 0)),
        out_shape=jax.ShapeDtypeStruct((TOK, D_MODEL), F32),
        scratch_shapes=[pltpu.VMEM((S5_WIDTH, S5_WIDTH), BF16), pltpu.VMEM((D_MODEL, D_MODEL), BF16)],
        compiler_params=pltpu.CompilerParams(
            dimension_semantics=("arbitrary",), vmem_limit_bytes=VMEM_MID),
        name="glu_conv_out",
    )(yf, yb, proj, proj, proj, proj, proj, proj, proj, proj,
      s5_d.reshape(1, -1), glu_w, glu_b.reshape(1, -1),
      jnp.pad(conv_w, ((0, MOD_ROWS - conv_w.shape[0]), (0, 0))), conv_b.reshape(1, -1), out_w, x, mod)


def _log_sigmoid(x):
    return jnp.minimum(x, 0.0) - jnp.log1p(jnp.exp(-jnp.abs(x)))


def _ret_kernel(*refs, latent, aliased):
    q_ref, k_ref, v_ref, g_ref, gam_ref, gn_ref = refs[:6]
    n_in = 6
    if latent:
        cos_ref, sin_ref, s0_ref = refs[6:9]
        n_in = 9
    n_in += aliased
    o_ref, sfin_ref, qs_ref, sc_ref, vb_ref, kv_ref, sk_ref = refs[n_in:]
    seq_chunks = RET_NC if latent else L_CTX // RET_C

    lg_f = _log_sigmoid(gam_ref[0, 0:1, :])
    lg_b = _log_sigmoid(gam_ref[1, 0:1, :])
    ii = lax.broadcasted_iota(jnp.int32, (RET_C, RET_C), 0)
    jj = lax.broadcasted_iota(jnp.int32, (RET_C, RET_C), 1)
    diff = (ii - jj).astype(F32)
    lgf_k, lgb_k = lg_f[:, :RET_DK], lg_b[:, :RET_DK]
    intra = (jnp.where(diff >= 0, jnp.exp(lgf_k * jnp.maximum(diff, 0.0)), 0.0)
             + jnp.where(diff <= 0, jnp.exp(lgb_k * jnp.maximum(-diff, 0.0)), 0.0))
    tk = lax.broadcasted_iota(jnp.int32, (RET_C, RET_DK), 0).astype(F32)
    tv = lax.broadcasted_iota(jnp.int32, (RET_C, RET_DV), 0).astype(F32)
    kdec_f = jnp.exp(lgf_k * (RET_C - 1.0 - tk))
    kdec_b = jnp.exp(lgb_k * tk)
    qdec_f = jnp.exp(lg_f * (tv + 1.0))
    qdec_b = jnp.exp(lg_b * (RET_C - tv))
    cd_f = jnp.exp(lg_f * RET_C)
    cd_b = jnp.exp(lg_b * RET_C)

    lane = lax.broadcasted_iota(jnp.int32, (RET_C, RET_DK), 1)
    first_half = jnp.bitwise_and(lane, 2 * (RET_DK // 4) - 1) < RET_DK // 4

    def rotate(x, cos, sin):
        swapped = jnp.where(first_half, pltpu.roll(x, RET_DK - RET_DK // 4, 1), pltpu.roll(x, RET_DK // 4, 1))
        return x * cos + swapped * sin

    tn_dims = (((0,), (0,)), ((), ()))
    nt_dims = (((1,), (1,)), ((), ()))

    def prep(c, _):
        r = pl.ds(pl.multiple_of(c * RET_C, RET_C), RET_C)
        q, k = q_ref[r, :], k_ref[r, :]
        if latent:
            cos, sin = cos_ref[r, :], sin_ref[r, :]
            q, k = rotate(q, cos, sin), rotate(k, cos, sin)
        qs = (q * (RET_DK ** -0.5)).astype(BF16)
        qs_ref[r, :] = qs
        scores = lax.dot_general(qs, k.astype(BF16), nt_dims, preferred_element_type=F32) * intra
        sc_ref[c] = scores.astype(BF16)
        kd = jnp.concatenate([(k * kdec_f).astype(BF16), (k * kdec_b).astype(BF16)], axis=1)
        vb = v_ref[r, :].astype(BF16)
        vb_ref[r, :] = vb
        kv_ref[c] = lax.dot_general(kd, vb, tn_dims, preferred_element_type=F32)
        return 0

    lax.fori_loop(0, RET_NC, prep, 0, unroll=RET_UNROLL)

    zeros = jnp.zeros((RET_DK, RET_DV), F32)
    s = s0_ref[0] if latent else zeros
    for c in range(RET_NC):
        if c % seq_chunks == 0 and not (latent and c == 0):
            s = zeros
        sk_ref[c, :, 0:RET_DV] = s.astype(BF16)
        s = cd_f * s + kv_ref[c, 0:RET_DK, :]
        if (c + 1) % seq_chunks == 0:
            sfin_ref[c // seq_chunks, 0] = s
    s = s0_ref[1] if latent else zeros
    for c in reversed(range(RET_NC)):
        if (c + 1) % seq_chunks == 0 and not (latent and c == RET_NC - 1):
            s = zeros
        sk_ref[c, :, RET_DV:2 * RET_DV] = s.astype(BF16)
        s = cd_b * s + kv_ref[c, RET_DK:2 * RET_DK, :]
        if c % seq_chunks == 0:
            sfin_ref[c // seq_chunks, 1] = s

    def emit(c, _):
        r = pl.ds(pl.multiple_of(c * RET_C, RET_C), RET_C)
        cross = jnp.dot(qs_ref[r, :], sk_ref[c], preferred_element_type=F32)
        o = (jnp.dot(sc_ref[c], vb_ref[r, :], preferred_element_type=F32)
             + cross[:, 0:RET_DV] * qdec_f + cross[:, RET_DV:2 * RET_DV] * qdec_b)
        o = o * lax.rsqrt(jnp.mean(o * o, axis=-1, keepdims=True) + EPS) * gn_ref[...]
        o_ref[r, :] = (_silu(g_ref[r, :]) * o).astype(o_ref.dtype)
        return 0

    lax.fori_loop(0, RET_NC, emit, 0, unroll=RET_UNROLL)


def _ret_call(proj, gam, gn_w, latent, layer, rope=None, s0=None, o_prev=None, s_prev=None):
    nblk = TOK_CTX // RET_ROWS
    blk0 = nblk if latent else 0
    seqs = 1 if latent else RET_ROWS // L_CTX
    kcol, vcol, gcol = RET_QK // RET_DK, 2 * RET_QK // RET_DV, (2 * RET_QK + RET_V) // RET_DV
    in_specs = [
        pl.BlockSpec((RET_ROWS, RET_DK), lambda b, h: (blk0 + b, h)),
        pl.BlockSpec((RET_ROWS, RET_DK), lambda b, h: (blk0 + b, kcol + h)),
        pl.BlockSpec((RET_ROWS, RET_DV), lambda b, h: (blk0 + b, vcol + h)),
        pl.BlockSpec((RET_ROWS, RET_DV), lambda b, h: (blk0 + b, gcol + h)),
        pl.BlockSpec((None, 2, SUBLANES, RET_DV), lambda b, h: (h, 0, 0, 0)),
        pl.BlockSpec((1, RET_DV), lambda b, h: (0, h)),
    ]
    args = [proj, proj, proj, proj, gam, gn_w.reshape(1, RET_V)]
    aliases = {}
    if latent:
        in_specs += [
            pl.BlockSpec((RET_ROWS, RET_DK), lambda b, h: (0, 0)),
            pl.BlockSpec((RET_ROWS, RET_DK), lambda b, h: (0, 0)),
            pl.BlockSpec((None, 2, None, RET_DK, RET_DV), lambda b, h: (b, 0, h, 0, 0)),
            pl.BlockSpec(memory_space=pl.ANY),
        ]
        args += [rope[0], rope[1], s0, o_prev]
        aliases = {len(args) - 1: 0}
        s_shape = jax.ShapeDtypeStruct((N_LAT, 1, 2, RET_H, RET_DK, RET_DV), F32)
        s_layer = 0
    else:
        s_shape = jax.ShapeDtypeStruct((N_CTX, DEPTH // 2, 2, RET_H, RET_DK, RET_DV), F32)
        s_layer = layer
        if s_prev is not None:
            in_specs += [pl.BlockSpec(memory_space=pl.ANY)]
            args += [s_prev]
            aliases = {len(args) - 1: 1}
    return pl.pallas_call(
        functools.partial(_ret_kernel, latent=latent, aliased=len(aliases)),
        grid=(nblk, RET_H),
        in_specs=in_specs,
        out_specs=[
            pl.BlockSpec((RET_ROWS, RET_DV), lambda b, h: (blk0 + b, h)),
            pl.BlockSpec((seqs, None, 2, None, RET_DK, RET_DV), lambda b, h: (b, s_layer, 0, h, 0, 0)),
        ],
        out_shape=(jax.ShapeDtypeStruct((TOK, RET_V), BF16), s_shape),
        scratch_shapes=[
            pltpu.VMEM((RET_ROWS, RET_DK), BF16),
            pltpu.VMEM((RET_NC, RET_C, RET_C), BF16),
            pltpu.VMEM((RET_ROWS, RET_DV), BF16),
            pltpu.VMEM((RET_NC, 2 * RET_DK, RET_DV), F32),
            pltpu.VMEM((RET_NC, RET_DK, 2 * RET_DV), BF16),
        ],
        input_output_aliases=aliases,
        compiler_params=pltpu.CompilerParams(
            dimension_semantics=("parallel", "arbitrary"), vmem_limit_bytes=VMEM_MID),
        name="retention_latent" if latent else "retention_context",
    )(*args)


def _rope_tables():
    n_freq = RET_DK // 4
    t = jnp.arange(L_LAT)
    row = (t // GRID_W).astype(F32)
    col = (t % GRID_W).astype(F32)
    inv_freq = jnp.power(ROPE_BASE, -jnp.arange(n_freq, dtype=F32) / n_freq)
    ar, ac = row[:, None] * inv_freq, col[:, None] * inv_freq
    cos = jnp.concatenate([jnp.cos(ar), jnp.cos(ar), jnp.cos(ac), jnp.cos(ac)], axis=-1)
    sin = jnp.concatenate([-jnp.sin(ar), jnp.sin(ar), -jnp.sin(ac), jnp.sin(ac)], axis=-1)
    return cos, sin


def kernel(x_prompt, x_sample, state_s5_re, state_s5_im, state_ret, c, c_ctx, norm1_w, norm2_w, ada_w, ada_b,
           hy_in_w, hy_out_w, s5_lam_re, s5_lam_im, s5_log_step, s5_b_re, s5_b_im, s5_c_re, s5_c_im,
           s5_d, s5_glu_w, s5_glu_b, conv_w, conv_b, ret_in_w, ret_out_w, ret_gamma_logit, ret_gn_w,
           mlp_w1, mlp_w2, final_norm_w):
    x = jnp.concatenate([x_prompt.reshape(TOK_CTX, D_MODEL), x_sample.reshape(TOK - TOK_CTX, D_MODEL)], axis=0)

    cond = jnp.concatenate([c_ctx[None, :], c], axis=0)
    cond = jnp.pad(cond, ((0, MOD_ROWS - N_COND), (0, 0)))
    mod_all = _ada_call(cond, ada_w, ada_b)
    mod_all = mod_all[:, :N_COND].reshape(DEPTH, N_COND, N_MOD, D_MODEL)
    mod_all = jnp.pad(mod_all, ((0, 0), (0, 0), (0, MOD_ROWS - N_MOD), (0, 0)))

    n_s5 = s5_lam_re.shape[0]
    rows = n_s5 * 2 * S5_GROUPS
    ab_re, ab_im, bb_re, bb_im = _disc_call(
        s5_lam_re.reshape(rows, S5_STATE), s5_lam_im.reshape(rows, S5_STATE),
        jnp.broadcast_to(s5_log_step.reshape(rows, 1), (rows, S5_STATE)),
        jnp.moveaxis(s5_b_re, -1, 0).reshape(S5_GROUP_CH, rows, S5_STATE),
        jnp.moveaxis(s5_b_im, -1, 0).reshape(S5_GROUP_CH, rows, S5_STATE))
    ab_re = ab_re.reshape(n_s5, 2, S5_GROUPS, S5_STATE)
    ab_im = ab_im.reshape(n_s5, 2, S5_GROUPS, S5_STATE)
    bb_re = bb_re.reshape(S5_GROUP_CH, n_s5, 2, S5_GROUPS, S5_STATE)
    bb_im = bb_im.reshape(S5_GROUP_CH, n_s5, 2, S5_GROUPS, S5_STATE)

    rope = _rope_tables()
    zero_state = jnp.zeros((N_CTX // SUBLANES, 2, S5_KBLK, SUBLANES, 2 * S5_BLK), F32)

    new_re, new_im, new_ret = [], [], None
    y_prompt = y_sample = None
    for i in range(DEPTH):
        j = i // 2
        mod = mod_all[i]
        if i % 2 == 0:
            proj = _in_call(x, mod, norm1_w[i], hy_in_w, j)
            wb, wc, a_bar = _s5_weights(ab_re[j], ab_im[j], bb_re[:, j], bb_im[:, j], s5_c_re[j], s5_c_im[j])
            yf, yb, h_ctx = _s5_call(proj, wb, wc, a_bar, zero_state, latent=False)
            yf, yb, _ = _s5_call(proj, wb, wc, a_bar, _s5_state_in(state_s5_re[:, j], state_s5_im[:, j]),
                                 latent=True, y_prev=(yf, yb))
            x = _mix_call(yf, yb, proj, s5_d[j], s5_glu_w, s5_glu_b[j], conv_w[j], conv_b[j],
                          hy_out_w, x, mod, j)
            s_re, s_im = _s5_state_out(h_ctx)
            new_re.append(s_re)
            new_im.append(s_im)
        else:
            proj = _in_call(x, mod, norm1_w[i], ret_in_w, j)
            gam = jnp.broadcast_to(ret_gamma_logit[j].T[:, :, None, None], (RET_H, 2, SUBLANES, RET_DV))
            o, new_ret = _ret_call(proj, gam, ret_gn_w[j], latent=False, layer=j, s_prev=new_ret)
            o, _ = _ret_call(proj, gam, ret_gn_w[j], latent=True, layer=j, rope=rope, s0=state_ret[:, j], o_prev=o)
            x = _resid_call(o, ret_out_w, x, mod, j)
        if i < DEPTH - 1:
            x = _mlp_call(x, mod, norm2_w[i], mlp_w1, mlp_w2, i)
        else:
            y_prompt, y_sample = _mlp_call(x, mod, norm2_w[i], mlp_w1, mlp_w2, i, fw=final_norm_w)

    return (y_prompt.reshape(N_CTX, L_CTX, D_MODEL), y_sample.reshape(N_LAT, L_LAT, D_MODEL),
            jnp.stack(new_re, 1), jnp.stack(new_im, 1), new_ret)
```

```python
import functools
import math

import jax
import jax.numpy as jnp
from jax import lax
from jax.experimental import pallas as pl
from jax.experimental.pallas import tpu as pltpu

F32 = jnp.float32
BF16 = jnp.bfloat16

D_MODEL = 1024
DEPTH = 4
N_CTX, L_CTX = 32, 256
N_LAT, L_LAT = 4, 2048
TOK_CTX = N_CTX * L_CTX
TOK = TOK_CTX + N_LAT * L_LAT
N_COND = 1 + N_LAT
GRID_W = 64
EPS = 1e-6

S5_WIDTH = 512
S5_GROUP_CH = 16
S5_GROUPS = 32
S5_STATE = 64
S5_KBLK = 4
S5_GPB = S5_GROUPS // S5_KBLK
S5_BLK = S5_GPB * S5_STATE
S5_SLAB = 128
S5_QUARTERS = S5_BLK // S5_SLAB
S5_ROWS = 512
SUBLANES = 8
CONV_W = 512

RET_H = 8
RET_DK = 128
RET_DV = 256
RET_QK = RET_H * RET_DK
RET_V = RET_H * RET_DV
RET_IN = 2 * RET_QK + 2 * RET_V
RET_C = 128
RET_ROWS = 2048
RET_NC = RET_ROWS // RET_C
RET_UNROLL = 8
ROPE_BASE = 10000.0
MLP_H = 4 * D_MODEL
N_MOD = 6
MOD_ROWS = 8

VMEM_BIG = 56 * 1024 * 1024
VMEM_MID = 40 * 1024 * 1024

TM_DENSE = 1024
TN_DENSE = 1024
TH_MLP = 1024
TM_MIX = 512
HALO = 8


def _cond_row(i, tm):
    row0 = i * tm
    return jnp.where(row0 < TOK_CTX, 0, 1 + (row0 - TOK_CTX) // L_LAT)


def _norm_mod(x, nw, scale, shift):
    y = x * lax.rsqrt(jnp.mean(x * x, axis=-1, keepdims=True) + EPS)
    return (y * nw) * (1.0 + scale) + shift


def _silu(x):
    return x * jax.nn.sigmoid(x)


def _ada_kernel(c_ref, w_ref, b_ref, o_ref):
    sc = _silu(c_ref[...]).astype(BF16)
    o_ref[0] = jnp.dot(sc, w_ref[0].astype(BF16), preferred_element_type=F32) + b_ref[0]


def _ada_call(cond, ada_w, ada_b):
    tn = 1536
    return pl.pallas_call(
        _ada_kernel,
        grid=(DEPTH, N_MOD * D_MODEL // tn),
        in_specs=[
            pl.BlockSpec((MOD_ROWS, D_MODEL), lambda l, j: (0, 0)),
            pl.BlockSpec((1, D_MODEL, tn), lambda l, j: (l, 0, j)),
            pl.BlockSpec((1, 1, tn), lambda l, j: (l, 0, j)),
        ],
        out_specs=pl.BlockSpec((1, MOD_ROWS, tn), lambda l, j: (l, 0, j)),
        out_shape=jax.ShapeDtypeStruct((DEPTH, MOD_ROWS, N_MOD * D_MODEL), F32),
        compiler_params=pltpu.CompilerParams(
            dimension_semantics=("arbitrary", "arbitrary"), vmem_limit_bytes=VMEM_MID),
        name="ada_mod",
    )(cond, ada_w, ada_b.reshape(DEPTH, 1, N_MOD * D_MODEL))


def _in_kernel(x_ref, mod_ref, nw_ref, w_ref, o_ref, h_ref, wb_ref):
    i, j = pl.program_id(0), pl.program_id(1)

    @pl.when(j == 0)
    def _():
        h = _norm_mod(x_ref[...], nw_ref[...], mod_ref[1:2, :], mod_ref[0:1, :])
        h_ref[...] = h.astype(BF16)

    @pl.when(i == 0)
    def _():
        wb_ref[j] = w_ref[...].astype(BF16)

    o_ref[...] = jnp.dot(h_ref[...], wb_ref[j], preferred_element_type=F32)


def _in_call(x, mod, nw, w, layer):
    n = w.shape[2]
    tm, tn = TM_DENSE, TN_DENSE
    nt = n // tn
    return pl.pallas_call(
        _in_kernel,
        grid=(TOK // tm, nt),
        in_specs=[
            pl.BlockSpec((tm, D_MODEL), lambda i, j: (i, 0)),
            pl.BlockSpec((None, MOD_ROWS, D_MODEL), lambda i, j: (_cond_row(i, tm), 0, 0)),
            pl.BlockSpec((1, D_MODEL), lambda i, j: (0, 0)),
            pl.BlockSpec((None, D_MODEL, tn), lambda i, j: (layer, 0, jnp.where(i == 0, j, nt - 1))),
        ],
        out_specs=pl.BlockSpec((tm, tn), lambda i, j: (i, j)),
        out_shape=jax.ShapeDtypeStruct((TOK, n), F32),
        scratch_shapes=[pltpu.VMEM((tm, D_MODEL), BF16), pltpu.VMEM((nt, D_MODEL, tn), BF16)],
        compiler_params=pltpu.CompilerParams(
            dimension_semantics=("arbitrary", "arbitrary"), vmem_limit_bytes=VMEM_BIG),
        name="norm_proj",
    )(x, mod, nw.reshape(1, D_MODEL), w)


def _uproj_kernel(*refs, paired, n_ctx_tiles):
    if paired:
        xc_ref, xl_ref, mod_ref, nw_ref, w_ref, u_ref, xcat_ref, wb_ref = refs
        x = jnp.where(pl.program_id(0) < n_ctx_tiles, xc_ref[...], xl_ref[...])
        xcat_ref[...] = x
    else:
        x_ref, mod_ref, nw_ref, w_ref, u_ref, wb_ref = refs
        x = x_ref[...]

    @pl.when(pl.program_id(0) == 0)
    def _():
        wb_ref[...] = w_ref[...].astype(BF16)

    h = _norm_mod(x, nw_ref[...], mod_ref[1:2, :], mod_ref[0:1, :]).astype(BF16)
    u_ref[...] = jnp.dot(h, wb_ref[...], preferred_element_type=F32)


def _uproj_call(x, mod, nw, w, layer):
    tm = TM_DENSE
    paired = isinstance(x, tuple)
    n_ctx_tiles = TOK_CTX // tm
    if paired:
        x_specs = [pl.BlockSpec((tm, D_MODEL), lambda i: (jnp.minimum(i, n_ctx_tiles - 1), 0)),
                   pl.BlockSpec((tm, D_MODEL), lambda i: (jnp.maximum(i - n_ctx_tiles, 0), 0))]
        xs = list(x)
    else:
        x_specs = [pl.BlockSpec((tm, D_MODEL), lambda i: (i, 0))]
        xs = [x]
    tok_out = lambda width: pl.BlockSpec((tm, width), lambda i: (i, 0))
    out_specs = [tok_out(S5_WIDTH)] + ([tok_out(D_MODEL)] if paired else [])
    out_shape = [jax.ShapeDtypeStruct((TOK, S5_WIDTH), F32)] + (
        [jax.ShapeDtypeStruct((TOK, D_MODEL), F32)] if paired else [])
    res = pl.pallas_call(
        functools.partial(_uproj_kernel, paired=paired, n_ctx_tiles=n_ctx_tiles),
        grid=(TOK // tm,),
        in_specs=x_specs + [
            pl.BlockSpec((None, MOD_ROWS, D_MODEL), lambda i: (_cond_row(i, tm), 0, 0)),
            pl.BlockSpec((1, D_MODEL), lambda i: (0, 0)),
            pl.BlockSpec((None, D_MODEL, S5_WIDTH), lambda i: (layer, 0, 0)),
        ],
        out_specs=out_specs,
        out_shape=out_shape,
        scratch_shapes=[pltpu.VMEM((D_MODEL, S5_WIDTH), BF16)],
        compiler_params=pltpu.CompilerParams(
            dimension_semantics=("arbitrary",), vmem_limit_bytes=VMEM_MID),
        name="norm_uproj",
    )(*xs, mod, nw.reshape(1, D_MODEL), w)
    return (res[0], res[1]) if paired else (res[0], x)


def _mlp_body(x_ref, mod_ref, nw_ref, w1_ref, w2_ref, h_ref, acc_ref):
    j = pl.program_id(1)

    @pl.when(j == 0)
    def _():
        h = _norm_mod(x_ref[...], nw_ref[...], mod_ref[4:5, :], mod_ref[3:4, :])
        h_ref[...] = h.astype(BF16)
        acc_ref[...] = jnp.zeros_like(acc_ref)

    a = jnp.maximum(jnp.dot(h_ref[...], w1_ref[...].astype(BF16), preferred_element_type=F32), 0.0)
    acc_ref[...] += jnp.dot((a * a).astype(BF16), w2_ref[...].astype(BF16), preferred_element_type=F32)


def _mlp_kernel(x_ref, mod_ref, nw_ref, w1_ref, w2_ref, o_ref, h_ref, acc_ref):
    _mlp_body(x_ref, mod_ref, nw_ref, w1_ref, w2_ref, h_ref, acc_ref)

    @pl.when(pl.program_id(1) == pl.num_programs(1) - 1)
    def _():
        o_ref[...] = x_ref[...] + mod_ref[5:6, :] * acc_ref[...]


def _mlp_final_kernel(x_ref, mod_ref, nw_ref, w1_ref, w2_ref, fw_ref, oc_ref, ol_ref, h_ref, acc_ref, *, tm):
    _mlp_body(x_ref, mod_ref, nw_ref, w1_ref, w2_ref, h_ref, acc_ref)
    i = pl.program_id(0)
    last = pl.program_id(1) == pl.num_programs(1) - 1

    def result():
        y = x_ref[...] + mod_ref[5:6, :] * acc_ref[...]
        return y * lax.rsqrt(jnp.mean(y * y, axis=-1, keepdims=True) + EPS) * fw_ref[...]

    @pl.when(jnp.logical_and(last, i < TOK_CTX // tm))
    def _():
        oc_ref[...] = result()

    @pl.when(jnp.logical_and(last, i >= TOK_CTX // tm))
    def _():
        ol_ref[...] = result()


def _mlp_call(x, mod, nw, w1, w2, layer, fw=None):
    tm = TM_DENSE
    th = TH_MLP if fw is None else TH_MLP // 2
    n_ctx_tiles = TOK_CTX // tm
    in_specs = [
        pl.BlockSpec((tm, D_MODEL), lambda i, j: (i, 0)),
        pl.BlockSpec((None, MOD_ROWS, D_MODEL), lambda i, j: (_cond_row(i, tm), 0, 0)),
        pl.BlockSpec((1, D_MODEL), lambda i, j: (0, 0)),
        pl.BlockSpec((None, D_MODEL, th), lambda i, j: (layer, 0, j)),
        pl.BlockSpec((None, th, D_MODEL), lambda i, j: (layer, j, 0)),
    ]
    scratch = [pltpu.VMEM((tm, D_MODEL), BF16), pltpu.VMEM((tm, D_MODEL), F32)]
    args = [x, mod, nw.reshape(1, D_MODEL), w1, w2]
    if fw is None:
        return pl.pallas_call(
            _mlp_kernel,
            grid=(TOK // tm, MLP_H // th),
            in_specs=in_specs,
            out_specs=pl.BlockSpec((tm, D_MODEL), lambda i, j: (i, 0)),
            out_shape=jax.ShapeDtypeStruct((TOK, D_MODEL), F32),
            scratch_shapes=scratch,
            compiler_params=pltpu.CompilerParams(
                dimension_semantics=("parallel", "arbitrary"), vmem_limit_bytes=VMEM_BIG),
            name="mlp",
        )(*args)
    return pl.pallas_call(
        functools.partial(_mlp_final_kernel, tm=tm),
        grid=(TOK // tm, MLP_H // th),
        in_specs=in_specs + [pl.BlockSpec((1, D_MODEL), lambda i, j: (0, 0))],
        out_specs=[
            pl.BlockSpec((tm, D_MODEL), lambda i, j: (jnp.minimum(i, n_ctx_tiles - 1), 0)),
            pl.BlockSpec((tm, D_MODEL), lambda i, j: (jnp.maximum(i - n_ctx_tiles, 0), 0)),
        ],
        out_shape=(jax.ShapeDtypeStruct((TOK_CTX, D_MODEL), F32),
                   jax.ShapeDtypeStruct((TOK - TOK_CTX, D_MODEL), F32)),
        scratch_shapes=scratch,
        compiler_params=pltpu.CompilerParams(
            dimension_semantics=("arbitrary", "arbitrary"), vmem_limit_bytes=VMEM_BIG),
        name="mlp_final",
    )(*args, fw.reshape(1, D_MODEL))


def _resid_kernel(a_ref, w_ref, x_ref, mod_ref, o_ref, wb_ref):
    @pl.when(pl.program_id(0) == 0)
    def _():
        wb_ref[...] = w_ref[...].astype(BF16)

    y = jnp.dot(a_ref[...], wb_ref[...], preferred_element_type=F32)
    o_ref[...] = x_ref[...] + mod_ref[2:3, :] * y


def _resid_call(a, w, x, mod, layer):
    tm = TM_DENSE
    k = a.shape[1]
    return pl.pallas_call(
        _resid_kernel,
        grid=(TOK // tm,),
        in_specs=[
            pl.BlockSpec((tm, k), lambda i: (i, 0)),
            pl.BlockSpec((None, k, D_MODEL), lambda i: (layer, 0, 0)),
            pl.BlockSpec((tm, D_MODEL), lambda i: (i, 0)),
            pl.BlockSpec((None, MOD_ROWS, D_MODEL), lambda i: (_cond_row(i, tm), 0, 0)),
        ],
        out_specs=pl.BlockSpec((tm, D_MODEL), lambda i: (i, 0)),
        out_shape=jax.ShapeDtypeStruct((TOK, D_MODEL), F32),
        scratch_shapes=[pltpu.VMEM((k, D_MODEL), BF16)],
        compiler_params=pltpu.CompilerParams(
            dimension_semantics=("arbitrary",), vmem_limit_bytes=VMEM_BIG),
        name="out_proj_resid",
    )(a, w, x, mod)


def _disc_kernel(lr_ref, li_ref, ls_ref, br_ref, bi_ref, abr_ref, abi_ref, bbr_ref, bbi_ref):
    lr, li = lr_ref[...], li_ref[...]
    dt = jnp.exp(ls_ref[...])
    mag = jnp.exp(lr * dt)
    abr = mag * jnp.cos(li * dt)
    abi = mag * jnp.sin(li * dt)
    den = lr * lr + li * li
    fr = ((abr - 1.0) * lr + abi * li) / den
    fi = (abi * lr - (abr - 1.0) * li) / den
    abr_ref[...] = abr
    abi_ref[...] = abi
    for c in range(S5_GROUP_CH):
        br, bi = br_ref[c], bi_ref[c]
        bbr_ref[c] = fr * br - fi * bi
        bbi_ref[c] = fr * bi + fi * br


def _disc_call(lam_re, lam_im, log_step, b_re, b_im):
    r = lam_re.shape[0]
    small = jax.ShapeDtypeStruct((r, S5_STATE), F32)
    big = jax.ShapeDtypeStruct((S5_GROUP_CH, r, S5_STATE), F32)
    return pl.pallas_call(_disc_kernel, out_shape=(small, small, big, big), name="s5_discretise")(
        lam_re, lam_im, log_step, b_re, b_im)


def _s5_kernel(*refs, ns, steps, aliased):
    uf_ref, ub_ref, wb_ref, wc_ref, a_ref, h0_ref = refs[:6]
    yf_ref, yb_ref, hfin_ref, utm_ref, ytm_ref, bu_ref, xs_ref, st_ref = refs[6 + (2 if aliased else 0):]
    c = pl.program_id(1)
    slabs_per_k = 2 * S5_QUARTERS
    u_refs = (uf_ref, ub_ref)
    y_refs = (yf_ref, yb_ref)

    def seq_rows(b):
        return pl.ds(b, steps, stride=ns)

    @pl.when(c == 0)
    def _():
        st_ref[...] = h0_ref[...]

    for d in range(2):
        for k in range(S5_KBLK):
            for b in range(ns):
                utm_ref[d, k, seq_rows(b), :] = u_refs[d][b, :, k * 128:(k + 1) * 128]

    for k in range(S5_KBLK):
        for d in range(2):
            res = jnp.dot(utm_ref[d, k].astype(BF16), wb_ref[d, k], preferred_element_type=F32)
            for s in range(slabs_per_k):
                bu_ref[k % 2, d, s] = res[:, s * S5_SLAB:(s + 1) * S5_SLAB]

        chains = [(d, q) for d in range(2) for q in range(S5_QUARTERS)]
        coef = {dq: (a_ref[dq[0], k, 0:ns, dq[1] * S5_SLAB:(dq[1] + 1) * S5_SLAB],
                     a_ref[dq[0], k, 0:ns, S5_BLK + dq[1] * S5_SLAB:S5_BLK + (dq[1] + 1) * S5_SLAB])
                for dq in chains}
        state = {dq: (st_ref[dq[0], k, 0:ns, dq[1] * S5_SLAB:(dq[1] + 1) * S5_SLAB],
                      st_ref[dq[0], k, 0:ns, S5_BLK + dq[1] * S5_SLAB:S5_BLK + (dq[1] + 1) * S5_SLAB])
                 for dq in chains}
        for i in range(steps):
            for d, q in chains:
                t = i if d == 0 else steps - 1 - i
                tile = slice(t * ns, (t + 1) * ns)
                (ar, ai), (xr, xi) = coef[d, q], state[d, q]
                xr, xi = (ar * xr - ai * xi + bu_ref[k % 2, d, q, tile, :],
                          ar * xi + ai * xr + bu_ref[k % 2, d, S5_QUARTERS + q, tile, :])
                xs_ref[k % 2, d, q, tile, :] = xr
                xs_ref[k % 2, d, S5_QUARTERS + q, tile, :] = xi
                state[d, q] = (xr, xi)
        for d, q in chains:
            st_ref[d, k, 0:ns, q * S5_SLAB:(q + 1) * S5_SLAB] = state[d, q][0]
            st_ref[d, k, 0:ns, S5_BLK + q * S5_SLAB:S5_BLK + (q + 1) * S5_SLAB] = state[d, q][1]

        for d in range(2):
            xk = jnp.concatenate([xs_ref[k % 2, d, s] for s in range(slabs_per_k)], axis=1).astype(BF16)
            ytm_ref[d, k] = jnp.dot(xk, wc_ref[d, k], preferred_element_type=F32)

    for d in range(2):
        for k in range(S5_KBLK):
            for b in range(ns):
                y_refs[d][b, :, k * 128:(k + 1) * 128] = ytm_ref[d, k, seq_rows(b), :]

    @pl.when(c == pl.num_programs(1) - 1)
    def _():
        hfin_ref[...] = st_ref[...]


def _s5_call(u, wb, wc, a_bar, h0, latent, y_prev=None):
    if latent:
        length, ns, blk0 = L_LAT, N_LAT, TOK_CTX // (N_LAT * L_LAT)
    else:
        length, ns, blk0 = L_CTX, SUBLANES, 0
    steps = S5_ROWS // ns
    ng, nc = h0.shape[0], length // steps
    ublk = (ns, steps, S5_WIDTH)
    stblk = (None, 2, S5_KBLK, SUBLANES, 2 * S5_BLK)
    fwd = lambda g, c: (blk0 + g, c, 0)
    bwd = lambda g, c: (blk0 + g, nc - 1 - c, 0)
    const4 = lambda g, c: (0, 0, 0, 0)
    in_specs = [
        pl.BlockSpec(ublk, fwd),
        pl.BlockSpec(ublk, bwd),
        pl.BlockSpec((2, S5_KBLK, 128, 2 * S5_BLK), const4),
        pl.BlockSpec((2, S5_KBLK, 2 * S5_BLK, 128), const4),
        pl.BlockSpec((2, S5_KBLK, SUBLANES, 2 * S5_BLK), const4),
        pl.BlockSpec(stblk, lambda g, c: (g, 0, 0, 0, 0)),
    ]
    u3 = u.reshape(TOK // length, length, S5_WIDTH)
    args = [u3, u3, wb, wc, a_bar, h0]
    aliases = {}
    if y_prev is not None:
        in_specs += [pl.BlockSpec(memory_space=pl.ANY)] * 2
        args += [y.reshape(TOK // length, length, S5_WIDTH) for y in y_prev]
        aliases = {6: 0, 7: 1}
    y_shape = jax.ShapeDtypeStruct((TOK // length, length, S5_WIDTH), F32)
    yf, yb, hfin = pl.pallas_call(
        functools.partial(_s5_kernel, ns=ns, steps=steps, aliased=y_prev is not None),
        grid=(ng, nc),
        in_specs=in_specs,
        out_specs=[
            pl.BlockSpec(ublk, fwd),
            pl.BlockSpec(ublk, bwd),
            pl.BlockSpec(stblk, lambda g, c: (g, 0, 0, 0, 0)),
        ],
        out_shape=(y_shape, y_shape, jax.ShapeDtypeStruct(h0.shape, F32)),
        scratch_shapes=[
            pltpu.VMEM((2, S5_KBLK, S5_ROWS, S5_SLAB), F32),
            pltpu.VMEM((2, S5_KBLK, S5_ROWS, S5_SLAB), F32),
            pltpu.VMEM((2, 2, 2 * S5_QUARTERS, S5_ROWS, S5_SLAB), F32),
            pltpu.VMEM((2, 2, 2 * S5_QUARTERS, S5_ROWS, S5_SLAB), F32),
            pltpu.VMEM((2, S5_KBLK, SUBLANES, 2 * S5_BLK), F32),
        ],
        input_output_aliases=aliases,
        compiler_params=pltpu.CompilerParams(
            dimension_semantics=("parallel", "arbitrary"), vmem_limit_bytes=VMEM_BIG),
        name="s5_scan_latent" if latent else "s5_scan_context",
    )(*args)
    return yf.reshape(TOK, S5_WIDTH), yb.reshape(TOK, S5_WIDTH), hfin


def _s5_weights(ab_re, ab_im, bb_re, bb_im, c_re, c_im):
    eye = jnp.eye(S5_GPB, dtype=F32)

    def in_side(bb):
        t = bb.reshape(S5_GROUP_CH, 2, S5_KBLK, S5_GPB, S5_STATE)
        t = jnp.einsum('cdkgp,gh->dkgchp', t, eye)
        return t.reshape(2, S5_KBLK, S5_GPB * S5_GROUP_CH, S5_BLK)

    def out_side(cc):
        t = cc.reshape(2, S5_KBLK, S5_GPB, S5_GROUP_CH, S5_STATE)
        t = jnp.einsum('dkgcp,gh->dkgphc', t, eye)
        return t.reshape(2, S5_KBLK, S5_BLK, S5_GPB * S5_GROUP_CH)

    wb = jnp.concatenate([in_side(bb_re), in_side(bb_im)], axis=-1).astype(BF16)
    wc = jnp.concatenate([out_side(c_re), -out_side(c_im)], axis=-2).astype(BF16)
    a = jnp.concatenate([ab_re.reshape(2, S5_KBLK, 1, S5_BLK), ab_im.reshape(2, S5_KBLK, 1, S5_BLK)], axis=-1)
    a = jnp.broadcast_to(a, (2, S5_KBLK, SUBLANES, 2 * S5_BLK))
    return wb, wc, a


def _s5_state_in(s_re, s_im):
    def blk(s):
        return s.reshape(-1, 2, S5_KBLK, S5_BLK).transpose(1, 2, 0, 3)
    h = jnp.concatenate([blk(s_re), blk(s_im)], axis=-1)
    h = jnp.pad(h, ((0, 0), (0, 0), (0, SUBLANES - h.shape[2]), (0, 0)))
    return h[None]


def _s5_state_out(h):
    def unblk(s):
        return s.transpose(0, 3, 1, 2, 4).reshape(-1, 2, S5_GROUPS, S5_STATE)
    return unblk(h[..., :S5_BLK]), unblk(h[..., S5_BLK:])


def _gelu_tanh(x):
    return x * (0.5 * (1.0 + jnp.tanh(math.sqrt(2.0 / math.pi) * (x + 0.044715 * (x * x * x)))))


def _mix_kernel(yf_ref, yb_ref, u_ref, x_ref, xp_ref, xn_ref, mod_ref, nw_ref, wbg_ref, wcg_ref, wv_ref,
                d_ref, gw_ref, gb_ref, cw_ref, cb_ref, ow_ref, o_ref, wbcv_ref, gwb_ref, owb_ref):
    tm = TM_MIX
    ext = tm + 2 * HALO

    @pl.when(pl.program_id(0) == 0)
    def _():
        wbcv_ref[:, 0:CONV_W] = wbg_ref[...].astype(BF16)
        wbcv_ref[:, CONV_W:2 * CONV_W] = wcg_ref[...].astype(BF16)
        wbcv_ref[:, 2 * CONV_W:3 * CONV_W] = wv_ref[...].astype(BF16)
        gwb_ref[...] = gw_ref[...].astype(BF16)
        owb_ref[...] = ow_ref[...].astype(BF16)

    y = yf_ref[...] + yb_ref[...] + d_ref[...] * u_ref[...]
    z = _gelu_tanh(y)
    gate = jax.nn.sigmoid(jnp.dot(z.astype(BF16), gwb_ref[...], preferred_element_type=F32) + gb_ref[...])
    a_out = z * gate

    x = x_ref[...]
    xe = jnp.concatenate([xp_ref[...], x, xn_ref[...]], axis=0)
    h = _norm_mod(xe, nw_ref[...], mod_ref[1:2, :], mod_ref[0:1, :]).astype(BF16)
    bcv = jnp.dot(h, wbcv_ref[...], preferred_element_type=F32)
    bg = bcv[HALO:HALO + tm, 0:CONV_W]
    pe = bcv[:, CONV_W:2 * CONV_W] * bcv[:, 2 * CONV_W:3 * CONV_W]
    p = pe[HALO:HALO + tm, :]
    local = lax.broadcasted_iota(jnp.int32, (tm, 1), 0)
    row = pl.program_id(0) * tm + local
    seq_len = jnp.where(row < TOK_CTX, L_CTX, L_LAT)
    pos = jnp.bitwise_and(row, seq_len - 1)
    p_prev = jnp.where(pos == 0, 0.0, pltpu.roll(pe, 1, 0)[HALO:HALO + tm, :])
    p_next = jnp.where(pos == seq_len - 1, 0.0, pltpu.roll(pe, ext - 1, 0)[HALO:HALO + tm, :])
    conv = cw_ref[0:1, :] * p_prev + cw_ref[1:2, :] * p + cw_ref[2:3, :] * p_next + cb_ref[...]
    b_out = bg * conv

    out = (jnp.dot(a_out.astype(BF16), owb_ref[0:S5_WIDTH, :], preferred_element_type=F32)
           + jnp.dot(b_out.astype(BF16), owb_ref[S5_WIDTH:, :], preferred_element_type=F32))
    o_ref[...] = x + mod_ref[2:3, :] * out


def _mix_call(yf, yb, u, x, mod, nw, in_w, s5_d, glu_w, glu_b, conv_w, conv_b, out_w, layer):
    tm = TM_MIX
    hb = tm // HALO
    last_halo = TOK // HALO - 1
    tok_blk = lambda width: pl.BlockSpec((tm, width), lambda i: (i, 0))
    row1 = lambda width: pl.BlockSpec((1, width), lambda i: (0, 0))
    once = dict(pipeline_mode=pl.Buffered(1))
    w_col = lambda col: pl.BlockSpec((None, D_MODEL, CONV_W), lambda i: (layer, 0, col), **once)
    return pl.pallas_call(
        _mix_kernel,
        grid=(TOK // tm,),
        in_specs=[
            tok_blk(S5_WIDTH), tok_blk(S5_WIDTH), tok_blk(S5_WIDTH), tok_blk(D_MODEL),
            pl.BlockSpec((HALO, D_MODEL), lambda i: (jnp.maximum(i * hb - 1, 0), 0)),
            pl.BlockSpec((HALO, D_MODEL), lambda i: (jnp.minimum((i + 1) * hb, last_halo), 0)),
            pl.BlockSpec((None, MOD_ROWS, D_MODEL), lambda i: (_cond_row(i, tm), 0, 0)),
            row1(D_MODEL),
            w_col(1), w_col(2), w_col(3),
            row1(CONV_W),
            pl.BlockSpec((None, S5_WIDTH, S5_WIDTH), lambda i: (layer, 0, 0), **once),
            row1(CONV_W),
            pl.BlockSpec((MOD_ROWS, CONV_W), lambda i: (0, 0)),
            row1(CONV_W),
            pl.BlockSpec((None, D_MODEL, D_MODEL), lambda i: (layer, 0, 0), **once),
        ],
        out_specs=pl.BlockSpec((tm, D_MODEL), lambda i: (i, 0)),
        out_shape=jax.ShapeDtypeStruct((TOK, D_MODEL), F32),
        scratch_shapes=[pltpu.VMEM((D_MODEL, 3 * CONV_W), BF16), pltpu.VMEM((S5_WIDTH, S5_WIDTH), BF16),
                        pltpu.VMEM((D_MODEL, D_MODEL), BF16)],
        compiler_params=pltpu.CompilerParams(
            dimension_semantics=("arbitrary",), vmem_limit_bytes=VMEM_BIG),
        name="glu_conv_out",
    )(yf, yb, u, x, x, x, mod, nw.reshape(1, D_MODEL), in_w, in_w, in_w,
      s5_d.reshape(1, -1), glu_w, glu_b.reshape(1, -1),
      jnp.pad(conv_w, ((0, MOD_ROWS - conv_w.shape[0]), (0, 0))), conv_b.reshape(1, -1), out_w)


def _log_sigmoid(x):
    return jnp.minimum(x, 0.0) - jnp.log1p(jnp.exp(-jnp.abs(x)))


def _ret_kernel(*refs, latent, aliased):
    q_ref, k_ref, v_ref, g_ref, gam_ref, gn_ref = refs[:6]
    n_in = 6
    if latent:
        cos_ref, sin_ref, s0_ref = refs[6:9]
        n_in = 9
    n_in += aliased
    o_ref, sfin_ref, qs_ref, sc_ref, vb_ref, kv_ref, sk_ref = refs[n_in:]
    seq_chunks = RET_NC if latent else L_CTX // RET_C

    lg_f = _log_sigmoid(gam_ref[0, 0:1, :])
    lg_b = _log_sigmoid(gam_ref[1, 0:1, :])
    ii = lax.broadcasted_iota(jnp.int32, (RET_C, RET_C), 0)
    jj = lax.broadcasted_iota(jnp.int32, (RET_C, RET_C), 1)
    diff = (ii - jj).astype(F32)
    lgf_k, lgb_k = lg_f[:, :RET_DK], lg_b[:, :RET_DK]
    intra = (jnp.where(diff >= 0, jnp.exp(lgf_k * jnp.maximum(diff, 0.0)), 0.0)
             + jnp.where(diff <= 0, jnp.exp(lgb_k * jnp.maximum(-diff, 0.0)), 0.0))
    tk = lax.broadcasted_iota(jnp.int32, (RET_C, RET_DK), 0).astype(F32)
    tv = lax.broadcasted_iota(jnp.int32, (RET_C, RET_DV), 0).astype(F32)
    kdec_f = jnp.exp(lgf_k * (RET_C - 1.0 - tk))
    kdec_b = jnp.exp(lgb_k * tk)
    qdec_f = jnp.exp(lg_f * (tv + 1.0))
    qdec_b = jnp.exp(lg_b * (RET_C - tv))
    cd_f = jnp.exp(lg_f * RET_C)
    cd_b = jnp.exp(lg_b * RET_C)

    lane = lax.broadcasted_iota(jnp.int32, (RET_C, RET_DK), 1)
    first_half = jnp.bitwise_and(lane, 2 * (RET_DK // 4) - 1) < RET_DK // 4

    def rotate(x, cos, sin):
        swapped = jnp.where(first_half, pltpu.roll(x, RET_DK - RET_DK // 4, 1), pltpu.roll(x, RET_DK // 4, 1))
        return x * cos + swapped * sin

    tn_dims = (((0,), (0,)), ((), ()))
    nt_dims = (((1,), (1,)), ((), ()))

    def prep(c, _):
        r = pl.ds(pl.multiple_of(c * RET_C, RET_C), RET_C)
        q, k = q_ref[r, :], k_ref[r, :]
        if latent:
            cos, sin = cos_ref[r, :], sin_ref[r, :]
            q, k = rotate(q, cos, sin), rotate(k, cos, sin)
        qs = (q * (RET_DK ** -0.5)).astype(BF16)
        qs_ref[r, :] = qs
        scores = lax.dot_general(qs, k.astype(BF16), nt_dims, preferred_element_type=F32) * intra
        sc_ref[c] = scores.astype(BF16)
        kd = jnp.concatenate([(k * kdec_f).astype(BF16), (k * kdec_b).astype(BF16)], axis=1)
        vb = v_ref[r, :].astype(BF16)
        vb_ref[r, :] = vb
        kv_ref[c] = lax.dot_general(kd, vb, tn_dims, preferred_element_type=F32)
        return 0

    lax.fori_loop(0, RET_NC, prep, 0, unroll=RET_UNROLL)

    zeros = jnp.zeros((RET_DK, RET_DV), F32)
    s = s0_ref[0] if latent else zeros
    for c in range(RET_NC):
        if c % seq_chunks == 0 and not (latent and c == 0):
            s = zeros
        sk_ref[c, :, 0:RET_DV] = s.astype(BF16)
        s = cd_f * s + kv_ref[c, 0:RET_DK, :]
        if (c + 1) % seq_chunks == 0:
            sfin_ref[c // seq_chunks, 0] = s
    s = s0_ref[1] if latent else zeros
    for c in reversed(range(RET_NC)):
        if (c + 1) % seq_chunks == 0 and not (latent and c == RET_NC - 1):
            s = zeros
        sk_ref[c, :, RET_DV:2 * RET_DV] = s.astype(BF16)
        s = cd_b * s + kv_ref[c, RET_DK:2 * RET_DK, :]
        if c % seq_chunks == 0:
            sfin_ref[c // seq_chunks, 1] = s

    def emit(c, _):
        r = pl.ds(pl.multiple_of(c * RET_C, RET_C), RET_C)
        cross = jnp.dot(qs_ref[r, :], sk_ref[c], preferred_element_type=F32)
        o = (jnp.dot(sc_ref[c], vb_ref[r, :], preferred_element_type=F32)
             + cross[:, 0:RET_DV] * qdec_f + cross[:, RET_DV:2 * RET_DV] * qdec_b)
        o = o * lax.rsqrt(jnp.mean(o * o, axis=-1, keepdims=True) + EPS) * gn_ref[...]
        o_ref[r, :] = (_silu(g_ref[r, :]) * o).astype(o_ref.dtype)
        return 0

    lax.fori_loop(0, RET_NC, emit, 0, unroll=RET_UNROLL)


def _ret_call(proj, gam, gn_w, latent, layer, rope=None, s0=None, o_prev=None, s_prev=None):
    nblk = TOK_CTX // RET_ROWS
    blk0 = nblk if latent else 0
    seqs = 1 if latent else RET_ROWS // L_CTX
    kcol, vcol, gcol = RET_QK // RET_DK, 2 * RET_QK // RET_DV, (2 * RET_QK + RET_V) // RET_DV
    in_specs = [
        pl.BlockSpec((RET_ROWS, RET_DK), lambda b, h: (blk0 + b, h)),
        pl.BlockSpec((RET_ROWS, RET_DK), lambda b, h: (blk0 + b, kcol + h)),
        pl.BlockSpec((RET_ROWS, RET_DV), lambda b, h: (blk0 + b, vcol + h)),
        pl.BlockSpec((RET_ROWS, RET_DV), lambda b, h: (blk0 + b, gcol + h)),
        pl.BlockSpec((None, 2, SUBLANES, RET_DV), lambda b, h: (h, 0, 0, 0)),
        pl.BlockSpec((1, RET_DV), lambda b, h: (0, h)),
    ]
    args = [proj, proj, proj, proj, gam, gn_w.reshape(1, RET_V)]
    aliases = {}
    if latent:
        in_specs += [
            pl.BlockSpec((RET_ROWS, RET_DK), lambda b, h: (0, 0)),
            pl.BlockSpec((RET_ROWS, RET_DK), lambda b, h: (0, 0)),
            pl.BlockSpec((None, 2, None, RET_DK, RET_DV), lambda b, h: (b, 0, h, 0, 0)),
            pl.BlockSpec(memory_space=pl.ANY),
        ]
        args += [rope[0], rope[1], s0, o_prev]
        aliases = {len(args) - 1: 0}
        s_shape = jax.ShapeDtypeStruct((N_LAT, 1, 2, RET_H, RET_DK, RET_DV), F32)
        s_layer = 0
    else:
        s_shape = jax.ShapeDtypeStruct((N_CTX, DEPTH // 2, 2, RET_H, RET_DK, RET_DV), F32)
        s_layer = layer
        if s_prev is not None:
            in_specs += [pl.BlockSpec(memory_space=pl.ANY)]
            args += [s_prev]
            aliases = {len(args) - 1: 1}
    return pl.pallas_call(
        functools.partial(_ret_kernel, latent=latent, aliased=len(aliases)),
        grid=(nblk, RET_H),
        in_specs=in_specs,
        out_specs=[
            pl.BlockSpec((RET_ROWS, RET_DV), lambda b, h: (blk0 + b, h)),
            pl.BlockSpec((seqs, None, 2, None, RET_DK, RET_DV), lambda b, h: (b, s_layer, 0, h, 0, 0)),
        ],
        out_shape=(jax.ShapeDtypeStruct((TOK, RET_V), BF16), s_shape),
        scratch_shapes=[
            pltpu.VMEM((RET_ROWS, RET_DK), BF16),
            pltpu.VMEM((RET_NC, RET_C, RET_C), BF16),
            pltpu.VMEM((RET_ROWS, RET_DV), BF16),
            pltpu.VMEM((RET_NC, 2 * RET_DK, RET_DV), F32),
            pltpu.VMEM((RET_NC, RET_DK, 2 * RET_DV), BF16),
        ],
        input_output_aliases=aliases,
        compiler_params=pltpu.CompilerParams(
            dimension_semantics=("parallel", "arbitrary"), vmem_limit_bytes=VMEM_MID),
        name="retention_latent" if latent else "retention_context",
    )(*args)


def _rope_tables():
    n_freq = RET_DK // 4
    t = jnp.arange(L_LAT)
    row = (t // GRID_W).astype(F32)
    col = (t % GRID_W).astype(F32)
    inv_freq = jnp.power(ROPE_BASE, -jnp.arange(n_freq, dtype=F32) / n_freq)
    ar, ac = row[:, None] * inv_freq, col[:, None] * inv_freq
    cos = jnp.concatenate([jnp.cos(ar), jnp.cos(ar), jnp.cos(ac), jnp.cos(ac)], axis=-1)
    sin = jnp.concatenate([-jnp.sin(ar), jnp.sin(ar), -jnp.sin(ac), jnp.sin(ac)], axis=-1)
    return cos, sin


def kernel(x_prompt, x_sample, state_s5_re, state_s5_im, state_ret, c, c_ctx, norm1_w, norm2_w, ada_w, ada_b,
           hy_in_w, hy_out_w, s5_lam_re, s5_lam_im, s5_log_step, s5_b_re, s5_b_im, s5_c_re, s5_c_im,
           s5_d, s5_glu_w, s5_glu_b, conv_w, conv_b, ret_in_w, ret_out_w, ret_gamma_logit, ret_gn_w,
           mlp_w1, mlp_w2, final_norm_w):
    x = (x_prompt.reshape(TOK_CTX, D_MODEL), x_sample.reshape(TOK - TOK_CTX, D_MODEL))

    cond = jnp.concatenate([c_ctx[None, :], c], axis=0)
    cond = jnp.pad(cond, ((0, MOD_ROWS - N_COND), (0, 0)))
    mod_all = _ada_call(cond, ada_w, ada_b)
    mod_all = mod_all[:, :N_COND].reshape(DEPTH, N_COND, N_MOD, D_MODEL)
    mod_all = jnp.pad(mod_all, ((0, 0), (0, 0), (0, MOD_ROWS - N_MOD), (0, 0)))

    n_s5 = s5_lam_re.shape[0]
    rows = n_s5 * 2 * S5_GROUPS
    ab_re, ab_im, bb_re, bb_im = _disc_call(
        s5_lam_re.reshape(rows, S5_STATE), s5_lam_im.reshape(rows, S5_STATE),
        jnp.broadcast_to(s5_log_step.reshape(rows, 1), (rows, S5_STATE)),
        jnp.moveaxis(s5_b_re, -1, 0).reshape(S5_GROUP_CH, rows, S5_STATE),
        jnp.moveaxis(s5_b_im, -1, 0).reshape(S5_GROUP_CH, rows, S5_STATE))
    ab_re = ab_re.reshape(n_s5, 2, S5_GROUPS, S5_STATE)
    ab_im = ab_im.reshape(n_s5, 2, S5_GROUPS, S5_STATE)
    bb_re = bb_re.reshape(S5_GROUP_CH, n_s5, 2, S5_GROUPS, S5_STATE)
    bb_im = bb_im.reshape(S5_GROUP_CH, n_s5, 2, S5_GROUPS, S5_STATE)

    rope = _rope_tables()
    zero_state = jnp.zeros((N_CTX // SUBLANES, 2, S5_KBLK, SUBLANES, 2 * S5_BLK), F32)

    new_re, new_im, new_ret = [], [], None
    y_prompt = y_sample = None
    for i in range(DEPTH):
        j = i // 2
        mod = mod_all[i]
        if i % 2 == 0:
            u, x = _uproj_call(x, mod, norm1_w[i], hy_in_w, j)
            wb, wc, a_bar = _s5_weights(ab_re[j], ab_im[j], bb_re[:, j], bb_im[:, j], s5_c_re[j], s5_c_im[j])
            yf, yb, h_ctx = _s5_call(u, wb, wc, a_bar, zero_state, latent=False)
            yf, yb, _ = _s5_call(u, wb, wc, a_bar, _s5_state_in(state_s5_re[:, j], state_s5_im[:, j]),
                                 latent=True, y_prev=(yf, yb))
            x = _mix_call(yf, yb, u, x, mod, norm1_w[i], hy_in_w, s5_d[j], s5_glu_w, s5_glu_b[j],
                          conv_w[j], conv_b[j], hy_out_w, j)
            s_re, s_im = _s5_state_out(h_ctx)
            new_re.append(s_re)
            new_im.append(s_im)
        else:
            proj = _in_call(x, mod, norm1_w[i], ret_in_w, j)
            gam = jnp.broadcast_to(ret_gamma_logit[j].T[:, :, None, None], (RET_H, 2, SUBLANES, RET_DV))
            o, new_ret = _ret_call(proj, gam, ret_gn_w[j], latent=False, layer=j, s_prev=new_ret)
            o, _ = _ret_call(proj, gam, ret_gn_w[j], latent=True, layer=j, rope=rope, s0=state_ret[:, j], o_prev=o)
            x = _resid_call(o, ret_out_w, x, mod, j)
        if i < DEPTH - 1:
            x = _mlp_call(x, mod, norm2_w[i], mlp_w1, mlp_w2, i)
        else:
            y_prompt, y_sample = _mlp_call(x, mod, norm2_w[i], mlp_w1, mlp_w2, i, fw=final_norm_w)

    return (y_prompt.reshape(N_CTX, L_CTX, D_MODEL), y_sample.reshape(N_LAT, L_LAT, D_MODEL),
            jnp.stack(new_re, 1), jnp.stack(new_im, 1), new_ret)
```

```python
import functools
import math

import jax
import jax.numpy as jnp
from jax import lax
from jax.experimental import pallas as pl
from jax.experimental.pallas import tpu as pltpu

F32 = jnp.float32
BF16 = jnp.bfloat16

D_MODEL = 1024
DEPTH = 4
N_CTX, L_CTX = 32, 256
N_LAT, L_LAT = 4, 2048
TOK_CTX = N_CTX * L_CTX
TOK = TOK_CTX + N_LAT * L_LAT
N_COND = 1 + N_LAT
GRID_W = 64
EPS = 1e-6

S5_WIDTH = 512
S5_GROUP_CH = 16
S5_GROUPS = 32
S5_STATE = 64
S5_KBLK = 4
S5_GPB = S5_GROUPS // S5_KBLK
S5_BLK = S5_GPB * S5_STATE
S5_SLAB = 128
S5_QUARTERS = S5_BLK // S5_SLAB
S5_ROWS = 512
SUBLANES = 8
CONV_W = 512

RET_H = 8
RET_DK = 128
RET_DV = 256
RET_QK = RET_H * RET_DK
RET_V = RET_H * RET_DV
RET_IN = 2 * RET_QK + 2 * RET_V
RET_C = 128
RET_ROWS = 2048
RET_NC = RET_ROWS // RET_C
RET_UNROLL = 8
ROPE_BASE = 10000.0
MLP_H = 4 * D_MODEL
N_MOD = 6
MOD_ROWS = 8

VMEM_BIG = 56 * 1024 * 1024
VMEM_MID = 40 * 1024 * 1024

TM_DENSE = 1024
TN_DENSE = 1024
TM_PROJ = 512
TM_MLP = 512
MLP_CHUNKS = 8
TM_MIX = 512
HALO = 8


def _cond_row(i, tm):
    row0 = i * tm
    return jnp.where(row0 < TOK_CTX, 0, 1 + (row0 - TOK_CTX) // L_LAT)


def _norm_mod(x, nw, scale, shift):
    y = x * lax.rsqrt(jnp.mean(x * x, axis=-1, keepdims=True) + EPS)
    return (y * nw) * (1.0 + scale) + shift


def _silu(x):
    return x * jax.nn.sigmoid(x)


def _ada_kernel(c_ref, w_ref, b_ref, o_ref):
    sc = _silu(c_ref[...]).astype(BF16)
    o_ref[0] = jnp.dot(sc, w_ref[0].astype(BF16), preferred_element_type=F32) + b_ref[0]


def _ada_call(cond, ada_w, ada_b):
    tn = 1536
    return pl.pallas_call(
        _ada_kernel,
        grid=(DEPTH, N_MOD * D_MODEL // tn),
        in_specs=[
            pl.BlockSpec((MOD_ROWS, D_MODEL), lambda l, j: (0, 0)),
            pl.BlockSpec((1, D_MODEL, tn), lambda l, j: (l, 0, j)),
            pl.BlockSpec((1, 1, tn), lambda l, j: (l, 0, j)),
        ],
        out_specs=pl.BlockSpec((1, MOD_ROWS, tn), lambda l, j: (l, 0, j)),
        out_shape=jax.ShapeDtypeStruct((DEPTH, MOD_ROWS, N_MOD * D_MODEL), F32),
        compiler_params=pltpu.CompilerParams(
            dimension_semantics=("arbitrary", "arbitrary"), vmem_limit_bytes=VMEM_MID),
        name="ada_mod",
    )(cond, ada_w, ada_b.reshape(DEPTH, 1, N_MOD * D_MODEL))


def _in_kernel(x_ref, mod_ref, nw_ref, w_ref, o_ref, wb_ref, *, n_chunks, cw):
    s = pl.program_id(0)

    @pl.when(s < n_chunks)
    def _():
        wb_ref[s] = w_ref[...].astype(BF16)

    @pl.when(s >= n_chunks)
    def _():
        h = _norm_mod(x_ref[...], nw_ref[...], mod_ref[1:2, :], mod_ref[0:1, :]).astype(BF16)
        for c in range(n_chunks):
            o_ref[:, c * cw:(c + 1) * cw] = jnp.dot(h, wb_ref[c], preferred_element_type=F32)


def _in_call(x, mod, nw, w, layer):
    n = w.shape[2]
    tm, cw = TM_PROJ, TN_DENSE
    n_chunks = n // cw
    tile = lambda s: jnp.maximum(s - n_chunks, 0)
    return pl.pallas_call(
        functools.partial(_in_kernel, n_chunks=n_chunks, cw=cw),
        grid=(n_chunks + TOK // tm,),
        in_specs=[
            pl.BlockSpec((tm, D_MODEL), lambda s: (tile(s), 0)),
            pl.BlockSpec((None, MOD_ROWS, D_MODEL), lambda s: (_cond_row(tile(s), tm), 0, 0)),
            pl.BlockSpec((1, D_MODEL), lambda s: (0, 0)),
            pl.BlockSpec((None, D_MODEL, cw), lambda s: (layer, 0, jnp.minimum(s, n_chunks - 1))),
        ],
        out_specs=pl.BlockSpec((tm, n), lambda s: (tile(s), 0)),
        out_shape=jax.ShapeDtypeStruct((TOK, n), F32),
        scratch_shapes=[pltpu.VMEM((n_chunks, D_MODEL, cw), BF16)],
        compiler_params=pltpu.CompilerParams(
            dimension_semantics=("arbitrary",), vmem_limit_bytes=VMEM_BIG),
        name="norm_proj",
    )(x, mod, nw.reshape(1, D_MODEL), w)


def _uproj_kernel(*refs, paired, n_ctx_tiles):
    if paired:
        xc_ref, xl_ref, mod_ref, nw_ref, w_ref, u_ref, xcat_ref, wb_ref = refs
        x = jnp.where(pl.program_id(0) < n_ctx_tiles, xc_ref[...], xl_ref[...])
        xcat_ref[...] = x
    else:
        x_ref, mod_ref, nw_ref, w_ref, u_ref, wb_ref = refs
        x = x_ref[...]

    @pl.when(pl.program_id(0) == 0)
    def _():
        wb_ref[...] = w_ref[...].astype(BF16)

    h = _norm_mod(x, nw_ref[...], mod_ref[1:2, :], mod_ref[0:1, :]).astype(BF16)
    u_ref[...] = jnp.dot(h, wb_ref[...], preferred_element_type=F32)


def _uproj_call(x, mod, nw, w, layer):
    tm = TM_DENSE
    paired = isinstance(x, tuple)
    n_ctx_tiles = TOK_CTX // tm
    if paired:
        x_specs = [pl.BlockSpec((tm, D_MODEL), lambda i: (jnp.minimum(i, n_ctx_tiles - 1), 0)),
                   pl.BlockSpec((tm, D_MODEL), lambda i: (jnp.maximum(i - n_ctx_tiles, 0), 0))]
        xs = list(x)
    else:
        x_specs = [pl.BlockSpec((tm, D_MODEL), lambda i: (i, 0))]
        xs = [x]
    tok_out = lambda width: pl.BlockSpec((tm, width), lambda i: (i, 0))
    out_specs = [tok_out(S5_WIDTH)] + ([tok_out(D_MODEL)] if paired else [])
    out_shape = [jax.ShapeDtypeStruct((TOK, S5_WIDTH), F32)] + (
        [jax.ShapeDtypeStruct((TOK, D_MODEL), F32)] if paired else [])
    res = pl.pallas_call(
        functools.partial(_uproj_kernel, paired=paired, n_ctx_tiles=n_ctx_tiles),
        grid=(TOK // tm,),
        in_specs=x_specs + [
            pl.BlockSpec((None, MOD_ROWS, D_MODEL), lambda i: (_cond_row(i, tm), 0, 0)),
            pl.BlockSpec((1, D_MODEL), lambda i: (0, 0)),
            pl.BlockSpec((None, D_MODEL, S5_WIDTH), lambda i: (layer, 0, 0)),
        ],
        out_specs=out_specs,
        out_shape=out_shape,
        scratch_shapes=[pltpu.VMEM((D_MODEL, S5_WIDTH), BF16)],
        compiler_params=pltpu.CompilerParams(
            dimension_semantics=("arbitrary",), vmem_limit_bytes=VMEM_MID),
        name="norm_uproj",
    )(*xs, mod, nw.reshape(1, D_MODEL), w)
    return (res[0], res[1]) if paired else (res[0], x)


def _mlp_kernel(*refs, final, n_ctx_tiles):
    if final:
        x_ref, mod_ref, nw_ref, w1_ref, w2_ref, fw_ref, oc_ref, ol_ref, w1b_ref, w2b_ref = refs
    else:
        x_ref, mod_ref, nw_ref, w1_ref, w2_ref, o_ref, w1b_ref, w2b_ref = refs
    s = pl.program_id(0)

    @pl.when(s < MLP_CHUNKS)
    def _():
        w1b_ref[s] = w1_ref[...].astype(BF16)
        w2b_ref[s] = w2_ref[...].astype(BF16)

    def result():
        x = x_ref[...]
        h = _norm_mod(x, nw_ref[...], mod_ref[4:5, :], mod_ref[3:4, :]).astype(BF16)
        acc = None
        for c in range(MLP_CHUNKS):
            a = jnp.maximum(jnp.dot(h, w1b_ref[c], preferred_element_type=F32), 0.0)
            t = jnp.dot((a * a).astype(BF16), w2b_ref[c], preferred_element_type=F32)
            acc = t if acc is None else acc + t
        y = x + mod_ref[5:6, :] * acc
        if final:
            y = y * lax.rsqrt(jnp.mean(y * y, axis=-1, keepdims=True) + EPS) * fw_ref[...]
        return y

    tile = s - MLP_CHUNKS
    if final:
        @pl.when(jnp.logical_and(tile >= 0, tile < n_ctx_tiles))
        def _():
            oc_ref[...] = result()

        @pl.when(tile >= n_ctx_tiles)
        def _():
            ol_ref[...] = result()
    else:
        @pl.when(tile >= 0)
        def _():
            o_ref[...] = result()


def _mlp_call(x, mod, nw, w1, w2, layer, fw=None):
    tm = TM_MLP
    final = fw is not None
    n_tiles, n_ctx_tiles = TOK // tm, TOK_CTX // tm
    ch = MLP_H // MLP_CHUNKS
    tile = lambda s: jnp.maximum(s - MLP_CHUNKS, 0)
    chunk = lambda s: jnp.minimum(s, MLP_CHUNKS - 1)
    in_specs = [
        pl.BlockSpec((tm, D_MODEL), lambda s: (tile(s), 0)),
        pl.BlockSpec((None, MOD_ROWS, D_MODEL), lambda s: (_cond_row(tile(s), tm), 0, 0)),
        pl.BlockSpec((1, D_MODEL), lambda s: (0, 0)),
        pl.BlockSpec((None, D_MODEL, ch), lambda s: (layer, 0, chunk(s))),
        pl.BlockSpec((None, ch, D_MODEL), lambda s: (layer, chunk(s), 0)),
    ]
    args = [x, mod, nw.reshape(1, D_MODEL), w1, w2]
    if final:
        in_specs.append(pl.BlockSpec((1, D_MODEL), lambda s: (0, 0)))
        args.append(fw.reshape(1, D_MODEL))
        out_specs = [
            pl.BlockSpec((tm, D_MODEL), lambda s: (jnp.clip(s - MLP_CHUNKS, 0, n_ctx_tiles - 1), 0)),
            pl.BlockSpec((tm, D_MODEL), lambda s: (jnp.maximum(s - MLP_CHUNKS - n_ctx_tiles, 0), 0)),
        ]
        out_shape = (jax.ShapeDtypeStruct((TOK_CTX, D_MODEL), F32),
                     jax.ShapeDtypeStruct((TOK - TOK_CTX, D_MODEL), F32))
    else:
        out_specs = pl.BlockSpec((tm, D_MODEL), lambda s: (tile(s), 0))
        out_shape = jax.ShapeDtypeStruct((TOK, D_MODEL), F32)
    return pl.pallas_call(
        functools.partial(_mlp_kernel, final=final, n_ctx_tiles=n_ctx_tiles),
        grid=(MLP_CHUNKS + n_tiles,),
        in_specs=in_specs,
        out_specs=out_specs,
        out_shape=out_shape,
        scratch_shapes=[pltpu.VMEM((MLP_CHUNKS, D_MODEL, ch), BF16), pltpu.VMEM((MLP_CHUNKS, ch, D_MODEL), BF16)],
        compiler_params=pltpu.CompilerParams(
            dimension_semantics=("arbitrary",), vmem_limit_bytes=VMEM_BIG),
        name="mlp_final" if final else "mlp",
    )(*args)


def _resid_kernel(a_ref, w_ref, x_ref, mod_ref, o_ref, wb_ref):
    @pl.when(pl.program_id(0) == 0)
    def _():
        wb_ref[...] = w_ref[...].astype(BF16)

    y = jnp.dot(a_ref[...], wb_ref[...], preferred_element_type=F32)
    o_ref[...] = x_ref[...] + mod_ref[2:3, :] * y


def _resid_call(a, w, x, mod, layer):
    tm = TM_DENSE
    k = a.shape[1]
    return pl.pallas_call(
        _resid_kernel,
        grid=(TOK // tm,),
        in_specs=[
            pl.BlockSpec((tm, k), lambda i: (i, 0)),
            pl.BlockSpec((None, k, D_MODEL), lambda i: (layer, 0, 0)),
            pl.BlockSpec((tm, D_MODEL), lambda i: (i, 0)),
            pl.BlockSpec((None, MOD_ROWS, D_MODEL), lambda i: (_cond_row(i, tm), 0, 0)),
        ],
        out_specs=pl.BlockSpec((tm, D_MODEL), lambda i: (i, 0)),
        out_shape=jax.ShapeDtypeStruct((TOK, D_MODEL), F32),
        scratch_shapes=[pltpu.VMEM((k, D_MODEL), BF16)],
        compiler_params=pltpu.CompilerParams(
            dimension_semantics=("arbitrary",), vmem_limit_bytes=VMEM_BIG),
        name="out_proj_resid",
    )(a, w, x, mod)


def _disc_kernel(lr_ref, li_ref, ls_ref, br_ref, bi_ref, abr_ref, abi_ref, bbr_ref, bbi_ref):
    lr, li = lr_ref[...], li_ref[...]
    dt = jnp.exp(ls_ref[...])
    mag = jnp.exp(lr * dt)
    abr = mag * jnp.cos(li * dt)
    abi = mag * jnp.sin(li * dt)
    den = lr * lr + li * li
    fr = ((abr - 1.0) * lr + abi * li) / den
    fi = (abi * lr - (abr - 1.0) * li) / den
    abr_ref[...] = abr
    abi_ref[...] = abi
    for c in range(S5_GROUP_CH):
        br, bi = br_ref[c], bi_ref[c]
        bbr_ref[c] = fr * br - fi * bi
        bbi_ref[c] = fr * bi + fi * br


def _disc_call(lam_re, lam_im, log_step, b_re, b_im):
    r = lam_re.shape[0]
    small = jax.ShapeDtypeStruct((r, S5_STATE), F32)
    big = jax.ShapeDtypeStruct((S5_GROUP_CH, r, S5_STATE), F32)
    return pl.pallas_call(_disc_kernel, out_shape=(small, small, big, big), name="s5_discretise")(
        lam_re, lam_im, log_step, b_re, b_im)


def _s5_kernel(*refs, ns, steps, aliased):
    uf_ref, ub_ref, wb_ref, wc_ref, a_ref, h0_ref = refs[:6]
    yf_ref, yb_ref, hfin_ref, utm_ref, ytm_ref, bu_ref, xs_ref, st_ref = refs[6 + (2 if aliased else 0):]
    c = pl.program_id(1)
    slabs_per_k = 2 * S5_QUARTERS
    u_refs = (uf_ref, ub_ref)
    y_refs = (yf_ref, yb_ref)

    def seq_rows(b):
        return pl.ds(b, steps, stride=ns)

    @pl.when(c == 0)
    def _():
        st_ref[...] = h0_ref[...]

    for d in range(2):
        for k in range(S5_KBLK):
            for b in range(ns):
                utm_ref[d, k, seq_rows(b), :] = u_refs[d][b, :, k * 128:(k + 1) * 128]

    for k in range(S5_KBLK):
        for d in range(2):
            res = jnp.dot(utm_ref[d, k].astype(BF16), wb_ref[d, k], preferred_element_type=F32)
            for s in range(slabs_per_k):
                bu_ref[k % 2, d, s] = res[:, s * S5_SLAB:(s + 1) * S5_SLAB]

        chains = [(d, q) for d in range(2) for q in range(S5_QUARTERS)]
        coef = {dq: (a_ref[dq[0], k, 0:ns, dq[1] * S5_SLAB:(dq[1] + 1) * S5_SLAB],
                     a_ref[dq[0], k, 0:ns, S5_BLK + dq[1] * S5_SLAB:S5_BLK + (dq[1] + 1) * S5_SLAB])
                for dq in chains}
        state = {dq: (st_ref[dq[0], k, 0:ns, dq[1] * S5_SLAB:(dq[1] + 1) * S5_SLAB],
                      st_ref[dq[0], k, 0:ns, S5_BLK + dq[1] * S5_SLAB:S5_BLK + (dq[1] + 1) * S5_SLAB])
                 for dq in chains}
        for i in range(steps):
            for d, q in chains:
                t = i if d == 0 else steps - 1 - i
                tile = slice(t * ns, (t + 1) * ns)
                (ar, ai), (xr, xi) = coef[d, q], state[d, q]
                xr, xi = (ar * xr - ai * xi + bu_ref[k % 2, d, q, tile, :],
                          ar * xi + ai * xr + bu_ref[k % 2, d, S5_QUARTERS + q, tile, :])
                xs_ref[k % 2, d, q, tile, :] = xr
                xs_ref[k % 2, d, S5_QUARTERS + q, tile, :] = xi
                state[d, q] = (xr, xi)
        for d, q in chains:
            st_ref[d, k, 0:ns, q * S5_SLAB:(q + 1) * S5_SLAB] = state[d, q][0]
            st_ref[d, k, 0:ns, S5_BLK + q * S5_SLAB:S5_BLK + (q + 1) * S5_SLAB] = state[d, q][1]

        for d in range(2):
            xk = jnp.concatenate([xs_ref[k % 2, d, s] for s in range(slabs_per_k)], axis=1).astype(BF16)
            ytm_ref[d, k] = jnp.dot(xk, wc_ref[d, k], preferred_element_type=F32)

    for d in range(2):
        for k in range(S5_KBLK):
            for b in range(ns):
                y_refs[d][b, :, k * 128:(k + 1) * 128] = ytm_ref[d, k, seq_rows(b), :]

    @pl.when(c == pl.num_programs(1) - 1)
    def _():
        hfin_ref[...] = st_ref[...]


def _s5_call(u, wb, wc, a_bar, h0, latent, y_prev=None):
    if latent:
        length, ns, blk0 = L_LAT, N_LAT, TOK_CTX // (N_LAT * L_LAT)
    else:
        length, ns, blk0 = L_CTX, SUBLANES, 0
    steps = S5_ROWS // ns
    ng, nc = h0.shape[0], length // steps
    ublk = (ns, steps, S5_WIDTH)
    stblk = (None, 2, S5_KBLK, SUBLANES, 2 * S5_BLK)
    fwd = lambda g, c: (blk0 + g, c, 0)
    bwd = lambda g, c: (blk0 + g, nc - 1 - c, 0)
    const4 = lambda g, c: (0, 0, 0, 0)
    in_specs = [
        pl.BlockSpec(ublk, fwd),
        pl.BlockSpec(ublk, bwd),
        pl.BlockSpec((2, S5_KBLK, 128, 2 * S5_BLK), const4),
        pl.BlockSpec((2, S5_KBLK, 2 * S5_BLK, 128), const4),
        pl.BlockSpec((2, S5_KBLK, SUBLANES, 2 * S5_BLK), const4),
        pl.BlockSpec(stblk, lambda g, c: (g, 0, 0, 0, 0)),
    ]
    u3 = u.reshape(TOK // length, length, S5_WIDTH)
    args = [u3, u3, wb, wc, a_bar, h0]
    aliases = {}
    if y_prev is not None:
        in_specs += [pl.BlockSpec(memory_space=pl.ANY)] * 2
        args += [y.reshape(TOK // length, length, S5_WIDTH) for y in y_prev]
        aliases = {6: 0, 7: 1}
    y_shape = jax.ShapeDtypeStruct((TOK // length, length, S5_WIDTH), F32)
    yf, yb, hfin = pl.pallas_call(
        functools.partial(_s5_kernel, ns=ns, steps=steps, aliased=y_prev is not None),
        grid=(ng, nc),
        in_specs=in_specs,
        out_specs=[
            pl.BlockSpec(ublk, fwd),
            pl.BlockSpec(ublk, bwd),
            pl.BlockSpec(stblk, lambda g, c: (g, 0, 0, 0, 0)),
        ],
        out_shape=(y_shape, y_shape, jax.ShapeDtypeStruct(h0.shape, F32)),
        scratch_shapes=[
            pltpu.VMEM((2, S5_KBLK, S5_ROWS, S5_SLAB), F32),
            pltpu.VMEM((2, S5_KBLK, S5_ROWS, S5_SLAB), F32),
            pltpu.VMEM((2, 2, 2 * S5_QUARTERS, S5_ROWS, S5_SLAB), F32),
            pltpu.VMEM((2, 2, 2 * S5_QUARTERS, S5_ROWS, S5_SLAB), F32),
            pltpu.VMEM((2, S5_KBLK, SUBLANES, 2 * S5_BLK), F32),
        ],
        input_output_aliases=aliases,
        compiler_params=pltpu.CompilerParams(
            dimension_semantics=("parallel", "arbitrary"), vmem_limit_bytes=VMEM_BIG),
        name="s5_scan_latent" if latent else "s5_scan_context",
    )(*args)
    return yf.reshape(TOK, S5_WIDTH), yb.reshape(TOK, S5_WIDTH), hfin


def _s5_weights(ab_re, ab_im, bb_re, bb_im, c_re, c_im):
    eye = jnp.eye(S5_GPB, dtype=F32)

    def in_side(bb):
        t = bb.reshape(S5_GROUP_CH, 2, S5_KBLK, S5_GPB, S5_STATE)
        t = jnp.einsum('cdkgp,gh->dkgchp', t, eye)
        return t.reshape(2, S5_KBLK, S5_GPB * S5_GROUP_CH, S5_BLK)

    def out_side(cc):
        t = cc.reshape(2, S5_KBLK, S5_GPB, S5_GROUP_CH, S5_STATE)
        t = jnp.einsum('dkgcp,gh->dkgphc', t, eye)
        return t.reshape(2, S5_KBLK, S5_BLK, S5_GPB * S5_GROUP_CH)

    wb = jnp.concatenate([in_side(bb_re), in_side(bb_im)], axis=-1).astype(BF16)
    wc = jnp.concatenate([out_side(c_re), -out_side(c_im)], axis=-2).astype(BF16)
    a = jnp.concatenate([ab_re.reshape(2, S5_KBLK, 1, S5_BLK), ab_im.reshape(2, S5_KBLK, 1, S5_BLK)], axis=-1)
    a = jnp.broadcast_to(a, (2, S5_KBLK, SUBLANES, 2 * S5_BLK))
    return wb, wc, a


def _s5_state_in(s_re, s_im):
    def blk(s):
        return s.reshape(-1, 2, S5_KBLK, S5_BLK).transpose(1, 2, 0, 3)
    h = jnp.concatenate([blk(s_re), blk(s_im)], axis=-1)
    h = jnp.pad(h, ((0, 0), (0, 0), (0, SUBLANES - h.shape[2]), (0, 0)))
    return h[None]


def _s5_state_out(h):
    def unblk(s):
        return s.transpose(0, 3, 1, 2, 4).reshape(-1, 2, S5_GROUPS, S5_STATE)
    return unblk(h[..., :S5_BLK]), unblk(h[..., S5_BLK:])


def _gelu_tanh(x):
    return x * (0.5 * (1.0 + jnp.tanh(math.sqrt(2.0 / math.pi) * (x + 0.044715 * (x * x * x)))))


def _mix_kernel(yf_ref, yb_ref, u_ref, x_ref, xp_ref, xn_ref, mod_ref, nw_ref, wbg_ref, wcg_ref, wv_ref,
                d_ref, gw_ref, gb_ref, cw_ref, cb_ref, ow_ref, o_ref, wbcv_ref, gwb_ref, owb_ref):
    tm = TM_MIX
    ext = tm + 2 * HALO

    @pl.when(pl.program_id(0) == 0)
    def _():
        wbcv_ref[:, 0:CONV_W] = wbg_ref[...].astype(BF16)
        wbcv_ref[:, CONV_W:2 * CONV_W] = wcg_ref[...].astype(BF16)
        wbcv_ref[:, 2 * CONV_W:3 * CONV_W] = wv_ref[...].astype(BF16)
        gwb_ref[...] = gw_ref[...].astype(BF16)
        owb_ref[...] = ow_ref[...].astype(BF16)

    y = yf_ref[...] + yb_ref[...] + d_ref[...] * u_ref[...]
    z = _gelu_tanh(y)
    gate = jax.nn.sigmoid(jnp.dot(z.astype(BF16), gwb_ref[...], preferred_element_type=F32) + gb_ref[...])
    a_out = z * gate

    x = x_ref[...]
    xe = jnp.concatenate([xp_ref[...], x, xn_ref[...]], axis=0)
    h = _norm_mod(xe, nw_ref[...], mod_ref[1:2, :], mod_ref[0:1, :]).astype(BF16)
    bcv = jnp.dot(h, wbcv_ref[...], preferred_element_type=F32)
    bg = bcv[HALO:HALO + tm, 0:CONV_W]
    pe = bcv[:, CONV_W:2 * CONV_W] * bcv[:, 2 * CONV_W:3 * CONV_W]
    p = pe[HALO:HALO + tm, :]
    local = lax.broadcasted_iota(jnp.int32, (tm, 1), 0)
    row = pl.program_id(0) * tm + local
    seq_len = jnp.where(row < TOK_CTX, L_CTX, L_LAT)
    pos = jnp.bitwise_and(row, seq_len - 1)
    p_prev = jnp.where(pos == 0, 0.0, pltpu.roll(pe, 1, 0)[HALO:HALO + tm, :])
    p_next = jnp.where(pos == seq_len - 1, 0.0, pltpu.roll(pe, ext - 1, 0)[HALO:HALO + tm, :])
    conv = cw_ref[0:1, :] * p_prev + cw_ref[1:2, :] * p + cw_ref[2:3, :] * p_next + cb_ref[...]
    b_out = bg * conv

    out = (jnp.dot(a_out.astype(BF16), owb_ref[0:S5_WIDTH, :], preferred_element_type=F32)
           + jnp.dot(b_out.astype(BF16), owb_ref[S5_WIDTH:, :], preferred_element_type=F32))
    o_ref[...] = x + mod_ref[2:3, :] * out


def _mix_call(yf, yb, u, x, mod, nw, in_w, s5_d, glu_w, glu_b, conv_w, conv_b, out_w, layer):
    tm = TM_MIX
    hb = tm // HALO
    last_halo = TOK // HALO - 1
    tok_blk = lambda width: pl.BlockSpec((tm, width), lambda i: (i, 0))
    row1 = lambda width: pl.BlockSpec((1, width), lambda i: (0, 0))
    once = dict(pipeline_mode=pl.Buffered(1))
    w_col = lambda col: pl.BlockSpec((None, D_MODEL, CONV_W), lambda i: (layer, 0, col), **once)
    return pl.pallas_call(
        _mix_kernel,
        grid=(TOK // tm,),
        in_specs=[
            tok_blk(S5_WIDTH), tok_blk(S5_WIDTH), tok_blk(S5_WIDTH), tok_blk(D_MODEL),
            pl.BlockSpec((HALO, D_MODEL), lambda i: (jnp.maximum(i * hb - 1, 0), 0)),
            pl.BlockSpec((HALO, D_MODEL), lambda i: (jnp.minimum((i + 1) * hb, last_halo), 0)),
            pl.BlockSpec((None, MOD_ROWS, D_MODEL), lambda i: (_cond_row(i, tm), 0, 0)),
            row1(D_MODEL),
            w_col(1), w_col(2), w_col(3),
            row1(CONV_W),
            pl.BlockSpec((None, S5_WIDTH, S5_WIDTH), lambda i: (layer, 0, 0), **once),
            row1(CONV_W),
            pl.BlockSpec((MOD_ROWS, CONV_W), lambda i: (0, 0)),
            row1(CONV_W),
            pl.BlockSpec((None, D_MODEL, D_MODEL), lambda i: (layer, 0, 0), **once),
        ],
        out_specs=pl.BlockSpec((tm, D_MODEL), lambda i: (i, 0)),
        out_shape=jax.ShapeDtypeStruct((TOK, D_MODEL), F32),
        scratch_shapes=[pltpu.VMEM((D_MODEL, 3 * CONV_W), BF16), pltpu.VMEM((S5_WIDTH, S5_WIDTH), BF16),
                        pltpu.VMEM((D_MODEL, D_MODEL), BF16)],
        compiler_params=pltpu.CompilerParams(
            dimension_semantics=("arbitrary",), vmem_limit_bytes=VMEM_BIG),
        name="glu_conv_out",
    )(yf, yb, u, x, x, x, mod, nw.reshape(1, D_MODEL), in_w, in_w, in_w,
      s5_d.reshape(1, -1), glu_w, glu_b.reshape(1, -1),
      jnp.pad(conv_w, ((0, MOD_ROWS - conv_w.shape[0]), (0, 0))), conv_b.reshape(1, -1), out_w)


def _log_sigmoid(x):
    return jnp.minimum(x, 0.0) - jnp.log1p(jnp.exp(-jnp.abs(x)))


def _ret_kernel(*refs, latent, aliased):
    q_ref, k_ref, v_ref, g_ref, gam_ref, gn_ref = refs[:6]
    n_in = 6
    if latent:
        cos_ref, sin_ref, s0_ref = refs[6:9]
        n_in = 9
    n_in += aliased
    o_ref, sfin_ref, qs_ref, sc_ref, vb_ref, kv_ref, sk_ref = refs[n_in:]
    seq_chunks = RET_NC if latent else L_CTX // RET_C

    lg_f = _log_sigmoid(gam_ref[0, 0:1, :])
    lg_b = _log_sigmoid(gam_ref[1, 0:1, :])
    ii = lax.broadcasted_iota(jnp.int32, (RET_C, RET_C), 0)
    jj = lax.broadcasted_iota(jnp.int32, (RET_C, RET_C), 1)
    diff = (ii - jj).astype(F32)
    lgf_k, lgb_k = lg_f[:, :RET_DK], lg_b[:, :RET_DK]
    intra = (jnp.where(diff >= 0, jnp.exp(lgf_k * jnp.maximum(diff, 0.0)), 0.0)
             + jnp.where(diff <= 0, jnp.exp(lgb_k * jnp.maximum(-diff, 0.0)), 0.0))
    tk = lax.broadcasted_iota(jnp.int32, (RET_C, RET_DK), 0).astype(F32)
    tv = lax.broadcasted_iota(jnp.int32, (RET_C, RET_DV), 0).astype(F32)
    kdec_f = jnp.exp(lgf_k * (RET_C - 1.0 - tk))
    kdec_b = jnp.exp(lgb_k * tk)
    qdec_f = jnp.exp(lg_f * (tv + 1.0))
    qdec_b = jnp.exp(lg_b * (RET_C - tv))
    cd_f = jnp.exp(lg_f * RET_C)
    cd_b = jnp.exp(lg_b * RET_C)

    lane = lax.broadcasted_iota(jnp.int32, (RET_C, RET_DK), 1)
    first_half = jnp.bitwise_and(lane, 2 * (RET_DK // 4) - 1) < RET_DK // 4

    def rotate(x, cos, sin):
        swapped = jnp.where(first_half, pltpu.roll(x, RET_DK - RET_DK // 4, 1), pltpu.roll(x, RET_DK // 4, 1))
        return x * cos + swapped * sin

    tn_dims = (((0,), (0,)), ((), ()))
    nt_dims = (((1,), (1,)), ((), ()))

    def prep(c, _):
        r = pl.ds(pl.multiple_of(c * RET_C, RET_C), RET_C)
        q, k = q_ref[r, :], k_ref[r, :]
        if latent:
            cos, sin = cos_ref[r, :], sin_ref[r, :]
            q, k = rotate(q, cos, sin), rotate(k, cos, sin)
        qs = (q * (RET_DK ** -0.5)).astype(BF16)
        qs_ref[r, :] = qs
        scores = lax.dot_general(qs, k.astype(BF16), nt_dims, preferred_element_type=F32) * intra
        sc_ref[c] = scores.astype(BF16)
        kd = jnp.concatenate([(k * kdec_f).astype(BF16), (k * kdec_b).astype(BF16)], axis=1)
        vb = v_ref[r, :].astype(BF16)
        vb_ref[r, :] = vb
        kv_ref[c] = lax.dot_general(kd, vb, tn_dims, preferred_element_type=F32)
        return 0

    lax.fori_loop(0, RET_NC, prep, 0, unroll=RET_UNROLL)

    zeros = jnp.zeros((RET_DK, RET_DV), F32)
    s = s0_ref[0] if latent else zeros
    for c in range(RET_NC):
        if c % seq_chunks == 0 and not (latent and c == 0):
            s = zeros
        sk_ref[c, :, 0:RET_DV] = s.astype(BF16)
        s = cd_f * s + kv_ref[c, 0:RET_DK, :]
        if (c + 1) % seq_chunks == 0:
            sfin_ref[c // seq_chunks, 0] = s
    s = s0_ref[1] if latent else zeros
    for c in reversed(range(RET_NC)):
        if (c + 1) % seq_chunks == 0 and not (latent and c == RET_NC - 1):
            s = zeros
        sk_ref[c, :, RET_DV:2 * RET_DV] = s.astype(BF16)
        s = cd_b * s + kv_ref[c, RET_DK:2 * RET_DK, :]
        if c % seq_chunks == 0:
            sfin_ref[c // seq_chunks, 1] = s

    def emit(c, _):
        r = pl.ds(pl.multiple_of(c * RET_C, RET_C), RET_C)
        cross = jnp.dot(qs_ref[r, :], sk_ref[c], preferred_element_type=F32)
        o = (jnp.dot(sc_ref[c], vb_ref[r, :], preferred_element_type=F32)
             + cross[:, 0:RET_DV] * qdec_f + cross[:, RET_DV:2 * RET_DV] * qdec_b)
        o = o * lax.rsqrt(jnp.mean(o * o, axis=-1, keepdims=True) + EPS) * gn_ref[...]
        o_ref[r, :] = (_silu(g_ref[r, :]) * o).astype(o_ref.dtype)
        return 0

    lax.fori_loop(0, RET_NC, emit, 0, unroll=RET_UNROLL)


def _ret_call(proj, gam, gn_w, latent, layer, rope=None, s0=None, o_prev=None, s_prev=None):
    nblk = TOK_CTX // RET_ROWS
    blk0 = nblk if latent else 0
    seqs = 1 if latent else RET_ROWS // L_CTX
    kcol, vcol, gcol = RET_QK // RET_DK, 2 * RET_QK // RET_DV, (2 * RET_QK + RET_V) // RET_DV
    in_specs = [
        pl.BlockSpec((RET_ROWS, RET_DK), lambda b, h: (blk0 + b, h)),
        pl.BlockSpec((RET_ROWS, RET_DK), lambda b, h: (blk0 + b, kcol + h)),
        pl.BlockSpec((RET_ROWS, RET_DV), lambda b, h: (blk0 + b, vcol + h)),
        pl.BlockSpec((RET_ROWS, RET_DV), lambda b, h: (blk0 + b, gcol + h)),
        pl.BlockSpec((None, 2, SUBLANES, RET_DV), lambda b, h: (h, 0, 0, 0)),
        pl.BlockSpec((1, RET_DV), lambda b, h: (0, h)),
    ]
    args = [proj, proj, proj, proj, gam, gn_w.reshape(1, RET_V)]
    aliases = {}
    if latent:
        in_specs += [
            pl.BlockSpec((RET_ROWS, RET_DK), lambda b, h: (0, 0)),
            pl.BlockSpec((RET_ROWS, RET_DK), lambda b, h: (0, 0)),
            pl.BlockSpec((None, 2, None, RET_DK, RET_DV), lambda b, h: (b, 0, h, 0, 0)),
            pl.BlockSpec(memory_space=pl.ANY),
        ]
        args += [rope[0], rope[1], s0, o_prev]
        aliases = {len(args) - 1: 0}
        s_shape = jax.ShapeDtypeStruct((N_LAT, 1, 2, RET_H, RET_DK, RET_DV), F32)
        s_layer = 0
    else:
        s_shape = jax.ShapeDtypeStruct((N_CTX, DEPTH // 2, 2, RET_H, RET_DK, RET_DV), F32)
        s_layer = layer
        if s_prev is not None:
            in_specs += [pl.BlockSpec(memory_space=pl.ANY)]
            args += [s_prev]
            aliases = {len(args) - 1: 1}
    return pl.pallas_call(
        functools.partial(_ret_kernel, latent=latent, aliased=len(aliases)),
        grid=(nblk, RET_H),
        in_specs=in_specs,
        out_specs=[
            pl.BlockSpec((RET_ROWS, RET_DV), lambda b, h: (blk0 + b, h)),
            pl.BlockSpec((seqs, None, 2, None, RET_DK, RET_DV), lambda b, h: (b, s_layer, 0, h, 0, 0)),
        ],
        out_shape=(jax.ShapeDtypeStruct((TOK, RET_V), BF16), s_shape),
        scratch_shapes=[
            pltpu.VMEM((RET_ROWS, RET_DK), BF16),
            pltpu.VMEM((RET_NC, RET_C, RET_C), BF16),
            pltpu.VMEM((RET_ROWS, RET_DV), BF16),
            pltpu.VMEM((RET_NC, 2 * RET_DK, RET_DV), F32),
            pltpu.VMEM((RET_NC, RET_DK, 2 * RET_DV), BF16),
        ],
        input_output_aliases=aliases,
        compiler_params=pltpu.CompilerParams(
            dimension_semantics=("parallel", "arbitrary"), vmem_limit_bytes=VMEM_MID),
        name="retention_latent" if latent else "retention_context",
    )(*args)


def _rope_tables():
    n_freq = RET_DK // 4
    t = jnp.arange(L_LAT)
    row = (t // GRID_W).astype(F32)
    col = (t % GRID_W).astype(F32)
    inv_freq = jnp.power(ROPE_BASE, -jnp.arange(n_freq, dtype=F32) / n_freq)
    ar, ac = row[:, None] * inv_freq, col[:, None] * inv_freq
    cos = jnp.concatenate([jnp.cos(ar), jnp.cos(ar), jnp.cos(ac), jnp.cos(ac)], axis=-1)
    sin = jnp.concatenate([-jnp.sin(ar), jnp.sin(ar), -jnp.sin(ac), jnp.sin(ac)], axis=-1)
    return cos, sin


def kernel(x_prompt, x_sample, state_s5_re, state_s5_im, state_ret, c, c_ctx, norm1_w, norm2_w, ada_w, ada_b,
           hy_in_w, hy_out_w, s5_lam_re, s5_lam_im, s5_log_step, s5_b_re, s5_b_im, s5_c_re, s5_c_im,
           s5_d, s5_glu_w, s5_glu_b, conv_w, conv_b, ret_in_w, ret_out_w, ret_gamma_logit, ret_gn_w,
           mlp_w1, mlp_w2, final_norm_w):
    x = (x_prompt.reshape(TOK_CTX, D_MODEL), x_sample.reshape(TOK - TOK_CTX, D_MODEL))

    cond = jnp.concatenate([c_ctx[None, :], c], axis=0)
    cond = jnp.pad(cond, ((0, MOD_ROWS - N_COND), (0, 0)))
    mod_all = _ada_call(cond, ada_w, ada_b)
    mod_all = mod_all[:, :N_COND].reshape(DEPTH, N_COND, N_MOD, D_MODEL)
    mod_all = jnp.pad(mod_all, ((0, 0), (0, 0), (0, MOD_ROWS - N_MOD), (0, 0)))

    n_s5 = s5_lam_re.shape[0]
    rows = n_s5 * 2 * S5_GROUPS
    ab_re, ab_im, bb_re, bb_im = _disc_call(
        s5_lam_re.reshape(rows, S5_STATE), s5_lam_im.reshape(rows, S5_STATE),
        jnp.broadcast_to(s5_log_step.reshape(rows, 1), (rows, S5_STATE)),
        jnp.moveaxis(s5_b_re, -1, 0).reshape(S5_GROUP_CH, rows, S5_STATE),
        jnp.moveaxis(s5_b_im, -1, 0).reshape(S5_GROUP_CH, rows, S5_STATE))
    ab_re = ab_re.reshape(n_s5, 2, S5_GROUPS, S5_STATE)
    ab_im = ab_im.reshape(n_s5, 2, S5_GROUPS, S5_STATE)
    bb_re = bb_re.reshape(S5_GROUP_CH, n_s5, 2, S5_GROUPS, S5_STATE)
    bb_im = bb_im.reshape(S5_GROUP_CH, n_s5, 2, S5_GROUPS, S5_STATE)

    rope = _rope_tables()
    zero_state = jnp.zeros((N_CTX // SUBLANES, 2, S5_KBLK, SUBLANES, 2 * S5_BLK), F32)

    new_re, new_im, new_ret = [], [], None
    y_prompt = y_sample = None
    for i in range(DEPTH):
        j = i // 2
        mod = mod_all[i]
        if i % 2 == 0:
            u, x = _uproj_call(x, mod, norm1_w[i], hy_in_w, j)
            wb, wc, a_bar = _s5_weights(ab_re[j], ab_im[j], bb_re[:, j], bb_im[:, j], s5_c_re[j], s5_c_im[j])
            yf, yb, h_ctx = _s5_call(u, wb, wc, a_bar, zero_state, latent=False)
            yf, yb, _ = _s5_call(u, wb, wc, a_bar, _s5_state_in(state_s5_re[:, j], state_s5_im[:, j]),
                                 latent=True, y_prev=(yf, yb))
            x = _mix_call(yf, yb, u, x, mod, norm1_w[i], hy_in_w, s5_d[j], s5_glu_w, s5_glu_b[j],
                          conv_w[j], conv_b[j], hy_out_w, j)
            s_re, s_im = _s5_state_out(h_ctx)
            new_re.append(s_re)
            new_im.append(s_im)
        else:
            proj = _in_call(x, mod, norm1_w[i], ret_in_w, j)
            gam = jnp.broadcast_to(ret_gamma_logit[j].T[:, :, None, None], (RET_H, 2, SUBLANES, RET_DV))
            o, new_ret = _ret_call(proj, gam, ret_gn_w[j], latent=False, layer=j, s_prev=new_ret)
            o, _ = _ret_call(proj, gam, ret_gn_w[j], latent=True, layer=j, rope=rope, s0=state_ret[:, j], o_prev=o)
            x = _resid_call(o, ret_out_w, x, mod, j)
        if i < DEPTH - 1:
            x = _mlp_call(x, mod, norm2_w[i], mlp_w1, mlp_w2, i)
        else:
            y_prompt, y_sample = _mlp_call(x, mod, norm2_w[i], mlp_w1, mlp_w2, i, fw=final_norm_w)

    return (y_prompt.reshape(N_CTX, L_CTX, D_MODEL), y_sample.reshape(N_LAT, L_LAT, D_MODEL),
            jnp.stack(new_re, 1), jnp.stack(new_im, 1), new_ret)
```

```python
import functools
import math

import jax
import jax.numpy as jnp
from jax import lax
from jax.experimental import pallas as pl
from jax.experimental.pallas import tpu as pltpu

F32 = jnp.float32
BF16 = jnp.bfloat16

D_MODEL = 1024
DEPTH = 4
N_CTX, L_CTX = 32, 256
N_LAT, L_LAT = 4, 2048
TOK_CTX = N_CTX * L_CTX
TOK = TOK_CTX + N_LAT * L_LAT
N_COND = 1 + N_LAT
GRID_W = 64
EPS = 1e-6

S5_WIDTH = 512
S5_GROUP_CH = 16
S5_GROUPS = 32
S5_STATE = 64
S5_KBLK = 4
S5_GPB = S5_GROUPS // S5_KBLK
S5_BLK = S5_GPB * S5_STATE
S5_SLAB = 128
S5_QUARTERS = S5_BLK // S5_SLAB
S5_ROWS = 512
SUBLANES = 8
CONV_W = 512

RET_H = 8
RET_DK = 128
RET_DV = 256
RET_QK = RET_H * RET_DK
RET_V = RET_H * RET_DV
RET_IN = 2 * RET_QK + 2 * RET_V
RET_C = 128
RET_ROWS = 2048
RET_NC = RET_ROWS // RET_C
RET_UNROLL = 8
ROPE_BASE = 10000.0
MLP_H = 4 * D_MODEL
N_MOD = 6
MOD_ROWS = 8

VMEM_BIG = 56 * 1024 * 1024
VMEM_MID = 40 * 1024 * 1024

TM_DENSE = 1024
TN_DENSE = 1024
TM_PROJ = 512
TM_MLP = 512
MLP_CHUNKS = 8
RET_IN_CHUNKS = RET_IN // TN_DENSE
RET_QK_CHUNKS = 2 * RET_QK // TN_DENSE
RET_V_CHUNKS = RET_V // TN_DENSE
TM_MIX = 512
HALO = 8


def _cond_row(i, tm):
    row0 = i * tm
    return jnp.where(row0 < TOK_CTX, 0, 1 + (row0 - TOK_CTX) // L_LAT)


def _norm_mod(x, nw, scale, shift):
    y = x * lax.rsqrt(jnp.mean(x * x, axis=-1, keepdims=True) + EPS)
    return (y * nw) * (1.0 + scale) + shift


def _silu(x):
    return x * jax.nn.sigmoid(x)


def _ada_kernel(c_ref, w_ref, b_ref, o_ref):
    sc = _silu(c_ref[...]).astype(BF16)
    o_ref[0] = jnp.dot(sc, w_ref[0].astype(BF16), preferred_element_type=F32) + b_ref[0]


def _ada_call(cond, ada_w, ada_b):
    tn = 1536
    return pl.pallas_call(
        _ada_kernel,
        grid=(DEPTH, N_MOD * D_MODEL // tn),
        in_specs=[
            pl.BlockSpec((MOD_ROWS, D_MODEL), lambda l, j: (0, 0)),
            pl.BlockSpec((1, D_MODEL, tn), lambda l, j: (l, 0, j)),
            pl.BlockSpec((1, 1, tn), lambda l, j: (l, 0, j)),
        ],
        out_specs=pl.BlockSpec((1, MOD_ROWS, tn), lambda l, j: (l, 0, j)),
        out_shape=jax.ShapeDtypeStruct((DEPTH, MOD_ROWS, N_MOD * D_MODEL), F32),
        compiler_params=pltpu.CompilerParams(
            dimension_semantics=("arbitrary", "arbitrary"), vmem_limit_bytes=VMEM_MID),
        name="ada_mod",
    )(cond, ada_w, ada_b.reshape(DEPTH, 1, N_MOD * D_MODEL))


def _pair_halves(w):
    qd = RET_DK // 4
    cols = []
    for hd in range(w.shape[1] // RET_DK):
        blk = w[:, hd * RET_DK:(hd + 1) * RET_DK]
        lane = lax.broadcasted_iota(jnp.int32, blk.shape, 1)
        blk = jnp.where(jnp.logical_and(lane >= qd, lane < 2 * qd), pltpu.roll(blk, RET_DK - qd, 1),
                        jnp.where(jnp.logical_and(lane >= 2 * qd, lane < 3 * qd), pltpu.roll(blk, qd, 1), blk))
        cols.append(blk)
    return jnp.concatenate(cols, axis=1)


def _in_kernel(x_ref, mod_ref, nw_ref, w_ref, oa_ref, ov_ref, wb_ref):
    cw = TN_DENSE
    s = pl.program_id(0)

    @pl.when(s < RET_QK_CHUNKS)
    def _():
        wb_ref[s] = _pair_halves(w_ref[...]).astype(BF16)

    @pl.when(jnp.logical_and(s >= RET_QK_CHUNKS, s < RET_IN_CHUNKS))
    def _():
        wb_ref[s] = w_ref[...].astype(BF16)

    @pl.when(s >= RET_IN_CHUNKS)
    def _():
        h = _norm_mod(x_ref[...], nw_ref[...], mod_ref[1:2, :], mod_ref[0:1, :]).astype(BF16)
        for c in range(RET_IN_CHUNKS):
            res = jnp.dot(h, wb_ref[c], preferred_element_type=F32)
            if RET_QK_CHUNKS <= c < RET_QK_CHUNKS + RET_V_CHUNKS:
                cv = c - RET_QK_CHUNKS
                ov_ref[:, cv * cw:(cv + 1) * cw] = res.astype(BF16)
            else:
                ca = c if c < RET_QK_CHUNKS else c - RET_V_CHUNKS
                oa_ref[:, ca * cw:(ca + 1) * cw] = res


def _in_call(x, mod, nw, w, layer):
    tm, cw = TM_PROJ, TN_DENSE
    tile = lambda s: jnp.maximum(s - RET_IN_CHUNKS, 0)
    return pl.pallas_call(
        _in_kernel,
        grid=(RET_IN_CHUNKS + TOK // tm,),
        in_specs=[
            pl.BlockSpec((tm, D_MODEL), lambda s: (tile(s), 0)),
            pl.BlockSpec((None, MOD_ROWS, D_MODEL), lambda s: (_cond_row(tile(s), tm), 0, 0)),
            pl.BlockSpec((1, D_MODEL), lambda s: (0, 0)),
            pl.BlockSpec((None, D_MODEL, cw), lambda s: (layer, 0, jnp.minimum(s, RET_IN_CHUNKS - 1))),
        ],
        out_specs=[pl.BlockSpec((tm, RET_IN - RET_V), lambda s: (tile(s), 0)),
                   pl.BlockSpec((tm, RET_V), lambda s: (tile(s), 0))],
        out_shape=(jax.ShapeDtypeStruct((TOK, RET_IN - RET_V), F32), jax.ShapeDtypeStruct((TOK, RET_V), BF16)),
        scratch_shapes=[pltpu.VMEM((RET_IN_CHUNKS, D_MODEL, cw), BF16)],
        compiler_params=pltpu.CompilerParams(
            dimension_semantics=("arbitrary",), vmem_limit_bytes=VMEM_BIG),
        name="norm_proj",
    )(x, mod, nw.reshape(1, D_MODEL), w)


def _uproj_kernel(*refs, paired, n_ctx_tiles):
    if paired:
        xc_ref, xl_ref, mod_ref, nw_ref, w_ref, u_ref, xcat_ref, wb_ref = refs
        x = jnp.where(pl.program_id(0) < n_ctx_tiles, xc_ref[...], xl_ref[...])
        xcat_ref[...] = x
    else:
        x_ref, mod_ref, nw_ref, w_ref, u_ref, wb_ref = refs
        x = x_ref[...]

    @pl.when(pl.program_id(0) == 0)
    def _():
        wb_ref[...] = w_ref[...].astype(BF16)

    h = _norm_mod(x, nw_ref[...], mod_ref[1:2, :], mod_ref[0:1, :]).astype(BF16)
    u_ref[...] = jnp.dot(h, wb_ref[...], preferred_element_type=F32)


def _uproj_call(x, mod, nw, w, layer):
    tm = TM_DENSE
    paired = isinstance(x, tuple)
    n_ctx_tiles = TOK_CTX // tm
    if paired:
        x_specs = [pl.BlockSpec((tm, D_MODEL), lambda i: (jnp.minimum(i, n_ctx_tiles - 1), 0)),
                   pl.BlockSpec((tm, D_MODEL), lambda i: (jnp.maximum(i - n_ctx_tiles, 0), 0))]
        xs = list(x)
    else:
        x_specs = [pl.BlockSpec((tm, D_MODEL), lambda i: (i, 0))]
        xs = [x]
    tok_out = lambda width: pl.BlockSpec((tm, width), lambda i: (i, 0))
    out_specs = [tok_out(S5_WIDTH)] + ([tok_out(D_MODEL)] if paired else [])
    out_shape = [jax.ShapeDtypeStruct((TOK, S5_WIDTH), F32)] + (
        [jax.ShapeDtypeStruct((TOK, D_MODEL), F32)] if paired else [])
    res = pl.pallas_call(
        functools.partial(_uproj_kernel, paired=paired, n_ctx_tiles=n_ctx_tiles),
        grid=(TOK // tm,),
        in_specs=x_specs + [
            pl.BlockSpec((None, MOD_ROWS, D_MODEL), lambda i: (_cond_row(i, tm), 0, 0)),
            pl.BlockSpec((1, D_MODEL), lambda i: (0, 0)),
            pl.BlockSpec((None, D_MODEL, S5_WIDTH), lambda i: (layer, 0, 0)),
        ],
        out_specs=out_specs,
        out_shape=out_shape,
        scratch_shapes=[pltpu.VMEM((D_MODEL, S5_WIDTH), BF16)],
        compiler_params=pltpu.CompilerParams(
            dimension_semantics=("arbitrary",), vmem_limit_bytes=VMEM_MID),
        name="norm_uproj",
    )(*xs, mod, nw.reshape(1, D_MODEL), w)
    return (res[0], res[1]) if paired else (res[0], x)


def _mlp_kernel(*refs, final, n_ctx_tiles):
    if final:
        x_ref, mod_ref, nw_ref, w1_ref, w2_ref, fw_ref, oc_ref, ol_ref, w1b_ref, w2b_ref = refs
    else:
        x_ref, mod_ref, nw_ref, w1_ref, w2_ref, o_ref, w1b_ref, w2b_ref = refs
    s = pl.program_id(0)

    @pl.when(s < MLP_CHUNKS)
    def _():
        w1b_ref[s] = w1_ref[...].astype(BF16)
        w2b_ref[s] = w2_ref[...].astype(BF16)

    def result():
        x = x_ref[...]
        h = _norm_mod(x, nw_ref[...], mod_ref[4:5, :], mod_ref[3:4, :]).astype(BF16)
        acc = None
        for c in range(MLP_CHUNKS):
            a = jnp.maximum(jnp.dot(h, w1b_ref[c], preferred_element_type=F32), 0.0)
            t = jnp.dot((a * a).astype(BF16), w2b_ref[c], preferred_element_type=F32)
            acc = t if acc is None else acc + t
        y = x + mod_ref[5:6, :] * acc
        if final:
            y = y * lax.rsqrt(jnp.mean(y * y, axis=-1, keepdims=True) + EPS) * fw_ref[...]
        return y

    tile = s - MLP_CHUNKS
    if final:
        @pl.when(jnp.logical_and(tile >= 0, tile < n_ctx_tiles))
        def _():
            oc_ref[...] = result()

        @pl.when(tile >= n_ctx_tiles)
        def _():
            ol_ref[...] = result()
    else:
        @pl.when(tile >= 0)
        def _():
            o_ref[...] = result()


def _mlp_call(x, mod, nw, w1, w2, layer, fw=None):
    tm = TM_MLP
    final = fw is not None
    n_tiles, n_ctx_tiles = TOK // tm, TOK_CTX // tm
    ch = MLP_H // MLP_CHUNKS
    tile = lambda s: jnp.maximum(s - MLP_CHUNKS, 0)
    chunk = lambda s: jnp.minimum(s, MLP_CHUNKS - 1)
    in_specs = [
        pl.BlockSpec((tm, D_MODEL), lambda s: (tile(s), 0)),
        pl.BlockSpec((None, MOD_ROWS, D_MODEL), lambda s: (_cond_row(tile(s), tm), 0, 0)),
        pl.BlockSpec((1, D_MODEL), lambda s: (0, 0)),
        pl.BlockSpec((None, D_MODEL, ch), lambda s: (layer, 0, chunk(s))),
        pl.BlockSpec((None, ch, D_MODEL), lambda s: (layer, chunk(s), 0)),
    ]
    args = [x, mod, nw.reshape(1, D_MODEL), w1, w2]
    if final:
        in_specs.append(pl.BlockSpec((1, D_MODEL), lambda s: (0, 0)))
        args.append(fw.reshape(1, D_MODEL))
        out_specs = [
            pl.BlockSpec((tm, D_MODEL), lambda s: (jnp.clip(s - MLP_CHUNKS, 0, n_ctx_tiles - 1), 0)),
            pl.BlockSpec((tm, D_MODEL), lambda s: (jnp.maximum(s - MLP_CHUNKS - n_ctx_tiles, 0), 0)),
        ]
        out_shape = (jax.ShapeDtypeStruct((TOK_CTX, D_MODEL), F32),
                     jax.ShapeDtypeStruct((TOK - TOK_CTX, D_MODEL), F32))
    else:
        out_specs = pl.BlockSpec((tm, D_MODEL), lambda s: (tile(s), 0))
        out_shape = jax.ShapeDtypeStruct((TOK, D_MODEL), F32)
    return pl.pallas_call(
        functools.partial(_mlp_kernel, final=final, n_ctx_tiles=n_ctx_tiles),
        grid=(MLP_CHUNKS + n_tiles,),
        in_specs=in_specs,
        out_specs=out_specs,
        out_shape=out_shape,
        scratch_shapes=[pltpu.VMEM((MLP_CHUNKS, D_MODEL, ch), BF16), pltpu.VMEM((MLP_CHUNKS, ch, D_MODEL), BF16)],
        compiler_params=pltpu.CompilerParams(
            dimension_semantics=("arbitrary",), vmem_limit_bytes=VMEM_BIG),
        name="mlp_final" if final else "mlp",
    )(*args)


def _resid_kernel(a_ref, w_ref, x_ref, mod_ref, o_ref, wb_ref):
    @pl.when(pl.program_id(0) == 0)
    def _():
        wb_ref[...] = w_ref[...].astype(BF16)

    y = jnp.dot(a_ref[...], wb_ref[...], preferred_element_type=F32)
    o_ref[...] = x_ref[...] + mod_ref[2:3, :] * y


def _resid_call(a, w, x, mod, layer):
    tm = TM_DENSE
    k = a.shape[1]
    return pl.pallas_call(
        _resid_kernel,
        grid=(TOK // tm,),
        in_specs=[
            pl.BlockSpec((tm, k), lambda i: (i, 0)),
            pl.BlockSpec((None, k, D_MODEL), lambda i: (layer, 0, 0)),
            pl.BlockSpec((tm, D_MODEL), lambda i: (i, 0)),
            pl.BlockSpec((None, MOD_ROWS, D_MODEL), lambda i: (_cond_row(i, tm), 0, 0)),
        ],
        out_specs=pl.BlockSpec((tm, D_MODEL), lambda i: (i, 0)),
        out_shape=jax.ShapeDtypeStruct((TOK, D_MODEL), F32),
        scratch_shapes=[pltpu.VMEM((k, D_MODEL), BF16)],
        compiler_params=pltpu.CompilerParams(
            dimension_semantics=("arbitrary",), vmem_limit_bytes=VMEM_BIG),
        name="out_proj_resid",
    )(a, w, x, mod)


def _disc_kernel(lr_ref, li_ref, ls_ref, br_ref, bi_ref, abr_ref, abi_ref, bbr_ref, bbi_ref):
    lr, li = lr_ref[...], li_ref[...]
    dt = jnp.exp(ls_ref[...])
    mag = jnp.exp(lr * dt)
    abr = mag * jnp.cos(li * dt)
    abi = mag * jnp.sin(li * dt)
    den = lr * lr + li * li
    fr = ((abr - 1.0) * lr + abi * li) / den
    fi = (abi * lr - (abr - 1.0) * li) / den
    abr_ref[...] = abr
    abi_ref[...] = abi
    for c in range(S5_GROUP_CH):
        br, bi = br_ref[c], bi_ref[c]
        bbr_ref[c] = fr * br - fi * bi
        bbi_ref[c] = fr * bi + fi * br


def _disc_call(lam_re, lam_im, log_step, b_re, b_im):
    r = lam_re.shape[0]
    small = jax.ShapeDtypeStruct((r, S5_STATE), F32)
    big = jax.ShapeDtypeStruct((S5_GROUP_CH, r, S5_STATE), F32)
    return pl.pallas_call(_disc_kernel, out_shape=(small, small, big, big), name="s5_discretise")(
        lam_re, lam_im, log_step, b_re, b_im)


def _s5_kernel(*refs, ns, steps, aliased):
    uf_ref, ub_ref, wb_ref, wc_ref, a_ref, h0_ref = refs[:6]
    yf_ref, yb_ref, hfin_ref, utm_ref, ytm_ref, bu_ref, xs_ref, st_ref = refs[6 + (2 if aliased else 0):]
    c = pl.program_id(1)
    slabs_per_k = 2 * S5_QUARTERS
    u_refs = (uf_ref, ub_ref)
    y_refs = (yf_ref, yb_ref)

    def seq_rows(b):
        return pl.ds(b, steps, stride=ns)

    @pl.when(c == 0)
    def _():
        st_ref[...] = h0_ref[...]

    for d in range(2):
        for k in range(S5_KBLK):
            for b in range(ns):
                utm_ref[d, k, seq_rows(b), :] = u_refs[d][b, :, k * 128:(k + 1) * 128]

    for k in range(S5_KBLK):
        for d in range(2):
            res = jnp.dot(utm_ref[d, k].astype(BF16), wb_ref[d, k], preferred_element_type=F32)
            for s in range(slabs_per_k):
                bu_ref[k % 2, d, s] = res[:, s * S5_SLAB:(s + 1) * S5_SLAB]

        chains = [(d, q) for d in range(2) for q in range(S5_QUARTERS)]
        coef = {dq: (a_ref[dq[0], k, 0:ns, dq[1] * S5_SLAB:(dq[1] + 1) * S5_SLAB],
                     a_ref[dq[0], k, 0:ns, S5_BLK + dq[1] * S5_SLAB:S5_BLK + (dq[1] + 1) * S5_SLAB])
                for dq in chains}
        state = {dq: (st_ref[dq[0], k, 0:ns, dq[1] * S5_SLAB:(dq[1] + 1) * S5_SLAB],
                      st_ref[dq[0], k, 0:ns, S5_BLK + dq[1] * S5_SLAB:S5_BLK + (dq[1] + 1) * S5_SLAB])
                 for dq in chains}
        for i in range(steps):
            for d, q in chains:
                t = i if d == 0 else steps - 1 - i
                tile = slice(t * ns, (t + 1) * ns)
                (ar, ai), (xr, xi) = coef[d, q], state[d, q]
                xr, xi = (ar * xr - ai * xi + bu_ref[k % 2, d, q, tile, :],
                          ar * xi + ai * xr + bu_ref[k % 2, d, S5_QUARTERS + q, tile, :])
                xs_ref[k % 2, d, q, tile, :] = xr
                xs_ref[k % 2, d, S5_QUARTERS + q, tile, :] = xi
                state[d, q] = (xr, xi)
        for d, q in chains:
            st_ref[d, k, 0:ns, q * S5_SLAB:(q + 1) * S5_SLAB] = state[d, q][0]
            st_ref[d, k, 0:ns, S5_BLK + q * S5_SLAB:S5_BLK + (q + 1) * S5_SLAB] = state[d, q][1]

        for d in range(2):
            xk = jnp.concatenate([xs_ref[k % 2, d, s] for s in range(slabs_per_k)], axis=1).astype(BF16)
            ytm_ref[d, k] = jnp.dot(xk, wc_ref[d, k], preferred_element_type=F32)

    for d in range(2):
        for k in range(S5_KBLK):
            for b in range(ns):
                y_refs[d][b, :, k * 128:(k + 1) * 128] = ytm_ref[d, k, seq_rows(b), :]

    @pl.when(c == pl.num_programs(1) - 1)
    def _():
        hfin_ref[...] = st_ref[...]


def _s5_call(u, wb, wc, a_bar, h0, latent, y_prev=None):
    if latent:
        length, ns, blk0 = L_LAT, N_LAT, TOK_CTX // (N_LAT * L_LAT)
    else:
        length, ns, blk0 = L_CTX, SUBLANES, 0
    steps = S5_ROWS // ns
    ng, nc = h0.shape[0], length // steps
    ublk = (ns, steps, S5_WIDTH)
    stblk = (None, 2, S5_KBLK, SUBLANES, 2 * S5_BLK)
    fwd = lambda g, c: (blk0 + g, c, 0)
    bwd = lambda g, c: (blk0 + g, nc - 1 - c, 0)
    const4 = lambda g, c: (0, 0, 0, 0)
    in_specs = [
        pl.BlockSpec(ublk, fwd),
        pl.BlockSpec(ublk, bwd),
        pl.BlockSpec((2, S5_KBLK, 128, 2 * S5_BLK), const4),
        pl.BlockSpec((2, S5_KBLK, 2 * S5_BLK, 128), const4),
        pl.BlockSpec((2, S5_KBLK, SUBLANES, 2 * S5_BLK), const4),
        pl.BlockSpec(stblk, lambda g, c: (g, 0, 0, 0, 0)),
    ]
    u3 = u.reshape(TOK // length, length, S5_WIDTH)
    args = [u3, u3, wb, wc, a_bar, h0]
    aliases = {}
    if y_prev is not None:
        in_specs += [pl.BlockSpec(memory_space=pl.ANY)] * 2
        args += [y.reshape(TOK // length, length, S5_WIDTH) for y in y_prev]
        aliases = {6: 0, 7: 1}
    y_shape = jax.ShapeDtypeStruct((TOK // length, length, S5_WIDTH), F32)
    yf, yb, hfin = pl.pallas_call(
        functools.partial(_s5_kernel, ns=ns, steps=steps, aliased=y_prev is not None),
        grid=(ng, nc),
        in_specs=in_specs,
        out_specs=[
            pl.BlockSpec(ublk, fwd),
            pl.BlockSpec(ublk, bwd),
            pl.BlockSpec(stblk, lambda g, c: (g, 0, 0, 0, 0)),
        ],
        out_shape=(y_shape, y_shape, jax.ShapeDtypeStruct(h0.shape, F32)),
        scratch_shapes=[
            pltpu.VMEM((2, S5_KBLK, S5_ROWS, S5_SLAB), F32),
            pltpu.VMEM((2, S5_KBLK, S5_ROWS, S5_SLAB), F32),
            pltpu.VMEM((2, 2, 2 * S5_QUARTERS, S5_ROWS, S5_SLAB), F32),
            pltpu.VMEM((2, 2, 2 * S5_QUARTERS, S5_ROWS, S5_SLAB), F32),
            pltpu.VMEM((2, S5_KBLK, SUBLANES, 2 * S5_BLK), F32),
        ],
        input_output_aliases=aliases,
        compiler_params=pltpu.CompilerParams(
            dimension_semantics=("parallel", "arbitrary"), vmem_limit_bytes=VMEM_BIG),
        name="s5_scan_latent" if latent else "s5_scan_context",
    )(*args)
    return yf.reshape(TOK, S5_WIDTH), yb.reshape(TOK, S5_WIDTH), hfin


def _s5_weights(ab_re, ab_im, bb_re, bb_im, c_re, c_im):
    eye = jnp.eye(S5_GPB, dtype=F32)

    def in_side(bb):
        t = bb.reshape(S5_GROUP_CH, 2, S5_KBLK, S5_GPB, S5_STATE)
        t = jnp.einsum('cdkgp,gh->dkgchp', t, eye)
        return t.reshape(2, S5_KBLK, S5_GPB * S5_GROUP_CH, S5_BLK)

    def out_side(cc):
        t = cc.reshape(2, S5_KBLK, S5_GPB, S5_GROUP_CH, S5_STATE)
        t = jnp.einsum('dkgcp,gh->dkgphc', t, eye)
        return t.reshape(2, S5_KBLK, S5_BLK, S5_GPB * S5_GROUP_CH)

    wb = jnp.concatenate([in_side(bb_re), in_side(bb_im)], axis=-1).astype(BF16)
    wc = jnp.concatenate([out_side(c_re), -out_side(c_im)], axis=-2).astype(BF16)
    a = jnp.concatenate([ab_re.reshape(2, S5_KBLK, 1, S5_BLK), ab_im.reshape(2, S5_KBLK, 1, S5_BLK)], axis=-1)
    a = jnp.broadcast_to(a, (2, S5_KBLK, SUBLANES, 2 * S5_BLK))
    return wb, wc, a


def _s5_state_in(s_re, s_im):
    def blk(s):
        return s.reshape(-1, 2, S5_KBLK, S5_BLK).transpose(1, 2, 0, 3)
    h = jnp.concatenate([blk(s_re), blk(s_im)], axis=-1)
    h = jnp.pad(h, ((0, 0), (0, 0), (0, SUBLANES - h.shape[2]), (0, 0)))
    return h[None]


def _s5_state_out(h):
    def unblk(s):
        return s.transpose(0, 3, 1, 2, 4).reshape(-1, 2, S5_GROUPS, S5_STATE)
    return unblk(h[..., :S5_BLK]), unblk(h[..., S5_BLK:])


def _gelu_tanh(x):
    return x * (0.5 * (1.0 + jnp.tanh(math.sqrt(2.0 / math.pi) * (x + 0.044715 * (x * x * x)))))


def _mix_kernel(yf_ref, yb_ref, u_ref, x_ref, xp_ref, xn_ref, mod_ref, nw_ref, wbg_ref, wcg_ref, wv_ref,
                d_ref, gw_ref, gb_ref, cw_ref, cb_ref, ow_ref, o_ref, wbcv_ref, gwb_ref, owb_ref):
    tm = TM_MIX
    ext = tm + 2 * HALO

    @pl.when(pl.program_id(0) == 0)
    def _():
        wbcv_ref[:, 0:CONV_W] = wbg_ref[...].astype(BF16)
        wbcv_ref[:, CONV_W:2 * CONV_W] = wcg_ref[...].astype(BF16)
        wbcv_ref[:, 2 * CONV_W:3 * CONV_W] = wv_ref[...].astype(BF16)
        gwb_ref[...] = gw_ref[...].astype(BF16)
        owb_ref[...] = ow_ref[...].astype(BF16)

    y = yf_ref[...] + yb_ref[...] + d_ref[...] * u_ref[...]
    z = _gelu_tanh(y)
    gate = jax.nn.sigmoid(jnp.dot(z.astype(BF16), gwb_ref[...], preferred_element_type=F32) + gb_ref[...])
    a_out = z * gate

    x = x_ref[...]
    xe = jnp.concatenate([xp_ref[...], x, xn_ref[...]], axis=0)
    h = _norm_mod(xe, nw_ref[...], mod_ref[1:2, :], mod_ref[0:1, :]).astype(BF16)
    bcv = jnp.dot(h, wbcv_ref[...], preferred_element_type=F32)
    bg = bcv[HALO:HALO + tm, 0:CONV_W]
    pe = bcv[:, CONV_W:2 * CONV_W] * bcv[:, 2 * CONV_W:3 * CONV_W]
    p = pe[HALO:HALO + tm, :]
    local = lax.broadcasted_iota(jnp.int32, (tm, 1), 0)
    row = pl.program_id(0) * tm + local
    seq_len = jnp.where(row < TOK_CTX, L_CTX, L_LAT)
    pos = jnp.bitwise_and(row, seq_len - 1)
    p_prev = jnp.where(pos == 0, 0.0, pltpu.roll(pe, 1, 0)[HALO:HALO + tm, :])
    p_next = jnp.where(pos == seq_len - 1, 0.0, pltpu.roll(pe, ext - 1, 0)[HALO:HALO + tm, :])
    conv = cw_ref[0:1, :] * p_prev + cw_ref[1:2, :] * p + cw_ref[2:3, :] * p_next + cb_ref[...]
    b_out = bg * conv

    out = (jnp.dot(a_out.astype(BF16), owb_ref[0:S5_WIDTH, :], preferred_element_type=F32)
           + jnp.dot(b_out.astype(BF16), owb_ref[S5_WIDTH:, :], preferred_element_type=F32))
    o_ref[...] = x + mod_ref[2:3, :] * out


def _mix_call(yf, yb, u, x, mod, nw, in_w, s5_d, glu_w, glu_b, conv_w, conv_b, out_w, layer):
    tm = TM_MIX
    hb = tm // HALO
    last_halo = TOK // HALO - 1
    tok_blk = lambda width: pl.BlockSpec((tm, width), lambda i: (i, 0))
    row1 = lambda width: pl.BlockSpec((1, width), lambda i: (0, 0))
    once = dict(pipeline_mode=pl.Buffered(1))
    w_col = lambda col: pl.BlockSpec((None, D_MODEL, CONV_W), lambda i: (layer, 0, col), **once)
    return pl.pallas_call(
        _mix_kernel,
        grid=(TOK // tm,),
        in_specs=[
            tok_blk(S5_WIDTH), tok_blk(S5_WIDTH), tok_blk(S5_WIDTH), tok_blk(D_MODEL),
            pl.BlockSpec((HALO, D_MODEL), lambda i: (jnp.maximum(i * hb - 1, 0), 0)),
            pl.BlockSpec((HALO, D_MODEL), lambda i: (jnp.minimum((i + 1) * hb, last_halo), 0)),
            pl.BlockSpec((None, MOD_ROWS, D_MODEL), lambda i: (_cond_row(i, tm), 0, 0)),
            row1(D_MODEL),
            w_col(1), w_col(2), w_col(3),
            row1(CONV_W),
            pl.BlockSpec((None, S5_WIDTH, S5_WIDTH), lambda i: (layer, 0, 0), **once),
            row1(CONV_W),
            pl.BlockSpec((MOD_ROWS, CONV_W), lambda i: (0, 0)),
            row1(CONV_W),
            pl.BlockSpec((None, D_MODEL, D_MODEL), lambda i: (layer, 0, 0), **once),
        ],
        out_specs=pl.BlockSpec((tm, D_MODEL), lambda i: (i, 0)),
        out_shape=jax.ShapeDtypeStruct((TOK, D_MODEL), F32),
        scratch_shapes=[pltpu.VMEM((D_MODEL, 3 * CONV_W), BF16), pltpu.VMEM((S5_WIDTH, S5_WIDTH), BF16),
                        pltpu.VMEM((D_MODEL, D_MODEL), BF16)],
        compiler_params=pltpu.CompilerParams(
            dimension_semantics=("arbitrary",), vmem_limit_bytes=VMEM_BIG),
        name="glu_conv_out",
    )(yf, yb, u, x, x, x, mod, nw.reshape(1, D_MODEL), in_w, in_w, in_w,
      s5_d.reshape(1, -1), glu_w, glu_b.reshape(1, -1),
      jnp.pad(conv_w, ((0, MOD_ROWS - conv_w.shape[0]), (0, 0))), conv_b.reshape(1, -1), out_w)


def _log_sigmoid(x):
    return jnp.minimum(x, 0.0) - jnp.log1p(jnp.exp(-jnp.abs(x)))


def _ret_kernel(*refs, latent, aliased):
    q_ref, k_ref, v_ref, g_ref, gam_ref, gn_ref = refs[:6]
    n_in = 6
    if latent:
        cos_ref, sin_ref, s0_ref = refs[6:9]
        n_in = 9
    n_in += aliased
    o_ref, sfin_ref, qs_ref, sc_ref, kv_ref, sk_ref = refs[n_in:]
    seq_chunks = RET_NC if latent else L_CTX // RET_C

    lg_f = _log_sigmoid(gam_ref[0, 0:1, :])
    lg_b = _log_sigmoid(gam_ref[1, 0:1, :])
    ii = lax.broadcasted_iota(jnp.int32, (RET_C, RET_C), 0)
    jj = lax.broadcasted_iota(jnp.int32, (RET_C, RET_C), 1)
    diff = (ii - jj).astype(F32)
    lgf_k, lgb_k = lg_f[:, :RET_DK], lg_b[:, :RET_DK]
    intra = (jnp.where(diff >= 0, jnp.exp(lgf_k * jnp.maximum(diff, 0.0)), 0.0)
             + jnp.where(diff <= 0, jnp.exp(lgb_k * jnp.maximum(-diff, 0.0)), 0.0))
    tk = lax.broadcasted_iota(jnp.int32, (RET_C, RET_DK), 0).astype(F32)
    tv = lax.broadcasted_iota(jnp.int32, (RET_C, RET_DV), 0).astype(F32)
    kdec_f = jnp.exp(lgf_k * (RET_C - 1.0 - tk))
    kdec_b = jnp.exp(lgb_k * tk)
    qdec_f = jnp.exp(lg_f * (tv + 1.0))
    qdec_b = jnp.exp(lg_b * (RET_C - tv))
    cd_f = jnp.exp(lg_f * RET_C)
    cd_b = jnp.exp(lg_b * RET_C)

    def rotate(x, cos, sin):
        return x * cos + pltpu.roll(x, RET_DK // 2, 1) * sin

    def swap_mid(s):
        qd = RET_DK // 4
        return jnp.concatenate([s[0:qd], s[2 * qd:3 * qd], s[qd:2 * qd], s[3 * qd:]], axis=0)

    tn_dims = (((0,), (0,)), ((), ()))
    nt_dims = (((1,), (1,)), ((), ()))

    def prep(c, _):
        r = pl.ds(pl.multiple_of(c * RET_C, RET_C), RET_C)
        q, k = q_ref[r, :], k_ref[r, :]
        if latent:
            cos, sin = cos_ref[r, :], sin_ref[r, :]
            q, k = rotate(q, cos, sin), rotate(k, cos, sin)
        qs = (q * (RET_DK ** -0.5)).astype(BF16)
        qs_ref[r, :] = qs
        scores = lax.dot_general(qs, k.astype(BF16), nt_dims, preferred_element_type=F32) * intra
        sc_ref[c] = scores.astype(BF16)
        kd = jnp.concatenate([(k * kdec_f).astype(BF16), (k * kdec_b).astype(BF16)], axis=1)
        kv_ref[c] = lax.dot_general(kd, v_ref[r, :], tn_dims, preferred_element_type=F32)
        return 0

    lax.fori_loop(0, RET_NC, prep, 0, unroll=RET_UNROLL)

    zeros = jnp.zeros((RET_DK, RET_DV), F32)
    s = swap_mid(s0_ref[0]) if latent else zeros
    for c in range(RET_NC):
        if c % seq_chunks == 0 and not (latent and c == 0):
            s = zeros
        sk_ref[c, :, 0:RET_DV] = s.astype(BF16)
        s = cd_f * s + kv_ref[c, 0:RET_DK, :]
        if (c + 1) % seq_chunks == 0:
            sfin_ref[c // seq_chunks, 0] = swap_mid(s)
    s = swap_mid(s0_ref[1]) if latent else zeros
    for c in reversed(range(RET_NC)):
        if (c + 1) % seq_chunks == 0 and not (latent and c == RET_NC - 1):
            s = zeros
        sk_ref[c, :, RET_DV:2 * RET_DV] = s.astype(BF16)
        s = cd_b * s + kv_ref[c, RET_DK:2 * RET_DK, :]
        if c % seq_chunks == 0:
            sfin_ref[c // seq_chunks, 1] = swap_mid(s)

    def emit(c, _):
        r = pl.ds(pl.multiple_of(c * RET_C, RET_C), RET_C)
        cross = jnp.dot(qs_ref[r, :], sk_ref[c], preferred_element_type=F32)
        o = (jnp.dot(sc_ref[c], v_ref[r, :], preferred_element_type=F32)
             + cross[:, 0:RET_DV] * qdec_f + cross[:, RET_DV:2 * RET_DV] * qdec_b)
        o = o * lax.rsqrt(jnp.mean(o * o, axis=-1, keepdims=True) + EPS) * gn_ref[...]
        o_ref[r, :] = (_silu(g_ref[r, :]) * o).astype(o_ref.dtype)
        return 0

    lax.fori_loop(0, RET_NC, emit, 0, unroll=RET_UNROLL)


def _ret_call(proj, gam, gn_w, latent, layer, rope=None, s0=None, o_prev=None, s_prev=None):
    nblk = TOK_CTX // RET_ROWS
    blk0 = nblk if latent else 0
    seqs = 1 if latent else RET_ROWS // L_CTX
    proj_qkg, proj_v = proj
    kcol, gcol = RET_QK // RET_DK, 2 * RET_QK // RET_DV
    in_specs = [
        pl.BlockSpec((RET_ROWS, RET_DK), lambda b, h: (blk0 + b, h)),
        pl.BlockSpec((RET_ROWS, RET_DK), lambda b, h: (blk0 + b, kcol + h)),
        pl.BlockSpec((RET_ROWS, RET_DV), lambda b, h: (blk0 + b, h)),
        pl.BlockSpec((RET_ROWS, RET_DV), lambda b, h: (blk0 + b, gcol + h)),
        pl.BlockSpec((None, 2, SUBLANES, RET_DV), lambda b, h: (h, 0, 0, 0)),
        pl.BlockSpec((1, RET_DV), lambda b, h: (0, h)),
    ]
    args = [proj_qkg, proj_qkg, proj_v, proj_qkg, gam, gn_w.reshape(1, RET_V)]
    aliases = {}
    if latent:
        in_specs += [
            pl.BlockSpec((RET_ROWS, RET_DK), lambda b, h: (0, 0)),
            pl.BlockSpec((RET_ROWS, RET_DK), lambda b, h: (0, 0)),
            pl.BlockSpec((None, 2, None, RET_DK, RET_DV), lambda b, h: (b, 0, h, 0, 0)),
            pl.BlockSpec(memory_space=pl.ANY),
        ]
        args += [rope[0], rope[1], s0, o_prev]
        aliases = {len(args) - 1: 0}
        s_shape = jax.ShapeDtypeStruct((N_LAT, 1, 2, RET_H, RET_DK, RET_DV), F32)
        s_layer = 0
    else:
        s_shape = jax.ShapeDtypeStruct((N_CTX, DEPTH // 2, 2, RET_H, RET_DK, RET_DV), F32)
        s_layer = layer
        if s_prev is not None:
            in_specs += [pl.BlockSpec(memory_space=pl.ANY)]
            args += [s_prev]
            aliases = {len(args) - 1: 1}
    return pl.pallas_call(
        functools.partial(_ret_kernel, latent=latent, aliased=len(aliases)),
        grid=(nblk, RET_H),
        in_specs=in_specs,
        out_specs=[
            pl.BlockSpec((RET_ROWS, RET_DV), lambda b, h: (blk0 + b, h)),
            pl.BlockSpec((seqs, None, 2, None, RET_DK, RET_DV), lambda b, h: (b, s_layer, 0, h, 0, 0)),
        ],
        out_shape=(jax.ShapeDtypeStruct((TOK, RET_V), BF16), s_shape),
        scratch_shapes=[
            pltpu.VMEM((RET_ROWS, RET_DK), BF16),
            pltpu.VMEM((RET_NC, RET_C, RET_C), BF16),
            pltpu.VMEM((RET_NC, 2 * RET_DK, RET_DV), F32),
            pltpu.VMEM((RET_NC, RET_DK, 2 * RET_DV), BF16),
        ],
        input_output_aliases=aliases,
        compiler_params=pltpu.CompilerParams(
            dimension_semantics=("parallel", "arbitrary"), vmem_limit_bytes=VMEM_MID),
        name="retention_latent" if latent else "retention_context",
    )(*args)


def _rope_tables():
    n_freq = RET_DK // 4
    t = jnp.arange(L_LAT)
    row = (t // GRID_W).astype(F32)
    col = (t % GRID_W).astype(F32)
    inv_freq = jnp.power(ROPE_BASE, -jnp.arange(n_freq, dtype=F32) / n_freq)
    ar, ac = row[:, None] * inv_freq, col[:, None] * inv_freq
    cos = jnp.concatenate([jnp.cos(ar), jnp.cos(ac), jnp.cos(ar), jnp.cos(ac)], axis=-1)
    sin = jnp.concatenate([-jnp.sin(ar), -jnp.sin(ac), jnp.sin(ar), jnp.sin(ac)], axis=-1)
    return cos, sin


def kernel(x_prompt, x_sample, state_s5_re, state_s5_im, state_ret, c, c_ctx, norm1_w, norm2_w, ada_w, ada_b,
           hy_in_w, hy_out_w, s5_lam_re, s5_lam_im, s5_log_step, s5_b_re, s5_b_im, s5_c_re, s5_c_im,
           s5_d, s5_glu_w, s5_glu_b, conv_w, conv_b, ret_in_w, ret_out_w, ret_gamma_logit, ret_gn_w,
           mlp_w1, mlp_w2, final_norm_w):
    x = (x_prompt.reshape(TOK_CTX, D_MODEL), x_sample.reshape(TOK - TOK_CTX, D_MODEL))

    cond = jnp.concatenate([c_ctx[None, :], c], axis=0)
    cond = jnp.pad(cond, ((0, MOD_ROWS - N_COND), (0, 0)))
    mod_all = _ada_call(cond, ada_w, ada_b)
    mod_all = mod_all[:, :N_COND].reshape(DEPTH, N_COND, N_MOD, D_MODEL)
    mod_all = jnp.pad(mod_all, ((0, 0), (0, 0), (0, MOD_ROWS - N_MOD), (0, 0)))

    n_s5 = s5_lam_re.shape[0]
    rows = n_s5 * 2 * S5_GROUPS
    ab_re, ab_im, bb_re, bb_im = _disc_call(
        s5_lam_re.reshape(rows, S5_STATE), s5_lam_im.reshape(rows, S5_STATE),
        jnp.broadcast_to(s5_log_step.reshape(rows, 1), (rows, S5_STATE)),
        jnp.moveaxis(s5_b_re, -1, 0).reshape(S5_GROUP_CH, rows, S5_STATE),
        jnp.moveaxis(s5_b_im, -1, 0).reshape(S5_GROUP_CH, rows, S5_STATE))
    ab_re = ab_re.reshape(n_s5, 2, S5_GROUPS, S5_STATE)
    ab_im = ab_im.reshape(n_s5, 2, S5_GROUPS, S5_STATE)
    bb_re = bb_re.reshape(S5_GROUP_CH, n_s5, 2, S5_GROUPS, S5_STATE)
    bb_im = bb_im.reshape(S5_GROUP_CH, n_s5, 2, S5_GROUPS, S5_STATE)

    rope = _rope_tables()
    zero_state = jnp.zeros((N_CTX // SUBLANES, 2, S5_KBLK, SUBLANES, 2 * S5_BLK), F32)

    new_re, new_im, new_ret = [], [], None
    y_prompt = y_sample = None
    for i in range(DEPTH):
        j = i // 2
        mod = mod_all[i]
        if i % 2 == 0:
            u, x = _uproj_call(x, mod, norm1_w[i], hy_in_w, j)
            wb, wc, a_bar = _s5_weights(ab_re[j], ab_im[j], bb_re[:, j], bb_im[:, j], s5_c_re[j], s5_c_im[j])
            yf, yb, h_ctx = _s5_call(u, wb, wc, a_bar, zero_state, latent=False)
            yf, yb, _ = _s5_call(u, wb, wc, a_bar, _s5_state_in(state_s5_re[:, j], state_s5_im[:, j]),
                                 latent=True, y_prev=(yf, yb))
            x = _mix_call(yf, yb, u, x, mod, norm1_w[i], hy_in_w, s5_d[j], s5_glu_w, s5_glu_b[j],
                          conv_w[j], conv_b[j], hy_out_w, j)
            s_re, s_im = _s5_state_out(h_ctx)
            new_re.append(s_re)
            new_im.append(s_im)
        else:
            proj = _in_call(x, mod, norm1_w[i], ret_in_w, j)
            gam = jnp.broadcast_to(ret_gamma_logit[j].T[:, :, None, None], (RET_H, 2, SUBLANES, RET_DV))
            o, new_ret = _ret_call(proj, gam, ret_gn_w[j], latent=False, layer=j, s_prev=new_ret)
            o, _ = _ret_call(proj, gam, ret_gn_w[j], latent=True, layer=j, rope=rope, s0=state_ret[:, j], o_prev=o)
            x = _resid_call(o, ret_out_w, x, mod, j)
        if i < DEPTH - 1:
            x = _mlp_call(x, mod, norm2_w[i], mlp_w1, mlp_w2, i)
        else:
            y_prompt, y_sample = _mlp_call(x, mod, norm2_w[i], mlp_w1, mlp_w2, i, fw=final_norm_w)

    return (y_prompt.reshape(N_CTX, L_CTX, D_MODEL), y_sample.reshape(N_LAT, L_LAT, D_MODEL),
            jnp.stack(new_re, 1), jnp.stack(new_im, 1), new_ret)
```

```python
import functools
import math

import jax
import jax.numpy as jnp
from jax import lax
from jax.experimental import pallas as pl
from jax.experimental.pallas import tpu as pltpu

F32 = jnp.float32
BF16 = jnp.bfloat16

D_MODEL = 1024
DEPTH = 4
N_CTX, L_CTX = 32, 256
N_LAT, L_LAT = 4, 2048
TOK_CTX = N_CTX * L_CTX
TOK = TOK_CTX + N_LAT * L_LAT
N_COND = 1 + N_LAT
GRID_W = 64
EPS = 1e-6

S5_WIDTH = 512
S5_GROUP_CH = 16
S5_GROUPS = 32
S5_STATE = 64
S5_KBLK = 4
S5_GPB = S5_GROUPS // S5_KBLK
S5_BLK = S5_GPB * S5_STATE
S5_SLAB = 128
S5_QUARTERS = S5_BLK // S5_SLAB
S5_ROWS = 512
SUBLANES = 8
CONV_W = 512

RET_H = 8
RET_DK = 128
RET_DV = 256
RET_QK = RET_H * RET_DK
RET_V = RET_H * RET_DV
RET_IN = 2 * RET_QK + 2 * RET_V
RET_C = 128
RET_ROWS = 2048
RET_NC = RET_ROWS // RET_C
RET_UNROLL = 8
ROPE_BASE = 10000.0
MLP_H = 4 * D_MODEL
N_MOD = 6
MOD_ROWS = 8

VMEM_BIG = 56 * 1024 * 1024
VMEM_MID = 40 * 1024 * 1024

TM_DENSE = 1024
TN_DENSE = 1024
TM_PROJ = 512
TM_MLP = 512
MLP_CHUNKS = 8
RET_IN_CHUNKS = RET_IN // TN_DENSE
RET_QK_CHUNKS = 2 * RET_QK // TN_DENSE
RET_V_CHUNKS = RET_V // TN_DENSE
TM_MIX = 512
HALO = 8


def _cond_row(i, tm):
    row0 = i * tm
    return jnp.where(row0 < TOK_CTX, 0, 1 + (row0 - TOK_CTX) // L_LAT)


def _norm_mod(x, nw, scale, shift):
    y = x * lax.rsqrt(jnp.mean(x * x, axis=-1, keepdims=True) + EPS)
    return (y * nw) * (1.0 + scale) + shift


def _silu(x):
    return x * jax.nn.sigmoid(x)


def _ada_kernel(c_ref, w_ref, b_ref, o_ref):
    sc = _silu(c_ref[...]).astype(BF16)
    o_ref[0] = jnp.dot(sc, w_ref[0].astype(BF16), preferred_element_type=F32) + b_ref[0]


def _ada_call(cond, ada_w, ada_b):
    tn = 1536
    return pl.pallas_call(
        _ada_kernel,
        grid=(DEPTH, N_MOD * D_MODEL // tn),
        in_specs=[
            pl.BlockSpec((MOD_ROWS, D_MODEL), lambda l, j: (0, 0)),
            pl.BlockSpec((1, D_MODEL, tn), lambda l, j: (l, 0, j)),
            pl.BlockSpec((1, 1, tn), lambda l, j: (l, 0, j)),
        ],
        out_specs=pl.BlockSpec((1, MOD_ROWS, tn), lambda l, j: (l, 0, j)),
        out_shape=jax.ShapeDtypeStruct((DEPTH, MOD_ROWS, N_MOD * D_MODEL), F32),
        compiler_params=pltpu.CompilerParams(
            dimension_semantics=("arbitrary", "arbitrary"), vmem_limit_bytes=VMEM_MID),
        name="ada_mod",
    )(cond, ada_w, ada_b.reshape(DEPTH, 1, N_MOD * D_MODEL))


def _pair_halves(w):
    qd = RET_DK // 4
    cols = []
    for hd in range(w.shape[1] // RET_DK):
        blk = w[:, hd * RET_DK:(hd + 1) * RET_DK]
        lane = lax.broadcasted_iota(jnp.int32, blk.shape, 1)
        blk = jnp.where(jnp.logical_and(lane >= qd, lane < 2 * qd), pltpu.roll(blk, RET_DK - qd, 1),
                        jnp.where(jnp.logical_and(lane >= 2 * qd, lane < 3 * qd), pltpu.roll(blk, qd, 1), blk))
        cols.append(blk)
    return jnp.concatenate(cols, axis=1)


def _in_kernel(x_ref, mod_ref, nw_ref, w_ref, oa_ref, ov_ref, wb_ref):
    cw = TN_DENSE
    s = pl.program_id(0)

    @pl.when(s < RET_QK_CHUNKS)
    def _():
        wb_ref[s] = _pair_halves(w_ref[...]).astype(BF16)

    @pl.when(jnp.logical_and(s >= RET_QK_CHUNKS, s < RET_IN_CHUNKS))
    def _():
        wb_ref[s] = w_ref[...].astype(BF16)

    @pl.when(s >= RET_IN_CHUNKS)
    def _():
        h = _norm_mod(x_ref[...], nw_ref[...], mod_ref[1:2, :], mod_ref[0:1, :]).astype(BF16)
        for c in range(RET_IN_CHUNKS):
            res = jnp.dot(h, wb_ref[c], preferred_element_type=F32)
            if RET_QK_CHUNKS <= c < RET_QK_CHUNKS + RET_V_CHUNKS:
                cv = c - RET_QK_CHUNKS
                ov_ref[:, cv * cw:(cv + 1) * cw] = res.astype(BF16)
            else:
                ca = c if c < RET_QK_CHUNKS else c - RET_V_CHUNKS
                oa_ref[:, ca * cw:(ca + 1) * cw] = res


def _in_call(x, mod, nw, w, layer):
    tm, cw = TM_PROJ, TN_DENSE
    tile = lambda s: jnp.maximum(s - RET_IN_CHUNKS, 0)
    return pl.pallas_call(
        _in_kernel,
        grid=(RET_IN_CHUNKS + TOK // tm,),
        in_specs=[
            pl.BlockSpec((tm, D_MODEL), lambda s: (tile(s), 0)),
            pl.BlockSpec((None, MOD_ROWS, D_MODEL), lambda s: (_cond_row(tile(s), tm), 0, 0)),
            pl.BlockSpec((1, D_MODEL), lambda s: (0, 0)),
            pl.BlockSpec((None, D_MODEL, cw), lambda s: (layer, 0, jnp.minimum(s, RET_IN_CHUNKS - 1))),
        ],
        out_specs=[pl.BlockSpec((tm, RET_IN - RET_V), lambda s: (tile(s), 0)),
                   pl.BlockSpec((tm, RET_V), lambda s: (tile(s), 0))],
        out_shape=(jax.ShapeDtypeStruct((TOK, RET_IN - RET_V), F32), jax.ShapeDtypeStruct((TOK, RET_V), BF16)),
        scratch_shapes=[pltpu.VMEM((RET_IN_CHUNKS, D_MODEL, cw), BF16)],
        compiler_params=pltpu.CompilerParams(
            dimension_semantics=("arbitrary",), vmem_limit_bytes=VMEM_BIG),
        name="norm_proj",
    )(x, mod, nw.reshape(1, D_MODEL), w)


def _uproj_kernel(*refs, paired, n_ctx_tiles):
    if paired:
        xc_ref, xl_ref, mod_ref, nw_ref, w_ref, u_ref, xcat_ref, wb_ref = refs
        x = jnp.where(pl.program_id(0) < n_ctx_tiles, xc_ref[...], xl_ref[...])
        xcat_ref[...] = x
    else:
        x_ref, mod_ref, nw_ref, w_ref, u_ref, wb_ref = refs
        x = x_ref[...]

    @pl.when(pl.program_id(0) == 0)
    def _():
        wb_ref[...] = w_ref[...].astype(BF16)

    h = _norm_mod(x, nw_ref[...], mod_ref[1:2, :], mod_ref[0:1, :]).astype(BF16)
    u_ref[...] = jnp.dot(h, wb_ref[...], preferred_element_type=F32)


def _uproj_call(x, mod, nw, w, layer):
    tm = TM_DENSE
    paired = isinstance(x, tuple)
    n_ctx_tiles = TOK_CTX // tm
    if paired:
        x_specs = [pl.BlockSpec((tm, D_MODEL), lambda i: (jnp.minimum(i, n_ctx_tiles - 1), 0)),
                   pl.BlockSpec((tm, D_MODEL), lambda i: (jnp.maximum(i - n_ctx_tiles, 0), 0))]
        xs = list(x)
    else:
        x_specs = [pl.BlockSpec((tm, D_MODEL), lambda i: (i, 0))]
        xs = [x]
    tok_out = lambda width: pl.BlockSpec((tm, width), lambda i: (i, 0))
    out_specs = [tok_out(S5_WIDTH)] + ([tok_out(D_MODEL)] if paired else [])
    out_shape = [jax.ShapeDtypeStruct((TOK, S5_WIDTH), F32)] + (
        [jax.ShapeDtypeStruct((TOK, D_MODEL), F32)] if paired else [])
    res = pl.pallas_call(
        functools.partial(_uproj_kernel, paired=paired, n_ctx_tiles=n_ctx_tiles),
        grid=(TOK // tm,),
        in_specs=x_specs + [
            pl.BlockSpec((None, MOD_ROWS, D_MODEL), lambda i: (_cond_row(i, tm), 0, 0)),
            pl.BlockSpec((1, D_MODEL), lambda i: (0, 0)),
            pl.BlockSpec((None, D_MODEL, S5_WIDTH), lambda i: (layer, 0, 0)),
        ],
        out_specs=out_specs,
        out_shape=out_shape,
        scratch_shapes=[pltpu.VMEM((D_MODEL, S5_WIDTH), BF16)],
        compiler_params=pltpu.CompilerParams(
            dimension_semantics=("arbitrary",), vmem_limit_bytes=VMEM_MID),
        name="norm_uproj",
    )(*xs, mod, nw.reshape(1, D_MODEL), w)
    return (res[0], res[1]) if paired else (res[0], x)


def _mlp_kernel(*refs, final, with_u, n_ctx_tiles):
    if final:
        x_ref, mod_ref, nw_ref, w1_ref, w2_ref, fw_ref, oc_ref, ol_ref, w1b_ref, w2b_ref = refs
    elif with_u:
        (x_ref, mod_ref, nw_ref, w1_ref, w2_ref, modn_ref, nwn_ref, wu_ref,
         o_ref, u_ref, w1b_ref, w2b_ref, wub_ref) = refs
    else:
        x_ref, mod_ref, nw_ref, w1_ref, w2_ref, o_ref, w1b_ref, w2b_ref = refs
    s = pl.program_id(0)

    @pl.when(s < MLP_CHUNKS)
    def _():
        w1b_ref[s] = w1_ref[...].astype(BF16)
        w2b_ref[s] = w2_ref[...].astype(BF16)

    if with_u:
        @pl.when(s == 0)
        def _():
            wub_ref[...] = wu_ref[...].astype(BF16)

    def result():
        x = x_ref[...]
        h = _norm_mod(x, nw_ref[...], mod_ref[4:5, :], mod_ref[3:4, :]).astype(BF16)
        acc = None
        for c in range(MLP_CHUNKS):
            a = jnp.maximum(jnp.dot(h, w1b_ref[c], preferred_element_type=F32), 0.0)
            t = jnp.dot((a * a).astype(BF16), w2b_ref[c], preferred_element_type=F32)
            acc = t if acc is None else acc + t
        y = x + mod_ref[5:6, :] * acc
        if final:
            y = y * lax.rsqrt(jnp.mean(y * y, axis=-1, keepdims=True) + EPS) * fw_ref[...]
        return y

    tile = s - MLP_CHUNKS
    if final:
        @pl.when(jnp.logical_and(tile >= 0, tile < n_ctx_tiles))
        def _():
            oc_ref[...] = result()

        @pl.when(tile >= n_ctx_tiles)
        def _():
            ol_ref[...] = result()
    else:
        @pl.when(tile >= 0)
        def _():
            y = result()
            o_ref[...] = y
            if with_u:
                hn = _norm_mod(y, nwn_ref[...], modn_ref[1:2, :], modn_ref[0:1, :]).astype(BF16)
                u_ref[...] = jnp.dot(hn, wub_ref[...], preferred_element_type=F32)


def _mlp_call(x, mod, nw, w1, w2, layer, fw=None, next_u=None):
    tm = TM_MLP
    final, with_u = fw is not None, next_u is not None
    n_tiles, n_ctx_tiles = TOK // tm, TOK_CTX // tm
    ch = MLP_H // MLP_CHUNKS
    tile = lambda s: jnp.maximum(s - MLP_CHUNKS, 0)
    chunk = lambda s: jnp.minimum(s, MLP_CHUNKS - 1)
    mod_spec = pl.BlockSpec((None, MOD_ROWS, D_MODEL), lambda s: (_cond_row(tile(s), tm), 0, 0))
    row_spec = pl.BlockSpec((1, D_MODEL), lambda s: (0, 0))
    tok_spec = lambda width: pl.BlockSpec((tm, width), lambda s: (tile(s), 0))
    in_specs = [
        tok_spec(D_MODEL), mod_spec, row_spec,
        pl.BlockSpec((None, D_MODEL, ch), lambda s: (layer, 0, chunk(s))),
        pl.BlockSpec((None, ch, D_MODEL), lambda s: (layer, chunk(s), 0)),
    ]
    args = [x, mod, nw.reshape(1, D_MODEL), w1, w2]
    scratch = [pltpu.VMEM((MLP_CHUNKS, D_MODEL, ch), BF16), pltpu.VMEM((MLP_CHUNKS, ch, D_MODEL), BF16)]
    if final:
        in_specs.append(row_spec)
        args.append(fw.reshape(1, D_MODEL))
        out_specs = [
            pl.BlockSpec((tm, D_MODEL), lambda s: (jnp.clip(s - MLP_CHUNKS, 0, n_ctx_tiles - 1), 0)),
            pl.BlockSpec((tm, D_MODEL), lambda s: (jnp.maximum(s - MLP_CHUNKS - n_ctx_tiles, 0), 0)),
        ]
        out_shape = (jax.ShapeDtypeStruct((TOK_CTX, D_MODEL), F32),
                     jax.ShapeDtypeStruct((TOK - TOK_CTX, D_MODEL), F32))
    elif with_u:
        mod_n, nw_n, w_n, layer_n = next_u
        in_specs += [mod_spec, row_spec, pl.BlockSpec((None, D_MODEL, S5_WIDTH), lambda s: (layer_n, 0, 0))]
        args += [mod_n, nw_n.reshape(1, D_MODEL), w_n]
        out_specs = [tok_spec(D_MODEL), tok_spec(S5_WIDTH)]
        out_shape = (jax.ShapeDtypeStruct((TOK, D_MODEL), F32), jax.ShapeDtypeStruct((TOK, S5_WIDTH), F32))
        scratch.append(pltpu.VMEM((D_MODEL, S5_WIDTH), BF16))
    else:
        out_specs = tok_spec(D_MODEL)
        out_shape = jax.ShapeDtypeStruct((TOK, D_MODEL), F32)
    return pl.pallas_call(
        functools.partial(_mlp_kernel, final=final, with_u=with_u, n_ctx_tiles=n_ctx_tiles),
        grid=(MLP_CHUNKS + n_tiles,),
        in_specs=in_specs,
        out_specs=out_specs,
        out_shape=out_shape,
        scratch_shapes=scratch,
        compiler_params=pltpu.CompilerParams(
            dimension_semantics=("arbitrary",), vmem_limit_bytes=VMEM_BIG),
        name="mlp_final" if final else "mlp",
    )(*args)


def _resid_kernel(a_ref, w_ref, x_ref, mod_ref, o_ref, wb_ref):
    @pl.when(pl.program_id(0) == 0)
    def _():
        wb_ref[...] = w_ref[...].astype(BF16)

    y = jnp.dot(a_ref[...], wb_ref[...], preferred_element_type=F32)
    o_ref[...] = x_ref[...] + mod_ref[2:3, :] * y


def _resid_call(a, w, x, mod, layer):
    tm = TM_DENSE
    k = a.shape[1]
    return pl.pallas_call(
        _resid_kernel,
        grid=(TOK // tm,),
        in_specs=[
            pl.BlockSpec((tm, k), lambda i: (i, 0)),
            pl.BlockSpec((None, k, D_MODEL), lambda i: (layer, 0, 0)),
            pl.BlockSpec((tm, D_MODEL), lambda i: (i, 0)),
            pl.BlockSpec((None, MOD_ROWS, D_MODEL), lambda i: (_cond_row(i, tm), 0, 0)),
        ],
        out_specs=pl.BlockSpec((tm, D_MODEL), lambda i: (i, 0)),
        out_shape=jax.ShapeDtypeStruct((TOK, D_MODEL), F32),
        scratch_shapes=[pltpu.VMEM((k, D_MODEL), BF16)],
        compiler_params=pltpu.CompilerParams(
            dimension_semantics=("arbitrary",), vmem_limit_bytes=VMEM_BIG),
        name="out_proj_resid",
    )(a, w, x, mod)


def _disc_kernel(lr_ref, li_ref, ls_ref, br_ref, bi_ref, abr_ref, abi_ref, bbr_ref, bbi_ref):
    lr, li = lr_ref[...], li_ref[...]
    dt = jnp.exp(ls_ref[...])
    mag = jnp.exp(lr * dt)
    abr = mag * jnp.cos(li * dt)
    abi = mag * jnp.sin(li * dt)
    den = lr * lr + li * li
    fr = ((abr - 1.0) * lr + abi * li) / den
    fi = (abi * lr - (abr - 1.0) * li) / den
    abr_ref[...] = abr
    abi_ref[...] = abi
    for c in range(S5_GROUP_CH):
        br, bi = br_ref[c], bi_ref[c]
        bbr_ref[c] = fr * br - fi * bi
        bbi_ref[c] = fr * bi + fi * br


def _disc_call(lam_re, lam_im, log_step, b_re, b_im):
    r = lam_re.shape[0]
    small = jax.ShapeDtypeStruct((r, S5_STATE), F32)
    big = jax.ShapeDtypeStruct((S5_GROUP_CH, r, S5_STATE), F32)
    return pl.pallas_call(_disc_kernel, out_shape=(small, small, big, big), name="s5_discretise")(
        lam_re, lam_im, log_step, b_re, b_im)


def _s5_kernel(*refs, ns, steps, aliased):
    uf_ref, ub_ref, wb_ref, wc_ref, a_ref, h0_ref = refs[:6]
    yf_ref, yb_ref, hfin_ref, utm_ref, ytm_ref, bu_ref, xs_ref, st_ref = refs[6 + (2 if aliased else 0):]
    c = pl.program_id(1)
    slabs_per_k = 2 * S5_QUARTERS
    u_refs = (uf_ref, ub_ref)
    y_refs = (yf_ref, yb_ref)

    def seq_rows(b):
        return pl.ds(b, steps, stride=ns)

    @pl.when(c == 0)
    def _():
        st_ref[...] = h0_ref[...]

    for d in range(2):
        for k in range(S5_KBLK):
            for b in range(ns):
                utm_ref[d, k, seq_rows(b), :] = u_refs[d][b, :, k * 128:(k + 1) * 128]

    for k in range(S5_KBLK):
        for d in range(2):
            res = jnp.dot(utm_ref[d, k].astype(BF16), wb_ref[d, k], preferred_element_type=F32)
            for s in range(slabs_per_k):
                bu_ref[k % 2, d, s] = res[:, s * S5_SLAB:(s + 1) * S5_SLAB]

        chains = [(d, q) for d in range(2) for q in range(S5_QUARTERS)]
        coef = {dq: (a_ref[dq[0], k, 0:ns, dq[1] * S5_SLAB:(dq[1] + 1) * S5_SLAB],
                     a_ref[dq[0], k, 0:ns, S5_BLK + dq[1] * S5_SLAB:S5_BLK + (dq[1] + 1) * S5_SLAB])
                for dq in chains}
        state = {dq: (st_ref[dq[0], k, 0:ns, dq[1] * S5_SLAB:(dq[1] + 1) * S5_SLAB],
                      st_ref[dq[0], k, 0:ns, S5_BLK + dq[1] * S5_SLAB:S5_BLK + (dq[1] + 1) * S5_SLAB])
                 for dq in chains}
        for i in range(steps):
            for d, q in chains:
                t = i if d == 0 else steps - 1 - i
                tile = slice(t * ns, (t + 1) * ns)
                (ar, ai), (xr, xi) = coef[d, q], state[d, q]
                xr, xi = (ar * xr - ai * xi + bu_ref[k % 2, d, q, tile, :],
                          ar * xi + ai * xr + bu_ref[k % 2, d, S5_QUARTERS + q, tile, :])
                xs_ref[k % 2, d, q, tile, :] = xr
                xs_ref[k % 2, d, S5_QUARTERS + q, tile, :] = xi
                state[d, q] = (xr, xi)
        for d, q in chains:
            st_ref[d, k, 0:ns, q * S5_SLAB:(q + 1) * S5_SLAB] = state[d, q][0]
            st_ref[d, k, 0:ns, S5_BLK + q * S5_SLAB:S5_BLK + (q + 1) * S5_SLAB] = state[d, q][1]

        for d in range(2):
            xk = jnp.concatenate([xs_ref[k % 2, d, s] for s in range(slabs_per_k)], axis=1).astype(BF16)
            ytm_ref[d, k] = jnp.dot(xk, wc_ref[d, k], preferred_element_type=F32)

    for d in range(2):
        for k in range(S5_KBLK):
            for b in range(ns):
                y_refs[d][b, :, k * 128:(k + 1) * 128] = ytm_ref[d, k, seq_rows(b), :]

    @pl.when(c == pl.num_programs(1) - 1)
    def _():
        hfin_ref[...] = st_ref[...]


def _s5_call(u, wb, wc, a_bar, h0, latent, y_prev=None):
    if latent:
        length, ns, blk0 = L_LAT, N_LAT, TOK_CTX // (N_LAT * L_LAT)
    else:
        length, ns, blk0 = L_CTX, SUBLANES, 0
    steps = S5_ROWS // ns
    ng, nc = h0.shape[0], length // steps
    ublk = (ns, steps, S5_WIDTH)
    stblk = (None, 2, S5_KBLK, SUBLANES, 2 * S5_BLK)
    fwd = lambda g, c: (blk0 + g, c, 0)
    bwd = lambda g, c: (blk0 + g, nc - 1 - c, 0)
    const4 = lambda g, c: (0, 0, 0, 0)
    in_specs = [
        pl.BlockSpec(ublk, fwd),
        pl.BlockSpec(ublk, bwd),
        pl.BlockSpec((2, S5_KBLK, 128, 2 * S5_BLK), const4),
        pl.BlockSpec((2, S5_KBLK, 2 * S5_BLK, 128), const4),
        pl.BlockSpec((2, S5_KBLK, SUBLANES, 2 * S5_BLK), const4),
        pl.BlockSpec(stblk, lambda g, c: (g, 0, 0, 0, 0)),
    ]
    u3 = u.reshape(TOK // length, length, S5_WIDTH)
    args = [u3, u3, wb, wc, a_bar, h0]
    aliases = {}
    if y_prev is not None:
        in_specs += [pl.BlockSpec(memory_space=pl.ANY)] * 2
        args += [y.reshape(TOK // length, length, S5_WIDTH) for y in y_prev]
        aliases = {6: 0, 7: 1}
    y_shape = jax.ShapeDtypeStruct((TOK // length, length, S5_WIDTH), F32)
    yf, yb, hfin = pl.pallas_call(
        functools.partial(_s5_kernel, ns=ns, steps=steps, aliased=y_prev is not None),
        grid=(ng, nc),
        in_specs=in_specs,
        out_specs=[
            pl.BlockSpec(ublk, fwd),
            pl.BlockSpec(ublk, bwd),
            pl.BlockSpec(stblk, lambda g, c: (g, 0, 0, 0, 0)),
        ],
        out_shape=(y_shape, y_shape, jax.ShapeDtypeStruct(h0.shape, F32)),
        scratch_shapes=[
            pltpu.VMEM((2, S5_KBLK, S5_ROWS, S5_SLAB), F32),
            pltpu.VMEM((2, S5_KBLK, S5_ROWS, S5_SLAB), F32),
            pltpu.VMEM((2, 2, 2 * S5_QUARTERS, S5_ROWS, S5_SLAB), F32),
            pltpu.VMEM((2, 2, 2 * S5_QUARTERS, S5_ROWS, S5_SLAB), F32),
            pltpu.VMEM((2, S5_KBLK, SUBLANES, 2 * S5_BLK), F32),
        ],
        input_output_aliases=aliases,
        compiler_params=pltpu.CompilerParams(
            dimension_semantics=("parallel", "arbitrary"), vmem_limit_bytes=VMEM_BIG),
        name="s5_scan_latent" if latent else "s5_scan_context",
    )(*args)
    return yf.reshape(TOK, S5_WIDTH), yb.reshape(TOK, S5_WIDTH), hfin


def _s5_weights(ab_re, ab_im, bb_re, bb_im, c_re, c_im):
    eye = jnp.eye(S5_GPB, dtype=F32)

    def in_side(bb):
        t = bb.reshape(S5_GROUP_CH, 2, S5_KBLK, S5_GPB, S5_STATE)
        t = jnp.einsum('cdkgp,gh->dkgchp', t, eye)
        return t.reshape(2, S5_KBLK, S5_GPB * S5_GROUP_CH, S5_BLK)

    def out_side(cc):
        t = cc.reshape(2, S5_KBLK, S5_GPB, S5_GROUP_CH, S5_STATE)
        t = jnp.einsum('dkgcp,gh->dkgphc', t, eye)
        return t.reshape(2, S5_KBLK, S5_BLK, S5_GPB * S5_GROUP_CH)

    wb = jnp.concatenate([in_side(bb_re), in_side(bb_im)], axis=-1).astype(BF16)
    wc = jnp.concatenate([out_side(c_re), -out_side(c_im)], axis=-2).astype(BF16)
    a = jnp.concatenate([ab_re.reshape(2, S5_KBLK, 1, S5_BLK), ab_im.reshape(2, S5_KBLK, 1, S5_BLK)], axis=-1)
    a = jnp.broadcast_to(a, (2, S5_KBLK, SUBLANES, 2 * S5_BLK))
    return wb, wc, a


def _s5_state_in(s_re, s_im):
    def blk(s):
        return s.reshape(-1, 2, S5_KBLK, S5_BLK).transpose(1, 2, 0, 3)
    h = jnp.concatenate([blk(s_re), blk(s_im)], axis=-1)
    h = jnp.pad(h, ((0, 0), (0, 0), (0, SUBLANES - h.shape[2]), (0, 0)))
    return h[None]


def _s5_state_out(h):
    def unblk(s):
        return s.transpose(0, 3, 1, 2, 4).reshape(-1, 2, S5_GROUPS, S5_STATE)
    return unblk(h[..., :S5_BLK]), unblk(h[..., S5_BLK:])


def _gelu_tanh(x):
    return x * (0.5 * (1.0 + jnp.tanh(math.sqrt(2.0 / math.pi) * (x + 0.044715 * (x * x * x)))))


def _mix_kernel(yf_ref, yb_ref, u_ref, x_ref, xp_ref, xn_ref, mod_ref, nw_ref, wbg_ref, wcg_ref, wv_ref,
                d_ref, gw_ref, gb_ref, cw_ref, cb_ref, ow_ref, o_ref, wbcv_ref, gwb_ref, owb_ref):
    tm = TM_MIX
    ext = tm + 2 * HALO

    @pl.when(pl.program_id(0) == 0)
    def _():
        wbcv_ref[:, 0:CONV_W] = wbg_ref[...].astype(BF16)
        wbcv_ref[:, CONV_W:2 * CONV_W] = wcg_ref[...].astype(BF16)
        wbcv_ref[:, 2 * CONV_W:3 * CONV_W] = wv_ref[...].astype(BF16)
        gwb_ref[...] = gw_ref[...].astype(BF16)
        owb_ref[...] = ow_ref[...].astype(BF16)

    y = yf_ref[...] + yb_ref[...] + d_ref[...] * u_ref[...]
    z = _gelu_tanh(y)
    gate = jax.nn.sigmoid(jnp.dot(z.astype(BF16), gwb_ref[...], preferred_element_type=F32) + gb_ref[...])
    a_out = z * gate

    x = x_ref[...]
    xe = jnp.concatenate([xp_ref[...], x, xn_ref[...]], axis=0)
    h = _norm_mod(xe, nw_ref[...], mod_ref[1:2, :], mod_ref[0:1, :]).astype(BF16)
    bcv = jnp.dot(h, wbcv_ref[...], preferred_element_type=F32)
    bg = bcv[HALO:HALO + tm, 0:CONV_W]
    pe = bcv[:, CONV_W:2 * CONV_W] * bcv[:, 2 * CONV_W:3 * CONV_W]
    p = pe[HALO:HALO + tm, :]
    local = lax.broadcasted_iota(jnp.int32, (tm, 1), 0)
    row = pl.program_id(0) * tm + local
    seq_len = jnp.where(row < TOK_CTX, L_CTX, L_LAT)
    pos = jnp.bitwise_and(row, seq_len - 1)
    p_prev = jnp.where(pos == 0, 0.0, pltpu.roll(pe, 1, 0)[HALO:HALO + tm, :])
    p_next = jnp.where(pos == seq_len - 1, 0.0, pltpu.roll(pe, ext - 1, 0)[HALO:HALO + tm, :])
    conv = cw_ref[0:1, :] * p_prev + cw_ref[1:2, :] * p + cw_ref[2:3, :] * p_next + cb_ref[...]
    b_out = bg * conv

    out = (jnp.dot(a_out.astype(BF16), owb_ref[0:S5_WIDTH, :], preferred_element_type=F32)
           + jnp.dot(b_out.astype(BF16), owb_ref[S5_WIDTH:, :], preferred_element_type=F32))
    o_ref[...] = x + mod_ref[2:3, :] * out


def _mix_call(yf, yb, u, x, mod, nw, in_w, s5_d, glu_w, glu_b, conv_w, conv_b, out_w, layer):
    tm = TM_MIX
    hb = tm // HALO
    last_halo = TOK // HALO - 1
    tok_blk = lambda width: pl.BlockSpec((tm, width), lambda i: (i, 0))
    row1 = lambda width: pl.BlockSpec((1, width), lambda i: (0, 0))
    once = dict(pipeline_mode=pl.Buffered(1))
    w_col = lambda col: pl.BlockSpec((None, D_MODEL, CONV_W), lambda i: (layer, 0, col), **once)
    return pl.pallas_call(
        _mix_kernel,
        grid=(TOK // tm,),
        in_specs=[
            tok_blk(S5_WIDTH), tok_blk(S5_WIDTH), tok_blk(S5_WIDTH), tok_blk(D_MODEL),
            pl.BlockSpec((HALO, D_MODEL), lambda i: (jnp.maximum(i * hb - 1, 0), 0)),
            pl.BlockSpec((HALO, D_MODEL), lambda i: (jnp.minimum((i + 1) * hb, last_halo), 0)),
            pl.BlockSpec((None, MOD_ROWS, D_MODEL), lambda i: (_cond_row(i, tm), 0, 0)),
            row1(D_MODEL),
            w_col(1), w_col(2), w_col(3),
            row1(CONV_W),
            pl.BlockSpec((None, S5_WIDTH, S5_WIDTH), lambda i: (layer, 0, 0), **once),
            row1(CONV_W),
            pl.BlockSpec((MOD_ROWS, CONV_W), lambda i: (0, 0)),
            row1(CONV_W),
            pl.BlockSpec((None, D_MODEL, D_MODEL), lambda i: (layer, 0, 0), **once),
        ],
        out_specs=pl.BlockSpec((tm, D_MODEL), lambda i: (i, 0)),
        out_shape=jax.ShapeDtypeStruct((TOK, D_MODEL), F32),
        scratch_shapes=[pltpu.VMEM((D_MODEL, 3 * CONV_W), BF16), pltpu.VMEM((S5_WIDTH, S5_WIDTH), BF16),
                        pltpu.VMEM((D_MODEL, D_MODEL), BF16)],
        compiler_params=pltpu.CompilerParams(
            dimension_semantics=("arbitrary",), vmem_limit_bytes=VMEM_BIG),
        name="glu_conv_out",
    )(yf, yb, u, x, x, x, mod, nw.reshape(1, D_MODEL), in_w, in_w, in_w,
      s5_d.reshape(1, -1), glu_w, glu_b.reshape(1, -1),
      jnp.pad(conv_w, ((0, MOD_ROWS - conv_w.shape[0]), (0, 0))), conv_b.reshape(1, -1), out_w)


def _log_sigmoid(x):
    return jnp.minimum(x, 0.0) - jnp.log1p(jnp.exp(-jnp.abs(x)))


def _ret_kernel(*refs, latent, aliased):
    q_ref, k_ref, v_ref, g_ref, gam_ref, gn_ref = refs[:6]
    n_in = 6
    if latent:
        cos_ref, sin_ref, s0_ref = refs[6:9]
        n_in = 9
    n_in += aliased
    o_ref, sfin_ref, qs_ref, sc_ref, kv_ref, sk_ref = refs[n_in:]
    seq_chunks = RET_NC if latent else L_CTX // RET_C

    lg_f = _log_sigmoid(gam_ref[0, 0:1, :])
    lg_b = _log_sigmoid(gam_ref[1, 0:1, :])
    ii = lax.broadcasted_iota(jnp.int32, (RET_C, RET_C), 0)
    jj = lax.broadcasted_iota(jnp.int32, (RET_C, RET_C), 1)
    diff = (ii - jj).astype(F32)
    lgf_k, lgb_k = lg_f[:, :RET_DK], lg_b[:, :RET_DK]
    intra = (jnp.where(diff >= 0, jnp.exp(lgf_k * jnp.maximum(diff, 0.0)), 0.0)
             + jnp.where(diff <= 0, jnp.exp(lgb_k * jnp.maximum(-diff, 0.0)), 0.0))
    tk = lax.broadcasted_iota(jnp.int32, (RET_C, RET_DK), 0).astype(F32)
    tv = lax.broadcasted_iota(jnp.int32, (RET_C, RET_DV), 0).astype(F32)
    kdec_f = jnp.exp(lgf_k * (RET_C - 1.0 - tk))
    kdec_b = jnp.exp(lgb_k * tk)
    qdec_f = jnp.exp(lg_f * (tv + 1.0))
    qdec_b = jnp.exp(lg_b * (RET_C - tv))
    cd_f = jnp.exp(lg_f * RET_C)
    cd_b = jnp.exp(lg_b * RET_C)

    def rotate(x, cos, sin):
        return x * cos + pltpu.roll(x, RET_DK // 2, 1) * sin

    def swap_mid(s):
        qd = RET_DK // 4
        return jnp.concatenate([s[0:qd], s[2 * qd:3 * qd], s[qd:2 * qd], s[3 * qd:]], axis=0)

    tn_dims = (((0,), (0,)), ((), ()))
    nt_dims = (((1,), (1,)), ((), ()))

    def prep(c, _):
        r = pl.ds(pl.multiple_of(c * RET_C, RET_C), RET_C)
        q, k = q_ref[r, :], k_ref[r, :]
        if latent:
            cos, sin = cos_ref[r, :], sin_ref[r, :]
            q, k = rotate(q, cos, sin), rotate(k, cos, sin)
        qs = (q * (RET_DK ** -0.5)).astype(BF16)
        qs_ref[r, :] = qs
        scores = lax.dot_general(qs, k.astype(BF16), nt_dims, preferred_element_type=F32) * intra
        sc_ref[c] = scores.astype(BF16)
        kd = jnp.concatenate([(k * kdec_f).astype(BF16), (k * kdec_b).astype(BF16)], axis=1)
        kv_ref[c] = lax.dot_general(kd, v_ref[r, :], tn_dims, preferred_element_type=F32)
        return 0

    lax.fori_loop(0, RET_NC, prep, 0, unroll=RET_UNROLL)

    zeros = jnp.zeros((RET_DK, RET_DV), F32)
    s = swap_mid(s0_ref[0]) if latent else zeros
    for c in range(RET_NC):
        if c % seq_chunks == 0 and not (latent and c == 0):
            s = zeros
        sk_ref[c, :, 0:RET_DV] = s.astype(BF16)
        s = cd_f * s + kv_ref[c, 0:RET_DK, :]
        if (c + 1) % seq_chunks == 0:
            sfin_ref[c // seq_chunks, 0] = swap_mid(s)
    s = swap_mid(s0_ref[1]) if latent else zeros
    for c in reversed(range(RET_NC)):
        if (c + 1) % seq_chunks == 0 and not (latent and c == RET_NC - 1):
            s = zeros
        sk_ref[c, :, RET_DV:2 * RET_DV] = s.astype(BF16)
        s = cd_b * s + kv_ref[c, RET_DK:2 * RET_DK, :]
        if c % seq_chunks == 0:
            sfin_ref[c // seq_chunks, 1] = swap_mid(s)

    def emit(c, _):
        r = pl.ds(pl.multiple_of(c * RET_C, RET_C), RET_C)
        cross = jnp.dot(qs_ref[r, :], sk_ref[c], preferred_element_type=F32)
        o = (jnp.dot(sc_ref[c], v_ref[r, :], preferred_element_type=F32)
             + cross[:, 0:RET_DV] * qdec_f + cross[:, RET_DV:2 * RET_DV] * qdec_b)
        o = o * lax.rsqrt(jnp.mean(o * o, axis=-1, keepdims=True) + EPS) * gn_ref[...]
        o_ref[r, :] = (_silu(g_ref[r, :]) * o).astype(o_ref.dtype)
        return 0

    lax.fori_loop(0, RET_NC, emit, 0, unroll=RET_UNROLL)


def _ret_call(proj, gam, gn_w, latent, layer, rope=None, s0=None, o_prev=None, s_prev=None):
    nblk = TOK_CTX // RET_ROWS
    blk0 = nblk if latent else 0
    seqs = 1 if latent else RET_ROWS // L_CTX
    proj_qkg, proj_v = proj
    kcol, gcol = RET_QK // RET_DK, 2 * RET_QK // RET_DV
    in_specs = [
        pl.BlockSpec((RET_ROWS, RET_DK), lambda b, h: (blk0 + b, h)),
        pl.BlockSpec((RET_ROWS, RET_DK), lambda b, h: (blk0 + b, kcol + h)),
        pl.BlockSpec((RET_ROWS, RET_DV), lambda b, h: (blk0 + b, h)),
        pl.BlockSpec((RET_ROWS, RET_DV), lambda b, h: (blk0 + b, gcol + h)),
        pl.BlockSpec((None, 2, SUBLANES, RET_DV), lambda b, h: (h, 0, 0, 0)),
        pl.BlockSpec((1, RET_DV), lambda b, h: (0, h)),
    ]
    args = [proj_qkg, proj_qkg, proj_v, proj_qkg, gam, gn_w.reshape(1, RET_V)]
    aliases = {}
    if latent:
        in_specs += [
            pl.BlockSpec((RET_ROWS, RET_DK), lambda b, h: (0, 0)),
            pl.BlockSpec((RET_ROWS, RET_DK), lambda b, h: (0, 0)),
            pl.BlockSpec((None, None, 2, None, RET_DK, RET_DV), lambda b, h: (b, layer, 0, h, 0, 0)),
            pl.BlockSpec(memory_space=pl.ANY),
        ]
        args += [rope[0], rope[1], s0, o_prev]
        aliases = {len(args) - 1: 0}
        s_shape = jax.ShapeDtypeStruct((N_LAT, 1, 2, RET_H, RET_DK, RET_DV), F32)
        s_layer = 0
    else:
        s_shape = jax.ShapeDtypeStruct((N_CTX, DEPTH // 2, 2, RET_H, RET_DK, RET_DV), F32)
        s_layer = layer
        if s_prev is not None:
            in_specs += [pl.BlockSpec(memory_space=pl.ANY)]
            args += [s_prev]
            aliases = {len(args) - 1: 1}
    return pl.pallas_call(
        functools.partial(_ret_kernel, latent=latent, aliased=len(aliases)),
        grid=(nblk, RET_H),
        in_specs=in_specs,
        out_specs=[
            pl.BlockSpec((RET_ROWS, RET_DV), lambda b, h: (blk0 + b, h)),
            pl.BlockSpec((seqs, None, 2, None, RET_DK, RET_DV), lambda b, h: (b, s_layer, 0, h, 0, 0)),
        ],
        out_shape=(jax.ShapeDtypeStruct((TOK, RET_V), BF16), s_shape),
        scratch_shapes=[
            pltpu.VMEM((RET_ROWS, RET_DK), BF16),
            pltpu.VMEM((RET_NC, RET_C, RET_C), BF16),
            pltpu.VMEM((RET_NC, 2 * RET_DK, RET_DV), F32),
            pltpu.VMEM((RET_NC, RET_DK, 2 * RET_DV), BF16),
        ],
        input_output_aliases=aliases,
        compiler_params=pltpu.CompilerParams(
            dimension_semantics=("parallel", "arbitrary"), vmem_limit_bytes=VMEM_MID),
        name="retention_latent" if latent else "retention_context",
    )(*args)


def _rope_tables():
    n_freq = RET_DK // 4
    t = jnp.arange(L_LAT)
    row = (t // GRID_W).astype(F32)
    col = (t % GRID_W).astype(F32)
    inv_freq = jnp.power(ROPE_BASE, -jnp.arange(n_freq, dtype=F32) / n_freq)
    ar, ac = row[:, None] * inv_freq, col[:, None] * inv_freq
    cos = jnp.concatenate([jnp.cos(ar), jnp.cos(ac), jnp.cos(ar), jnp.cos(ac)], axis=-1)
    sin = jnp.concatenate([-jnp.sin(ar), -jnp.sin(ac), jnp.sin(ar), jnp.sin(ac)], axis=-1)
    return cos, sin


def kernel(x_prompt, x_sample, state_s5_re, state_s5_im, state_ret, c, c_ctx, norm1_w, norm2_w, ada_w, ada_b,
           hy_in_w, hy_out_w, s5_lam_re, s5_lam_im, s5_log_step, s5_b_re, s5_b_im, s5_c_re, s5_c_im,
           s5_d, s5_glu_w, s5_glu_b, conv_w, conv_b, ret_in_w, ret_out_w, ret_gamma_logit, ret_gn_w,
           mlp_w1, mlp_w2, final_norm_w):
    x = (x_prompt.reshape(TOK_CTX, D_MODEL), x_sample.reshape(TOK - TOK_CTX, D_MODEL))

    cond = jnp.concatenate([c_ctx[None, :], c], axis=0)
    cond = jnp.pad(cond, ((0, MOD_ROWS - N_COND), (0, 0)))
    mod_all = _ada_call(cond, ada_w, ada_b)
    mod_all = mod_all[:, :N_COND].reshape(DEPTH, N_COND, N_MOD, D_MODEL)
    mod_all = jnp.pad(mod_all, ((0, 0), (0, 0), (0, MOD_ROWS - N_MOD), (0, 0)))

    n_s5 = s5_lam_re.shape[0]
    rows = n_s5 * 2 * S5_GROUPS
    ab_re, ab_im, bb_re, bb_im = _disc_call(
        s5_lam_re.reshape(rows, S5_STATE), s5_lam_im.reshape(rows, S5_STATE),
        jnp.broadcast_to(s5_log_step.reshape(rows, 1), (rows, S5_STATE)),
        jnp.moveaxis(s5_b_re, -1, 0).reshape(S5_GROUP_CH, rows, S5_STATE),
        jnp.moveaxis(s5_b_im, -1, 0).reshape(S5_GROUP_CH, rows, S5_STATE))
    ab_re = ab_re.reshape(n_s5, 2, S5_GROUPS, S5_STATE)
    ab_im = ab_im.reshape(n_s5, 2, S5_GROUPS, S5_STATE)
    bb_re = bb_re.reshape(S5_GROUP_CH, n_s5, 2, S5_GROUPS, S5_STATE)
    bb_im = bb_im.reshape(S5_GROUP_CH, n_s5, 2, S5_GROUPS, S5_STATE)

    rope = _rope_tables()
    zero_state = jnp.zeros((N_CTX // SUBLANES, 2, S5_KBLK, SUBLANES, 2 * S5_BLK), F32)

    new_re, new_im, new_ret = [], [], None
    y_prompt = y_sample = u_next = None
    for i in range(DEPTH):
        j = i // 2
        mod = mod_all[i]
        if i % 2 == 0:
            if u_next is None:
                u, x = _uproj_call(x, mod, norm1_w[i], hy_in_w, j)
            else:
                u = u_next
            wb, wc, a_bar = _s5_weights(ab_re[j], ab_im[j], bb_re[:, j], bb_im[:, j], s5_c_re[j], s5_c_im[j])
            yf, yb, h_ctx = _s5_call(u, wb, wc, a_bar, zero_state, latent=False)
            yf, yb, _ = _s5_call(u, wb, wc, a_bar, _s5_state_in(state_s5_re[:, j], state_s5_im[:, j]),
                                 latent=True, y_prev=(yf, yb))
            x = _mix_call(yf, yb, u, x, mod, norm1_w[i], hy_in_w, s5_d[j], s5_glu_w, s5_glu_b[j],
                          conv_w[j], conv_b[j], hy_out_w, j)
            s_re, s_im = _s5_state_out(h_ctx)
            new_re.append(s_re)
            new_im.append(s_im)
        else:
            proj = _in_call(x, mod, norm1_w[i], ret_in_w, j)
            gam = jnp.broadcast_to(ret_gamma_logit[j].T[:, :, None, None], (RET_H, 2, SUBLANES, RET_DV))
            o, new_ret = _ret_call(proj, gam, ret_gn_w[j], latent=False, layer=j, s_prev=new_ret)
            o, _ = _ret_call(proj, gam, ret_gn_w[j], latent=True, layer=j, rope=rope, s0=state_ret, o_prev=o)
            x = _resid_call(o, ret_out_w, x, mod, j)
        if i + 1 < DEPTH and (i + 1) % 2 == 0:
            x, u_next = _mlp_call(x, mod, norm2_w[i], mlp_w1, mlp_w2, i,
                                  next_u=(mod_all[i + 1], norm1_w[i + 1], hy_in_w, (i + 1) // 2))
        elif i + 1 < DEPTH:
            x = _mlp_call(x, mod, norm2_w[i], mlp_w1, mlp_w2, i)
        else:
            y_prompt, y_sample = _mlp_call(x, mod, norm2_w[i], mlp_w1, mlp_w2, i, fw=final_norm_w)

    return (y_prompt.reshape(N_CTX, L_CTX, D_MODEL), y_sample.reshape(N_LAT, L_LAT, D_MODEL),
            jnp.stack(new_re, 1), jnp.stack(new_im, 1), new_ret)
```

```python
import functools
import math

import jax
import jax.numpy as jnp
from jax import lax
from jax.experimental import pallas as pl
from jax.experimental.pallas import tpu as pltpu

F32 = jnp.float32
BF16 = jnp.bfloat16

D_MODEL = 1024
DEPTH = 4
N_CTX, L_CTX = 32, 256
N_LAT, L_LAT = 4, 2048
TOK_CTX = N_CTX * L_CTX
TOK = TOK_CTX + N_LAT * L_LAT
N_COND = 1 + N_LAT
GRID_W = 64
EPS = 1e-6

S5_WIDTH = 512
S5_GROUP_CH = 16
S5_GROUPS = 32
S5_STATE = 64
S5_KBLK = 4
S5_GPB = S5_GROUPS // S5_KBLK
S5_BLK = S5_GPB * S5_STATE
S5_SLAB = 128
S5_QUARTERS = S5_BLK // S5_SLAB
S5_ROWS = 512
SUBLANES = 8
CONV_W = 512

RET_H = 8
RET_DK = 128
RET_DV = 256
RET_QK = RET_H * RET_DK
RET_V = RET_H * RET_DV
RET_IN = 2 * RET_QK + 2 * RET_V
RET_C = 128
RET_ROWS = 2048
RET_NC = RET_ROWS // RET_C
RET_UNROLL = 8
ROPE_BASE = 10000.0
MLP_H = 4 * D_MODEL
N_MOD = 6
MOD_ROWS = 8

VMEM_BIG = 56 * 1024 * 1024
VMEM_MID = 40 * 1024 * 1024

TM_DENSE = 1024
TN_DENSE = 1024
TM_PROJ = 512
TM_MLP = 1024
MLP_CHUNKS = 8
RET_IN_CHUNKS = RET_IN // TN_DENSE
RET_QK_CHUNKS = 2 * RET_QK // TN_DENSE
RET_V_CHUNKS = RET_V // TN_DENSE
TM_MIX = 512
HALO = 8


def _cond_row(i, tm):
    row0 = i * tm
    return jnp.where(row0 < TOK_CTX, 0, 1 + (row0 - TOK_CTX) // L_LAT)


def _norm_mod(x, nw, scale, shift):
    y = x * lax.rsqrt(jnp.mean(x * x, axis=-1, keepdims=True) + EPS)
    return (y * nw) * (1.0 + scale) + shift


def _silu(x):
    return x * jax.nn.sigmoid(x)


def _ada_kernel(c_ref, w_ref, b_ref, o_ref):
    sc = _silu(c_ref[...]).astype(BF16)
    o_ref[0] = jnp.dot(sc, w_ref[0].astype(BF16), preferred_element_type=F32) + b_ref[0]


def _ada_call(cond, ada_w, ada_b):
    tn = 1536
    return pl.pallas_call(
        _ada_kernel,
        grid=(DEPTH, N_MOD * D_MODEL // tn),
        in_specs=[
            pl.BlockSpec((MOD_ROWS, D_MODEL), lambda l, j: (0, 0)),
            pl.BlockSpec((1, D_MODEL, tn), lambda l, j: (l, 0, j)),
            pl.BlockSpec((1, 1, tn), lambda l, j: (l, 0, j)),
        ],
        out_specs=pl.BlockSpec((1, MOD_ROWS, tn), lambda l, j: (l, 0, j)),
        out_shape=jax.ShapeDtypeStruct((DEPTH, MOD_ROWS, N_MOD * D_MODEL), F32),
        compiler_params=pltpu.CompilerParams(
            dimension_semantics=("arbitrary", "arbitrary"), vmem_limit_bytes=VMEM_MID),
        name="ada_mod",
    )(cond, ada_w, ada_b.reshape(DEPTH, 1, N_MOD * D_MODEL))


def _pair_halves(w):
    qd = RET_DK // 4
    cols = []
    for hd in range(w.shape[1] // RET_DK):
        blk = w[:, hd * RET_DK:(hd + 1) * RET_DK]
        lane = lax.broadcasted_iota(jnp.int32, blk.shape, 1)
        blk = jnp.where(jnp.logical_and(lane >= qd, lane < 2 * qd), pltpu.roll(blk, RET_DK - qd, 1),
                        jnp.where(jnp.logical_and(lane >= 2 * qd, lane < 3 * qd), pltpu.roll(blk, qd, 1), blk))
        cols.append(blk)
    return jnp.concatenate(cols, axis=1)


def _in_kernel(x_ref, mod_ref, nw_ref, w_ref, oa_ref, ov_ref, wb_ref):
    cw = TN_DENSE
    s = pl.program_id(0)

    @pl.when(s < RET_QK_CHUNKS)
    def _():
        wb_ref[s] = _pair_halves(w_ref[...]).astype(BF16)

    @pl.when(jnp.logical_and(s >= RET_QK_CHUNKS, s < RET_IN_CHUNKS))
    def _():
        wb_ref[s] = w_ref[...].astype(BF16)

    @pl.when(s >= RET_IN_CHUNKS)
    def _():
        h = _norm_mod(x_ref[...], nw_ref[...], mod_ref[1:2, :], mod_ref[0:1, :]).astype(BF16)
        for c in range(RET_IN_CHUNKS):
            res = jnp.dot(h, wb_ref[c], preferred_element_type=F32)
            if RET_QK_CHUNKS <= c < RET_QK_CHUNKS + RET_V_CHUNKS:
                cv = c - RET_QK_CHUNKS
                ov_ref[:, cv * cw:(cv + 1) * cw] = res.astype(BF16)
            else:
                ca = c if c < RET_QK_CHUNKS else c - RET_V_CHUNKS
                oa_ref[:, ca * cw:(ca + 1) * cw] = res


def _in_call(x, mod, nw, w, layer):
    tm, cw = TM_PROJ, TN_DENSE
    tile = lambda s: jnp.maximum(s - RET_IN_CHUNKS, 0)
    return pl.pallas_call(
        _in_kernel,
        grid=(RET_IN_CHUNKS + TOK // tm,),
        in_specs=[
            pl.BlockSpec((tm, D_MODEL), lambda s: (tile(s), 0)),
            pl.BlockSpec((None, MOD_ROWS, D_MODEL), lambda s: (_cond_row(tile(s), tm), 0, 0)),
            pl.BlockSpec((1, D_MODEL), lambda s: (0, 0)),
            pl.BlockSpec((None, D_MODEL, cw), lambda s: (layer, 0, jnp.minimum(s, RET_IN_CHUNKS - 1))),
        ],
        out_specs=[pl.BlockSpec((tm, RET_IN - RET_V), lambda s: (tile(s), 0)),
                   pl.BlockSpec((tm, RET_V), lambda s: (tile(s), 0))],
        out_shape=(jax.ShapeDtypeStruct((TOK, RET_IN - RET_V), F32), jax.ShapeDtypeStruct((TOK, RET_V), BF16)),
        scratch_shapes=[pltpu.VMEM((RET_IN_CHUNKS, D_MODEL, cw), BF16)],
        compiler_params=pltpu.CompilerParams(
            dimension_semantics=("arbitrary",), vmem_limit_bytes=VMEM_BIG),
        name="norm_proj",
    )(x, mod, nw.reshape(1, D_MODEL), w)


def _uproj_kernel(*refs, paired, n_ctx_tiles):
    if paired:
        xc_ref, xl_ref, mod_ref, nw_ref, w_ref, u_ref, xcat_ref, wb_ref = refs
        x = jnp.where(pl.program_id(0) < n_ctx_tiles, xc_ref[...], xl_ref[...])
        xcat_ref[...] = x
    else:
        x_ref, mod_ref, nw_ref, w_ref, u_ref, wb_ref = refs
        x = x_ref[...]

    @pl.when(pl.program_id(0) == 0)
    def _():
        wb_ref[...] = w_ref[...].astype(BF16)

    h = _norm_mod(x, nw_ref[...], mod_ref[1:2, :], mod_ref[0:1, :]).astype(BF16)
    u_ref[...] = jnp.dot(h, wb_ref[...], preferred_element_type=F32)


def _uproj_call(x, mod, nw, w, layer):
    tm = TM_DENSE
    paired = isinstance(x, tuple)
    n_ctx_tiles = TOK_CTX // tm
    if paired:
        x_specs = [pl.BlockSpec((tm, D_MODEL), lambda i: (jnp.minimum(i, n_ctx_tiles - 1), 0)),
                   pl.BlockSpec((tm, D_MODEL), lambda i: (jnp.maximum(i - n_ctx_tiles, 0), 0))]
        xs = list(x)
    else:
        x_specs = [pl.BlockSpec((tm, D_MODEL), lambda i: (i, 0))]
        xs = [x]
    tok_out = lambda width: pl.BlockSpec((tm, width), lambda i: (i, 0))
    out_specs = [tok_out(S5_WIDTH)] + ([tok_out(D_MODEL)] if paired else [])
    out_shape = [jax.ShapeDtypeStruct((TOK, S5_WIDTH), F32)] + (
        [jax.ShapeDtypeStruct((TOK, D_MODEL), F32)] if paired else [])
    res = pl.pallas_call(
        functools.partial(_uproj_kernel, paired=paired, n_ctx_tiles=n_ctx_tiles),
        grid=(TOK // tm,),
        in_specs=x_specs + [
            pl.BlockSpec((None, MOD_ROWS, D_MODEL), lambda i: (_cond_row(i, tm), 0, 0)),
            pl.BlockSpec((1, D_MODEL), lambda i: (0, 0)),
            pl.BlockSpec((None, D_MODEL, S5_WIDTH), lambda i: (layer, 0, 0)),
        ],
        out_specs=out_specs,
        out_shape=out_shape,
        scratch_shapes=[pltpu.VMEM((D_MODEL, S5_WIDTH), BF16)],
        compiler_params=pltpu.CompilerParams(
            dimension_semantics=("arbitrary",), vmem_limit_bytes=VMEM_MID),
        name="norm_uproj",
    )(*xs, mod, nw.reshape(1, D_MODEL), w)
    return (res[0], res[1]) if paired else (res[0], x)


def _mlp_kernel(*refs, final, with_u, n_ctx_tiles):
    if final:
        x_ref, mod_ref, nw_ref, w1_ref, w2_ref, fw_ref, oc_ref, ol_ref, w1b_ref, w2b_ref = refs
    elif with_u:
        (x_ref, mod_ref, nw_ref, w1_ref, w2_ref, modn_ref, nwn_ref, wu_ref,
         o_ref, u_ref, w1b_ref, w2b_ref, wub_ref) = refs
    else:
        x_ref, mod_ref, nw_ref, w1_ref, w2_ref, o_ref, w1b_ref, w2b_ref = refs
    s = pl.program_id(0)

    @pl.when(s < MLP_CHUNKS)
    def _():
        w1b_ref[s] = w1_ref[...].astype(BF16)
        w2b_ref[s] = w2_ref[...].astype(BF16)

    if with_u:
        @pl.when(s == 0)
        def _():
            wub_ref[...] = wu_ref[...].astype(BF16)

    def result():
        x = x_ref[...]
        h = _norm_mod(x, nw_ref[...], mod_ref[4:5, :], mod_ref[3:4, :]).astype(BF16)
        acc = None
        for c in range(MLP_CHUNKS):
            a = jnp.maximum(jnp.dot(h, w1b_ref[c], preferred_element_type=F32), 0.0)
            t = jnp.dot((a * a).astype(BF16), w2b_ref[c], preferred_element_type=F32)
            acc = t if acc is None else acc + t
        y = x + mod_ref[5:6, :] * acc
        if final:
            y = y * lax.rsqrt(jnp.mean(y * y, axis=-1, keepdims=True) + EPS) * fw_ref[...]
        return y

    tile = s - MLP_CHUNKS
    if final:
        @pl.when(jnp.logical_and(tile >= 0, tile < n_ctx_tiles))
        def _():
            oc_ref[...] = result()

        @pl.when(tile >= n_ctx_tiles)
        def _():
            ol_ref[...] = result()
    else:
        @pl.when(tile >= 0)
        def _():
            y = result()
            o_ref[...] = y
            if with_u:
                hn = _norm_mod(y, nwn_ref[...], modn_ref[1:2, :], modn_ref[0:1, :]).astype(BF16)
                u_ref[...] = jnp.dot(hn, wub_ref[...], preferred_element_type=F32)


def _mlp_call(x, mod, nw, w1, w2, layer, fw=None, next_u=None):
    tm = TM_MLP
    final, with_u = fw is not None, next_u is not None
    n_tiles, n_ctx_tiles = TOK // tm, TOK_CTX // tm
    ch = MLP_H // MLP_CHUNKS
    tile = lambda s: jnp.maximum(s - MLP_CHUNKS, 0)
    chunk = lambda s: jnp.minimum(s, MLP_CHUNKS - 1)
    mod_spec = pl.BlockSpec((None, MOD_ROWS, D_MODEL), lambda s: (_cond_row(tile(s), tm), 0, 0))
    row_spec = pl.BlockSpec((1, D_MODEL), lambda s: (0, 0))
    tok_spec = lambda width: pl.BlockSpec((tm, width), lambda s: (tile(s), 0))
    in_specs = [
        tok_spec(D_MODEL), mod_spec, row_spec,
        pl.BlockSpec((None, D_MODEL, ch), lambda s: (layer, 0, chunk(s))),
        pl.BlockSpec((None, ch, D_MODEL), lambda s: (layer, chunk(s), 0)),
    ]
    args = [x, mod, nw.reshape(1, D_MODEL), w1, w2]
    scratch = [pltpu.VMEM((MLP_CHUNKS, D_MODEL, ch), BF16), pltpu.VMEM((MLP_CHUNKS, ch, D_MODEL), BF16)]
    if final:
        in_specs.append(row_spec)
        args.append(fw.reshape(1, D_MODEL))
        out_specs = [
            pl.BlockSpec((tm, D_MODEL), lambda s: (jnp.clip(s - MLP_CHUNKS, 0, n_ctx_tiles - 1), 0)),
            pl.BlockSpec((tm, D_MODEL), lambda s: (jnp.maximum(s - MLP_CHUNKS - n_ctx_tiles, 0), 0)),
        ]
        out_shape = (jax.ShapeDtypeStruct((TOK_CTX, D_MODEL), F32),
                     jax.ShapeDtypeStruct((TOK - TOK_CTX, D_MODEL), F32))
    elif with_u:
        mod_n, nw_n, w_n, layer_n = next_u
        in_specs += [mod_spec, row_spec, pl.BlockSpec((None, D_MODEL, S5_WIDTH), lambda s: (layer_n, 0, 0))]
        args += [mod_n, nw_n.reshape(1, D_MODEL), w_n]
        out_specs = [tok_spec(D_MODEL), tok_spec(S5_WIDTH)]
        out_shape = (jax.ShapeDtypeStruct((TOK, D_MODEL), F32), jax.ShapeDtypeStruct((TOK, S5_WIDTH), F32))
        scratch.append(pltpu.VMEM((D_MODEL, S5_WIDTH), BF16))
    else:
        out_specs = tok_spec(D_MODEL)
        out_shape = jax.ShapeDtypeStruct((TOK, D_MODEL), F32)
    return pl.pallas_call(
        functools.partial(_mlp_kernel, final=final, with_u=with_u, n_ctx_tiles=n_ctx_tiles),
        grid=(MLP_CHUNKS + n_tiles,),
        in_specs=in_specs,
        out_specs=out_specs,
        out_shape=out_shape,
        scratch_shapes=scratch,
        compiler_params=pltpu.CompilerParams(
            dimension_semantics=("arbitrary",), vmem_limit_bytes=VMEM_BIG),
        name="mlp_final" if final else "mlp",
    )(*args)


def _resid_kernel(a_ref, w_ref, x_ref, mod_ref, o_ref, wb_ref):
    @pl.when(pl.program_id(0) == 0)
    def _():
        wb_ref[...] = w_ref[...].astype(BF16)

    y = jnp.dot(a_ref[...], wb_ref[...], preferred_element_type=F32)
    o_ref[...] = x_ref[...] + mod_ref[2:3, :] * y


def _resid_call(a, w, x, mod, layer):
    tm = TM_DENSE
    k = a.shape[1]
    return pl.pallas_call(
        _resid_kernel,
        grid=(TOK // tm,),
        in_specs=[
            pl.BlockSpec((tm, k), lambda i: (i, 0)),
            pl.BlockSpec((None, k, D_MODEL), lambda i: (layer, 0, 0)),
            pl.BlockSpec((tm, D_MODEL), lambda i: (i, 0)),
            pl.BlockSpec((None, MOD_ROWS, D_MODEL), lambda i: (_cond_row(i, tm), 0, 0)),
        ],
        out_specs=pl.BlockSpec((tm, D_MODEL), lambda i: (i, 0)),
        out_shape=jax.ShapeDtypeStruct((TOK, D_MODEL), F32),
        scratch_shapes=[pltpu.VMEM((k, D_MODEL), BF16)],
        compiler_params=pltpu.CompilerParams(
            dimension_semantics=("arbitrary",), vmem_limit_bytes=VMEM_BIG),
        name="out_proj_resid",
    )(a, w, x, mod)


def _disc_kernel(lr_ref, li_ref, ls_ref, br_ref, bi_ref, abr_ref, abi_ref, bbr_ref, bbi_ref):
    lr, li = lr_ref[...], li_ref[...]
    dt = jnp.exp(ls_ref[...])
    mag = jnp.exp(lr * dt)
    abr = mag * jnp.cos(li * dt)
    abi = mag * jnp.sin(li * dt)
    den = lr * lr + li * li
    fr = ((abr - 1.0) * lr + abi * li) / den
    fi = (abi * lr - (abr - 1.0) * li) / den
    abr_ref[...] = abr
    abi_ref[...] = abi
    for c in range(S5_GROUP_CH):
        br, bi = br_ref[c], bi_ref[c]
        bbr_ref[c] = fr * br - fi * bi
        bbi_ref[c] = fr * bi + fi * br


def _disc_call(lam_re, lam_im, log_step, b_re, b_im):
    r = lam_re.shape[0]
    small = jax.ShapeDtypeStruct((r, S5_STATE), F32)
    big = jax.ShapeDtypeStruct((S5_GROUP_CH, r, S5_STATE), F32)
    return pl.pallas_call(_disc_kernel, out_shape=(small, small, big, big), name="s5_discretise")(
        lam_re, lam_im, log_step, b_re, b_im)


def _s5_kernel(*refs, ns, steps, aliased):
    uf_ref, ub_ref, wb_ref, wc_ref, a_ref, h0_ref = refs[:6]
    yf_ref, yb_ref, hfin_ref, utm_ref, ytm_ref, bu_ref, xs_ref, st_ref = refs[6 + (2 if aliased else 0):]
    c = pl.program_id(1)
    slabs_per_k = 2 * S5_QUARTERS
    u_refs = (uf_ref, ub_ref)
    y_refs = (yf_ref, yb_ref)

    def seq_rows(b):
        return pl.ds(b, steps, stride=ns)

    @pl.when(c == 0)
    def _():
        st_ref[...] = h0_ref[...]

    for d in range(2):
        for k in range(S5_KBLK):
            for b in range(ns):
                utm_ref[d, k, seq_rows(b), :] = u_refs[d][b, :, k * 128:(k + 1) * 128]

    for k in range(S5_KBLK):
        for d in range(2):
            res = jnp.dot(utm_ref[d, k].astype(BF16), wb_ref[d, k], preferred_element_type=F32)
            for s in range(slabs_per_k):
                bu_ref[k % 2, d, s] = res[:, s * S5_SLAB:(s + 1) * S5_SLAB]

        chains = [(d, q) for d in range(2) for q in range(S5_QUARTERS)]
        coef = {dq: (a_ref[dq[0], k, 0:ns, dq[1] * S5_SLAB:(dq[1] + 1) * S5_SLAB],
                     a_ref[dq[0], k, 0:ns, S5_BLK + dq[1] * S5_SLAB:S5_BLK + (dq[1] + 1) * S5_SLAB])
                for dq in chains}
        state = {dq: (st_ref[dq[0], k, 0:ns, dq[1] * S5_SLAB:(dq[1] + 1) * S5_SLAB],
                      st_ref[dq[0], k, 0:ns, S5_BLK + dq[1] * S5_SLAB:S5_BLK + (dq[1] + 1) * S5_SLAB])
                 for dq in chains}
        for i in range(steps):
            for d, q in chains:
                t = i if d == 0 else steps - 1 - i
                tile = slice(t * ns, (t + 1) * ns)
                (ar, ai), (xr, xi) = coef[d, q], state[d, q]
                xr, xi = (ar * xr - ai * xi + bu_ref[k % 2, d, q, tile, :],
                          ar * xi + ai * xr + bu_ref[k % 2, d, S5_QUARTERS + q, tile, :])
                xs_ref[k % 2, d, q, tile, :] = xr
                xs_ref[k % 2, d, S5_QUARTERS + q, tile, :] = xi
                state[d, q] = (xr, xi)
        for d, q in chains:
            st_ref[d, k, 0:ns, q * S5_SLAB:(q + 1) * S5_SLAB] = state[d, q][0]
            st_ref[d, k, 0:ns, S5_BLK + q * S5_SLAB:S5_BLK + (q + 1) * S5_SLAB] = state[d, q][1]

        for d in range(2):
            xk = jnp.concatenate([xs_ref[k % 2, d, s] for s in range(slabs_per_k)], axis=1).astype(BF16)
            ytm_ref[d, k] = jnp.dot(xk, wc_ref[d, k], preferred_element_type=F32)

    for d in range(2):
        for k in range(S5_KBLK):
            for b in range(ns):
                y_refs[d][b, :, k * 128:(k + 1) * 128] = ytm_ref[d, k, seq_rows(b), :]

    @pl.when(c == pl.num_programs(1) - 1)
    def _():
        hfin_ref[...] = st_ref[...]


def _s5_call(u, wb, wc, a_bar, h0, latent, y_prev=None):
    if latent:
        length, ns, blk0 = L_LAT, N_LAT, TOK_CTX // (N_LAT * L_LAT)
    else:
        length, ns, blk0 = L_CTX, SUBLANES, 0
    steps = S5_ROWS // ns
    ng, nc = h0.shape[0], length // steps
    ublk = (ns, steps, S5_WIDTH)
    stblk = (None, 2, S5_KBLK, SUBLANES, 2 * S5_BLK)
    fwd = lambda g, c: (blk0 + g, c, 0)
    bwd = lambda g, c: (blk0 + g, nc - 1 - c, 0)
    const4 = lambda g, c: (0, 0, 0, 0)
    in_specs = [
        pl.BlockSpec(ublk, fwd),
        pl.BlockSpec(ublk, bwd),
        pl.BlockSpec((2, S5_KBLK, 128, 2 * S5_BLK), const4),
        pl.BlockSpec((2, S5_KBLK, 2 * S5_BLK, 128), const4),
        pl.BlockSpec((2, S5_KBLK, SUBLANES, 2 * S5_BLK), const4),
        pl.BlockSpec(stblk, lambda g, c: (g, 0, 0, 0, 0)),
    ]
    u3 = u.reshape(TOK // length, length, S5_WIDTH)
    args = [u3, u3, wb, wc, a_bar, h0]
    aliases = {}
    if y_prev is not None:
        in_specs += [pl.BlockSpec(memory_space=pl.ANY)] * 2
        args += [y.reshape(TOK // length, length, S5_WIDTH) for y in y_prev]
        aliases = {6: 0, 7: 1}
    y_shape = jax.ShapeDtypeStruct((TOK // length, length, S5_WIDTH), F32)
    yf, yb, hfin = pl.pallas_call(
        functools.partial(_s5_kernel, ns=ns, steps=steps, aliased=y_prev is not None),
        grid=(ng, nc),
        in_specs=in_specs,
        out_specs=[
            pl.BlockSpec(ublk, fwd),
            pl.BlockSpec(ublk, bwd),
            pl.BlockSpec(stblk, lambda g, c: (g, 0, 0, 0, 0)),
        ],
        out_shape=(y_shape, y_shape, jax.ShapeDtypeStruct(h0.shape, F32)),
        scratch_shapes=[
            pltpu.VMEM((2, S5_KBLK, S5_ROWS, S5_SLAB), F32),
            pltpu.VMEM((2, S5_KBLK, S5_ROWS, S5_SLAB), F32),
            pltpu.VMEM((2, 2, 2 * S5_QUARTERS, S5_ROWS, S5_SLAB), F32),
            pltpu.VMEM((2, 2, 2 * S5_QUARTERS, S5_ROWS, S5_SLAB), F32),
            pltpu.VMEM((2, S5_KBLK, SUBLANES, 2 * S5_BLK), F32),
        ],
        input_output_aliases=aliases,
        compiler_params=pltpu.CompilerParams(
            dimension_semantics=("parallel", "arbitrary"), vmem_limit_bytes=VMEM_BIG),
        name="s5_scan_latent" if latent else "s5_scan_context",
    )(*args)
    return yf.reshape(TOK, S5_WIDTH), yb.reshape(TOK, S5_WIDTH), hfin


def _s5_weights(ab_re, ab_im, bb_re, bb_im, c_re, c_im):
    eye = jnp.eye(S5_GPB, dtype=F32)

    def in_side(bb):
        t = bb.reshape(S5_GROUP_CH, 2, S5_KBLK, S5_GPB, S5_STATE)
        t = jnp.einsum('cdkgp,gh->dkgchp', t, eye)
        return t.reshape(2, S5_KBLK, S5_GPB * S5_GROUP_CH, S5_BLK)

    def out_side(cc):
        t = cc.reshape(2, S5_KBLK, S5_GPB, S5_GROUP_CH, S5_STATE)
        t = jnp.einsum('dkgcp,gh->dkgphc', t, eye)
        return t.reshape(2, S5_KBLK, S5_BLK, S5_GPB * S5_GROUP_CH)

    wb = jnp.concatenate([in_side(bb_re), in_side(bb_im)], axis=-1).astype(BF16)
    wc = jnp.concatenate([out_side(c_re), -out_side(c_im)], axis=-2).astype(BF16)
    a = jnp.concatenate([ab_re.reshape(2, S5_KBLK, 1, S5_BLK), ab_im.reshape(2, S5_KBLK, 1, S5_BLK)], axis=-1)
    a = jnp.broadcast_to(a, (2, S5_KBLK, SUBLANES, 2 * S5_BLK))
    return wb, wc, a


def _s5_state_in(s_re, s_im):
    def blk(s):
        return s.reshape(-1, 2, S5_KBLK, S5_BLK).transpose(1, 2, 0, 3)
    h = jnp.concatenate([blk(s_re), blk(s_im)], axis=-1)
    h = jnp.pad(h, ((0, 0), (0, 0), (0, SUBLANES - h.shape[2]), (0, 0)))
    return h[None]


def _s5_state_out(h):
    def unblk(s):
        return s.transpose(0, 3, 1, 2, 4).reshape(-1, 2, S5_GROUPS, S5_STATE)
    return unblk(h[..., :S5_BLK]), unblk(h[..., S5_BLK:])


def _gelu_tanh(x):
    return x * (0.5 * (1.0 + jnp.tanh(math.sqrt(2.0 / math.pi) * (x + 0.044715 * (x * x * x)))))


def _mix_kernel(yf_ref, yb_ref, u_ref, x_ref, xp_ref, xn_ref, mod_ref, nw_ref, wbg_ref, wcg_ref, wv_ref,
                d_ref, gw_ref, gb_ref, cw_ref, cb_ref, ow_ref, o_ref, wbcv_ref, gwb_ref, owb_ref):
    tm = TM_MIX
    ext = tm + 2 * HALO

    @pl.when(pl.program_id(0) == 0)
    def _():
        wbcv_ref[:, 0:CONV_W] = wbg_ref[...].astype(BF16)
        wbcv_ref[:, CONV_W:2 * CONV_W] = wcg_ref[...].astype(BF16)
        wbcv_ref[:, 2 * CONV_W:3 * CONV_W] = wv_ref[...].astype(BF16)
        gwb_ref[...] = gw_ref[...].astype(BF16)
        owb_ref[...] = ow_ref[...].astype(BF16)

    y = yf_ref[...] + yb_ref[...] + d_ref[...] * u_ref[...]
    z = _gelu_tanh(y)
    gate = jax.nn.sigmoid(jnp.dot(z.astype(BF16), gwb_ref[...], preferred_element_type=F32) + gb_ref[...])
    a_out = z * gate

    x = x_ref[...]
    xe = jnp.concatenate([xp_ref[...], x, xn_ref[...]], axis=0)
    h = _norm_mod(xe, nw_ref[...], mod_ref[1:2, :], mod_ref[0:1, :]).astype(BF16)
    cv = jnp.dot(h, wbcv_ref[:, CONV_W:3 * CONV_W], preferred_element_type=F32)
    bg = jnp.dot(h[HALO:HALO + tm, :], wbcv_ref[:, 0:CONV_W], preferred_element_type=F32)
    pe = cv[:, 0:CONV_W] * cv[:, CONV_W:2 * CONV_W]
    p = pe[HALO:HALO + tm, :]
    local = lax.broadcasted_iota(jnp.int32, (tm, 1), 0)
    row = pl.program_id(0) * tm + local
    seq_len = jnp.where(row < TOK_CTX, L_CTX, L_LAT)
    pos = jnp.bitwise_and(row, seq_len - 1)
    p_prev = jnp.where(pos == 0, 0.0, pltpu.roll(pe, 1, 0)[HALO:HALO + tm, :])
    p_next = jnp.where(pos == seq_len - 1, 0.0, pltpu.roll(pe, ext - 1, 0)[HALO:HALO + tm, :])
    conv = cw_ref[0:1, :] * p_prev + cw_ref[1:2, :] * p + cw_ref[2:3, :] * p_next + cb_ref[...]
    b_out = bg * conv

    out = (jnp.dot(a_out.astype(BF16), owb_ref[0:S5_WIDTH, :], preferred_element_type=F32)
           + jnp.dot(b_out.astype(BF16), owb_ref[S5_WIDTH:, :], preferred_element_type=F32))
    o_ref[...] = x + mod_ref[2:3, :] * out


def _mix_call(yf, yb, u, x, mod, nw, in_w, s5_d, glu_w, glu_b, conv_w, conv_b, out_w, layer):
    tm = TM_MIX
    hb = tm // HALO
    last_halo = TOK // HALO - 1
    tok_blk = lambda width: pl.BlockSpec((tm, width), lambda i: (i, 0))
    row1 = lambda width: pl.BlockSpec((1, width), lambda i: (0, 0))
    once = dict(pipeline_mode=pl.Buffered(1))
    w_col = lambda col: pl.BlockSpec((None, D_MODEL, CONV_W), lambda i: (layer, 0, col), **once)
    return pl.pallas_call(
        _mix_kernel,
        grid=(TOK // tm,),
        in_specs=[
            tok_blk(S5_WIDTH), tok_blk(S5_WIDTH), tok_blk(S5_WIDTH), tok_blk(D_MODEL),
            pl.BlockSpec((HALO, D_MODEL), lambda i: (jnp.maximum(i * hb - 1, 0), 0)),
            pl.BlockSpec((HALO, D_MODEL), lambda i: (jnp.minimum((i + 1) * hb, last_halo), 0)),
            pl.BlockSpec((None, MOD_ROWS, D_MODEL), lambda i: (_cond_row(i, tm), 0, 0)),
            row1(D_MODEL),
            w_col(1), w_col(2), w_col(3),
            row1(CONV_W),
            pl.BlockSpec((None, S5_WIDTH, S5_WIDTH), lambda i: (layer, 0, 0), **once),
            row1(CONV_W),
            pl.BlockSpec((MOD_ROWS, CONV_W), lambda i: (0, 0)),
            row1(CONV_W),
            pl.BlockSpec((None, D_MODEL, D_MODEL), lambda i: (layer, 0, 0), **once),
        ],
        out_specs=pl.BlockSpec((tm, D_MODEL), lambda i: (i, 0)),
        out_shape=jax.ShapeDtypeStruct((TOK, D_MODEL), F32),
        scratch_shapes=[pltpu.VMEM((D_MODEL, 3 * CONV_W), BF16), pltpu.VMEM((S5_WIDTH, S5_WIDTH), BF16),
                        pltpu.VMEM((D_MODEL, D_MODEL), BF16)],
        compiler_params=pltpu.CompilerParams(
            dimension_semantics=("arbitrary",), vmem_limit_bytes=VMEM_BIG),
        name="glu_conv_out",
    )(yf, yb, u, x, x, x, mod, nw.reshape(1, D_MODEL), in_w, in_w, in_w,
      s5_d.reshape(1, -1), glu_w, glu_b.reshape(1, -1),
      jnp.pad(conv_w, ((0, MOD_ROWS - conv_w.shape[0]), (0, 0))), conv_b.reshape(1, -1), out_w)


def _log_sigmoid(x):
    return jnp.minimum(x, 0.0) - jnp.log1p(jnp.exp(-jnp.abs(x)))


def _ret_kernel(*refs, latent, aliased):
    q_ref, k_ref, v_ref, g_ref, gam_ref, gn_ref = refs[:6]
    n_in = 6
    if latent:
        cos_ref, sin_ref, s0_ref = refs[6:9]
        n_in = 9
    n_in += aliased
    o_ref, sfin_ref, qs_ref, sc_ref, kv_ref, sk_ref = refs[n_in:]
    seq_chunks = RET_NC if latent else L_CTX // RET_C

    lg_f = _log_sigmoid(gam_ref[0, 0:1, :])
    lg_b = _log_sigmoid(gam_ref[1, 0:1, :])
    ii = lax.broadcasted_iota(jnp.int32, (RET_C, RET_C), 0)
    jj = lax.broadcasted_iota(jnp.int32, (RET_C, RET_C), 1)
    diff = (ii - jj).astype(F32)
    lgf_k, lgb_k = lg_f[:, :RET_DK], lg_b[:, :RET_DK]
    intra = (jnp.where(diff >= 0, jnp.exp(lgf_k * jnp.maximum(diff, 0.0)), 0.0)
             + jnp.where(diff <= 0, jnp.exp(lgb_k * jnp.maximum(-diff, 0.0)), 0.0))
    tk = lax.broadcasted_iota(jnp.int32, (RET_C, RET_DK), 0).astype(F32)
    tv = lax.broadcasted_iota(jnp.int32, (RET_C, RET_DV), 0).astype(F32)
    kdec_f = jnp.exp(lgf_k * (RET_C - 1.0 - tk))
    kdec_b = jnp.exp(lgb_k * tk)
    qdec_f = jnp.exp(lg_f * (tv + 1.0))
    qdec_b = jnp.exp(lg_b * (RET_C - tv))
    cd_f = jnp.exp(lg_f * RET_C)
    cd_b = jnp.exp(lg_b * RET_C)

    def rotate(x, cos, sin):
        return x * cos + pltpu.roll(x, RET_DK // 2, 1) * sin

    def swap_mid(s):
        qd = RET_DK // 4
        return jnp.concatenate([s[0:qd], s[2 * qd:3 * qd], s[qd:2 * qd], s[3 * qd:]], axis=0)

    tn_dims = (((0,), (0,)), ((), ()))
    nt_dims = (((1,), (1,)), ((), ()))

    def prep(c, _):
        r = pl.ds(pl.multiple_of(c * RET_C, RET_C), RET_C)
        q, k = q_ref[r, :], k_ref[r, :]
        if latent:
            cos, sin = cos_ref[r, :], sin_ref[r, :]
            q, k = rotate(q, cos, sin), rotate(k, cos, sin)
        qs = (q * (RET_DK ** -0.5)).astype(BF16)
        qs_ref[r, :] = qs
        scores = lax.dot_general(qs, k.astype(BF16), nt_dims, preferred_element_type=F32) * intra
        sc_ref[c] = scores.astype(BF16)
        kd = jnp.concatenate([(k * kdec_f).astype(BF16), (k * kdec_b).astype(BF16)], axis=1)
        kv_ref[c] = lax.dot_general(kd, v_ref[r, :], tn_dims, preferred_element_type=F32)
        return 0

    lax.fori_loop(0, RET_NC, prep, 0, unroll=RET_UNROLL)

    zeros = jnp.zeros((RET_DK, RET_DV), F32)
    s = swap_mid(s0_ref[0]) if latent else zeros
    for c in range(RET_NC):
        if c % seq_chunks == 0 and not (latent and c == 0):
            s = zeros
        sk_ref[c, :, 0:RET_DV] = s.astype(BF16)
        s = cd_f * s + kv_ref[c, 0:RET_DK, :]
        if (c + 1) % seq_chunks == 0:
            sfin_ref[c // seq_chunks, 0] = swap_mid(s)
    s = swap_mid(s0_ref[1]) if latent else zeros
    for c in reversed(range(RET_NC)):
        if (c + 1) % seq_chunks == 0 and not (latent and c == RET_NC - 1):
            s = zeros
        sk_ref[c, :, RET_DV:2 * RET_DV] = s.astype(BF16)
        s = cd_b * s + kv_ref[c, RET_DK:2 * RET_DK, :]
        if c % seq_chunks == 0:
            sfin_ref[c // seq_chunks, 1] = swap_mid(s)

    def emit(c, _):
        r = pl.ds(pl.multiple_of(c * RET_C, RET_C), RET_C)
        cross = jnp.dot(qs_ref[r, :], sk_ref[c], preferred_element_type=F32)
        o = (jnp.dot(sc_ref[c], v_ref[r, :], preferred_element_type=F32)
             + cross[:, 0:RET_DV] * qdec_f + cross[:, RET_DV:2 * RET_DV] * qdec_b)
        o = o * lax.rsqrt(jnp.mean(o * o, axis=-1, keepdims=True) + EPS) * gn_ref[...]
        o_ref[r, :] = (_silu(g_ref[r, :]) * o).astype(o_ref.dtype)
        return 0

    lax.fori_loop(0, RET_NC, emit, 0, unroll=RET_UNROLL)


def _ret_call(proj, gam, gn_w, latent, layer, rope=None, s0=None, o_prev=None, s_prev=None):
    nblk = TOK_CTX // RET_ROWS
    blk0 = nblk if latent else 0
    seqs = 1 if latent else RET_ROWS // L_CTX
    proj_qkg, proj_v = proj
    kcol, gcol = RET_QK // RET_DK, 2 * RET_QK // RET_DV
    in_specs = [
        pl.BlockSpec((RET_ROWS, RET_DK), lambda b, h: (blk0 + b, h)),
        pl.BlockSpec((RET_ROWS, RET_DK), lambda b, h: (blk0 + b, kcol + h)),
        pl.BlockSpec((RET_ROWS, RET_DV), lambda b, h: (blk0 + b, h)),
        pl.BlockSpec((RET_ROWS, RET_DV), lambda b, h: (blk0 + b, gcol + h)),
        pl.BlockSpec((None, 2, SUBLANES, RET_DV), lambda b, h: (h, 0, 0, 0)),
        pl.BlockSpec((1, RET_DV), lambda b, h: (0, h)),
    ]
    args = [proj_qkg, proj_qkg, proj_v, proj_qkg, gam, gn_w.reshape(1, RET_V)]
    aliases = {}
    if latent:
        in_specs += [
            pl.BlockSpec((RET_ROWS, RET_DK), lambda b, h: (0, 0)),
            pl.BlockSpec((RET_ROWS, RET_DK), lambda b, h: (0, 0)),
            pl.BlockSpec((None, None, 2, None, RET_DK, RET_DV), lambda b, h: (b, layer, 0, h, 0, 0)),
            pl.BlockSpec(memory_space=pl.ANY),
        ]
        args += [rope[0], rope[1], s0, o_prev]
        aliases = {len(args) - 1: 0}
        s_shape = jax.ShapeDtypeStruct((N_LAT, 1, 2, RET_H, RET_DK, RET_DV), F32)
        s_layer = 0
    else:
        s_shape = jax.ShapeDtypeStruct((N_CTX, DEPTH // 2, 2, RET_H, RET_DK, RET_DV), F32)
        s_layer = layer
        if s_prev is not None:
            in_specs += [pl.BlockSpec(memory_space=pl.ANY)]
            args += [s_prev]
            aliases = {len(args) - 1: 1}
    return pl.pallas_call(
        functools.partial(_ret_kernel, latent=latent, aliased=len(aliases)),
        grid=(nblk, RET_H),
        in_specs=in_specs,
        out_specs=[
            pl.BlockSpec((RET_ROWS, RET_DV), lambda b, h: (blk0 + b, h)),
            pl.BlockSpec((seqs, None, 2, None, RET_DK, RET_DV), lambda b, h: (b, s_layer, 0, h, 0, 0)),
        ],
        out_shape=(jax.ShapeDtypeStruct((TOK, RET_V), BF16), s_shape),
        scratch_shapes=[
            pltpu.VMEM((RET_ROWS, RET_DK), BF16),
            pltpu.VMEM((RET_NC, RET_C, RET_C), BF16),
            pltpu.VMEM((RET_NC, 2 * RET_DK, RET_DV), F32),
            pltpu.VMEM((RET_NC, RET_DK, 2 * RET_DV), BF16),
        ],
        input_output_aliases=aliases,
        compiler_params=pltpu.CompilerParams(
            dimension_semantics=("parallel", "arbitrary"), vmem_limit_bytes=VMEM_MID),
        name="retention_latent" if latent else "retention_context",
    )(*args)


def _rope_tables():
    n_freq = RET_DK // 4
    t = jnp.arange(L_LAT)
    row = (t // GRID_W).astype(F32)
    col = (t % GRID_W).astype(F32)
    inv_freq = jnp.power(ROPE_BASE, -jnp.arange(n_freq, dtype=F32) / n_freq)
    ar, ac = row[:, None] * inv_freq, col[:, None] * inv_freq
    cos = jnp.concatenate([jnp.cos(ar), jnp.cos(ac), jnp.cos(ar), jnp.cos(ac)], axis=-1)
    sin = jnp.concatenate([-jnp.sin(ar), -jnp.sin(ac), jnp.sin(ar), jnp.sin(ac)], axis=-1)
    return cos, sin


def kernel(x_prompt, x_sample, state_s5_re, state_s5_im, state_ret, c, c_ctx, norm1_w, norm2_w, ada_w, ada_b,
           hy_in_w, hy_out_w, s5_lam_re, s5_lam_im, s5_log_step, s5_b_re, s5_b_im, s5_c_re, s5_c_im,
           s5_d, s5_glu_w, s5_glu_b, conv_w, conv_b, ret_in_w, ret_out_w, ret_gamma_logit, ret_gn_w,
           mlp_w1, mlp_w2, final_norm_w):
    x = (x_prompt.reshape(TOK_CTX, D_MODEL), x_sample.reshape(TOK - TOK_CTX, D_MODEL))

    cond = jnp.concatenate([c_ctx[None, :], c], axis=0)
    cond = jnp.pad(cond, ((0, MOD_ROWS - N_COND), (0, 0)))
    mod_all = _ada_call(cond, ada_w, ada_b)
    mod_all = mod_all[:, :N_COND].reshape(DEPTH, N_COND, N_MOD, D_MODEL)
    mod_all = jnp.pad(mod_all, ((0, 0), (0, 0), (0, MOD_ROWS - N_MOD), (0, 0)))

    n_s5 = s5_lam_re.shape[0]
    rows = n_s5 * 2 * S5_GROUPS
    ab_re, ab_im, bb_re, bb_im = _disc_call(
        s5_lam_re.reshape(rows, S5_STATE), s5_lam_im.reshape(rows, S5_STATE),
        jnp.broadcast_to(s5_log_step.reshape(rows, 1), (rows, S5_STATE)),
        jnp.moveaxis(s5_b_re, -1, 0).reshape(S5_GROUP_CH, rows, S5_STATE),
        jnp.moveaxis(s5_b_im, -1, 0).reshape(S5_GROUP_CH, rows, S5_STATE))
    ab_re = ab_re.reshape(n_s5, 2, S5_GROUPS, S5_STATE)
    ab_im = ab_im.reshape(n_s5, 2, S5_GROUPS, S5_STATE)
    bb_re = bb_re.reshape(S5_GROUP_CH, n_s5, 2, S5_GROUPS, S5_STATE)
    bb_im = bb_im.reshape(S5_GROUP_CH, n_s5, 2, S5_GROUPS, S5_STATE)

    rope = _rope_tables()
    zero_state = jnp.zeros((N_CTX // SUBLANES, 2, S5_KBLK, SUBLANES, 2 * S5_BLK), F32)

    new_re, new_im, new_ret = [], [], None
    y_prompt = y_sample = u_next = None
    for i in range(DEPTH):
        j = i // 2
        mod = mod_all[i]
        if i % 2 == 0:
            if u_next is None:
                u, x = _uproj_call(x, mod, norm1_w[i], hy_in_w, j)
            else:
                u = u_next
            wb, wc, a_bar = _s5_weights(ab_re[j], ab_im[j], bb_re[:, j], bb_im[:, j], s5_c_re[j], s5_c_im[j])
            yf, yb, h_ctx = _s5_call(u, wb, wc, a_bar, zero_state, latent=False)
            yf, yb, _ = _s5_call(u, wb, wc, a_bar, _s5_state_in(state_s5_re[:, j], state_s5_im[:, j]),
                                 latent=True, y_prev=(yf, yb))
            x = _mix_call(yf, yb, u, x, mod, norm1_w[i], hy_in_w, s5_d[j], s5_glu_w, s5_glu_b[j],
                          conv_w[j], conv_b[j], hy_out_w, j)
            s_re, s_im = _s5_state_out(h_ctx)
            new_re.append(s_re)
            new_im.append(s_im)
        else:
            proj = _in_call(x, mod, norm1_w[i], ret_in_w, j)
            gam = jnp.broadcast_to(ret_gamma_logit[j].T[:, :, None, None], (RET_H, 2, SUBLANES, RET_DV))
            o, new_ret = _ret_call(proj, gam, ret_gn_w[j], latent=False, layer=j, s_prev=new_ret)
            o, _ = _ret_call(proj, gam, ret_gn_w[j], latent=True, layer=j, rope=rope, s0=state_ret, o_prev=o)
            x = _resid_call(o, ret_out_w, x, mod, j)
        if i + 1 < DEPTH and (i + 1) % 2 == 0:
            x, u_next = _mlp_call(x, mod, norm2_w[i], mlp_w1, mlp_w2, i,
                                  next_u=(mod_all[i + 1], norm1_w[i + 1], hy_in_w, (i + 1) // 2))
        elif i + 1 < DEPTH:
            x = _mlp_call(x, mod, norm2_w[i], mlp_w1, mlp_w2, i)
        else:
            y_prompt, y_sample = _mlp_call(x, mod, norm2_w[i], mlp_w1, mlp_w2, i, fw=final_norm_w)

    return (y_prompt.reshape(N_CTX, L_CTX, D_MODEL), y_sample.reshape(N_LAT, L_LAT, D_MODEL),
            jnp.stack(new_re, 1), jnp.stack(new_im, 1), new_ret)
```

```python
import functools
import math

import jax
import jax.numpy as jnp
from jax import lax
from jax.experimental import pallas as pl
from jax.experimental.pallas import tpu as pltpu

F32 = jnp.float32
BF16 = jnp.bfloat16

D_MODEL = 1024
DEPTH = 4
N_CTX, L_CTX = 32, 256
N_LAT, L_LAT = 4, 2048
TOK_CTX = N_CTX * L_CTX
TOK = TOK_CTX + N_LAT * L_LAT
N_COND = 1 + N_LAT
GRID_W = 64
EPS = 1e-6

S5_WIDTH = 512
S5_GROUP_CH = 16
S5_GROUPS = 32
S5_STATE = 64
S5_KBLK = 4
S5_GPB = S5_GROUPS // S5_KBLK
S5_BLK = S5_GPB * S5_STATE
S5_SLAB = 128
S5_QUARTERS = S5_BLK // S5_SLAB
S5_ROWS = 512
SUBLANES = 8
CONV_W = 512

RET_H = 8
RET_DK = 128
RET_DV = 256
RET_QK = RET_H * RET_DK
RET_V = RET_H * RET_DV
RET_IN = 2 * RET_QK + 2 * RET_V
RET_C = 128
RET_ROWS = 2048
RET_NC = RET_ROWS // RET_C
RET_UNROLL = 8
ROPE_BASE = 10000.0
MLP_H = 4 * D_MODEL
N_MOD = 6
MOD_ROWS = 8

VMEM_BIG = 56 * 1024 * 1024
VMEM_MID = 40 * 1024 * 1024

TM_DENSE = 1024
TN_DENSE = 1024
TM_PROJ = 512
TM_MLP = 1024
MLP_CHUNKS = 8
RET_IN_CHUNKS = RET_IN // TN_DENSE
RET_QK_CHUNKS = 2 * RET_QK // TN_DENSE
RET_V_CHUNKS = RET_V // TN_DENSE
TM_MIX = 512
HALO = 8


def _cond_row(i, tm):
    row0 = i * tm
    return jnp.where(row0 < TOK_CTX, 0, 1 + (row0 - TOK_CTX) // L_LAT)


def _norm_mod(x, nw, scale, shift):
    y = x * lax.rsqrt(jnp.mean(x * x, axis=-1, keepdims=True) + EPS)
    return (y * nw) * (1.0 + scale) + shift


def _silu(x):
    return x * jax.nn.sigmoid(x)


def _ada_kernel(c_ref, w_ref, b_ref, o_ref):
    sc = _silu(c_ref[...]).astype(BF16)
    o_ref[0] = jnp.dot(sc, w_ref[0].astype(BF16), preferred_element_type=F32) + b_ref[0]


def _ada_call(cond, ada_w, ada_b):
    tn = 1536
    return pl.pallas_call(
        _ada_kernel,
        grid=(DEPTH, N_MOD * D_MODEL // tn),
        in_specs=[
            pl.BlockSpec((MOD_ROWS, D_MODEL), lambda l, j: (0, 0)),
            pl.BlockSpec((1, D_MODEL, tn), lambda l, j: (l, 0, j)),
            pl.BlockSpec((1, 1, tn), lambda l, j: (l, 0, j)),
        ],
        out_specs=pl.BlockSpec((1, MOD_ROWS, tn), lambda l, j: (l, 0, j)),
        out_shape=jax.ShapeDtypeStruct((DEPTH, MOD_ROWS, N_MOD * D_MODEL), F32),
        compiler_params=pltpu.CompilerParams(
            dimension_semantics=("arbitrary", "arbitrary"), vmem_limit_bytes=VMEM_MID),
        name="ada_mod",
    )(cond, ada_w, ada_b.reshape(DEPTH, 1, N_MOD * D_MODEL))


def _pair_halves(w):
    qd = RET_DK // 4
    cols = []
    for hd in range(w.shape[1] // RET_DK):
        blk = w[:, hd * RET_DK:(hd + 1) * RET_DK]
        lane = lax.broadcasted_iota(jnp.int32, blk.shape, 1)
        blk = jnp.where(jnp.logical_and(lane >= qd, lane < 2 * qd), pltpu.roll(blk, RET_DK - qd, 1),
                        jnp.where(jnp.logical_and(lane >= 2 * qd, lane < 3 * qd), pltpu.roll(blk, qd, 1), blk))
        cols.append(blk)
    return jnp.concatenate(cols, axis=1)


def _in_kernel(x_ref, mod_ref, nw_ref, cos_ref, sin_ref, w_ref, oq_ref, oa_ref, ov_ref, wb_ref):
    cw = TN_DENSE
    s = pl.program_id(0)

    def rotate_heads(res):
        cos, sin = cos_ref[...], sin_ref[...]
        heads = [res[:, hd * RET_DK:(hd + 1) * RET_DK] for hd in range(cw // RET_DK)]
        return [blk * cos + pltpu.roll(blk, RET_DK // 2, 1) * sin for blk in heads]

    @pl.when(s < RET_QK_CHUNKS)
    def _():
        wb_ref[s] = _pair_halves(w_ref[...]).astype(BF16)

    @pl.when(jnp.logical_and(s >= RET_QK_CHUNKS, s < RET_IN_CHUNKS))
    def _():
        wb_ref[s] = w_ref[...].astype(BF16)

    @pl.when(s >= RET_IN_CHUNKS)
    def _():
        h = _norm_mod(x_ref[...], nw_ref[...], mod_ref[1:2, :], mod_ref[0:1, :]).astype(BF16)
        for c in range(RET_IN_CHUNKS):
            res = jnp.dot(h, wb_ref[c], preferred_element_type=F32)
            if c == 0:
                for hd, blk in enumerate(rotate_heads(res)):
                    oq_ref[:, hd * RET_DK:(hd + 1) * RET_DK] = (blk * (RET_DK ** -0.5)).astype(BF16)
            elif c == 1:
                for hd, blk in enumerate(rotate_heads(res)):
                    oa_ref[:, hd * RET_DK:(hd + 1) * RET_DK] = blk
            elif c < RET_QK_CHUNKS + RET_V_CHUNKS:
                cv = c - RET_QK_CHUNKS
                ov_ref[:, cv * cw:(cv + 1) * cw] = res.astype(BF16)
            else:
                ca = c - RET_QK_CHUNKS - RET_V_CHUNKS + 1
                oa_ref[:, ca * cw:(ca + 1) * cw] = res


def _in_call(x, mod, nw, w, layer, rope):
    tm, cw = TM_PROJ, TN_DENSE
    tile = lambda s: jnp.maximum(s - RET_IN_CHUNKS, 0)
    return pl.pallas_call(
        _in_kernel,
        grid=(RET_IN_CHUNKS + TOK // tm,),
        in_specs=[
            pl.BlockSpec((tm, D_MODEL), lambda s: (tile(s), 0)),
            pl.BlockSpec((None, MOD_ROWS, D_MODEL), lambda s: (_cond_row(tile(s), tm), 0, 0)),
            pl.BlockSpec((1, D_MODEL), lambda s: (0, 0)),
            pl.BlockSpec((tm, RET_DK), lambda s: (tile(s), 0)),
            pl.BlockSpec((tm, RET_DK), lambda s: (tile(s), 0)),
            pl.BlockSpec((None, D_MODEL, cw), lambda s: (layer, 0, jnp.minimum(s, RET_IN_CHUNKS - 1))),
        ],
        out_specs=[pl.BlockSpec((tm, RET_QK), lambda s: (tile(s), 0)),
                   pl.BlockSpec((tm, RET_QK + RET_V), lambda s: (tile(s), 0)),
                   pl.BlockSpec((tm, RET_V), lambda s: (tile(s), 0))],
        out_shape=(jax.ShapeDtypeStruct((TOK, RET_QK), BF16), jax.ShapeDtypeStruct((TOK, RET_QK + RET_V), F32),
                   jax.ShapeDtypeStruct((TOK, RET_V), BF16)),
        scratch_shapes=[pltpu.VMEM((RET_IN_CHUNKS, D_MODEL, cw), BF16)],
        compiler_params=pltpu.CompilerParams(
            dimension_semantics=("arbitrary",), vmem_limit_bytes=VMEM_BIG),
        name="norm_proj",
    )(x, mod, nw.reshape(1, D_MODEL), rope[0], rope[1], w)


def _uproj_kernel(*refs, paired, n_ctx_tiles):
    if paired:
        xc_ref, xl_ref, mod_ref, nw_ref, w_ref, u_ref, xcat_ref, wb_ref = refs
        x = jnp.where(pl.program_id(0) < n_ctx_tiles, xc_ref[...], xl_ref[...])
        xcat_ref[...] = x
    else:
        x_ref, mod_ref, nw_ref, w_ref, u_ref, wb_ref = refs
        x = x_ref[...]

    @pl.when(pl.program_id(0) == 0)
    def _():
        wb_ref[...] = w_ref[...].astype(BF16)

    h = _norm_mod(x, nw_ref[...], mod_ref[1:2, :], mod_ref[0:1, :]).astype(BF16)
    u_ref[...] = jnp.dot(h, wb_ref[...], preferred_element_type=F32)


def _uproj_call(x, mod, nw, w, layer):
    tm = TM_DENSE
    paired = isinstance(x, tuple)
    n_ctx_tiles = TOK_CTX // tm
    if paired:
        x_specs = [pl.BlockSpec((tm, D_MODEL), lambda i: (jnp.minimum(i, n_ctx_tiles - 1), 0)),
                   pl.BlockSpec((tm, D_MODEL), lambda i: (jnp.maximum(i - n_ctx_tiles, 0), 0))]
        xs = list(x)
    else:
        x_specs = [pl.BlockSpec((tm, D_MODEL), lambda i: (i, 0))]
        xs = [x]
    tok_out = lambda width: pl.BlockSpec((tm, width), lambda i: (i, 0))
    out_specs = [tok_out(S5_WIDTH)] + ([tok_out(D_MODEL)] if paired else [])
    out_shape = [jax.ShapeDtypeStruct((TOK, S5_WIDTH), F32)] + (
        [jax.ShapeDtypeStruct((TOK, D_MODEL), F32)] if paired else [])
    res = pl.pallas_call(
        functools.partial(_uproj_kernel, paired=paired, n_ctx_tiles=n_ctx_tiles),
        grid=(TOK // tm,),
        in_specs=x_specs + [
            pl.BlockSpec((None, MOD_ROWS, D_MODEL), lambda i: (_cond_row(i, tm), 0, 0)),
            pl.BlockSpec((1, D_MODEL), lambda i: (0, 0)),
            pl.BlockSpec((None, D_MODEL, S5_WIDTH), lambda i: (layer, 0, 0)),
        ],
        out_specs=out_specs,
        out_shape=out_shape,
        scratch_shapes=[pltpu.VMEM((D_MODEL, S5_WIDTH), BF16)],
        compiler_params=pltpu.CompilerParams(
            dimension_semantics=("arbitrary",), vmem_limit_bytes=VMEM_MID),
        name="norm_uproj",
    )(*xs, mod, nw.reshape(1, D_MODEL), w)
    return (res[0], res[1]) if paired else (res[0], x)


def _mlp_kernel(*refs, final, with_u, n_ctx_tiles):
    if final:
        x_ref, mod_ref, nw_ref, w1_ref, w2_ref, fw_ref, oc_ref, ol_ref, w1b_ref, w2b_ref = refs
    elif with_u:
        (x_ref, mod_ref, nw_ref, w1_ref, w2_ref, modn_ref, nwn_ref, wu_ref,
         o_ref, u_ref, w1b_ref, w2b_ref, wub_ref) = refs
    else:
        x_ref, mod_ref, nw_ref, w1_ref, w2_ref, o_ref, w1b_ref, w2b_ref = refs
    s = pl.program_id(0)

    @pl.when(s < MLP_CHUNKS)
    def _():
        w1b_ref[s] = w1_ref[...].astype(BF16)
        w2b_ref[s] = w2_ref[...].astype(BF16)

    if with_u:
        @pl.when(s == 0)
        def _():
            wub_ref[...] = wu_ref[...].astype(BF16)

    def result():
        x = x_ref[...]
        h = _norm_mod(x, nw_ref[...], mod_ref[4:5, :], mod_ref[3:4, :]).astype(BF16)
        acc = None
        for c in range(MLP_CHUNKS):
            a = jnp.maximum(jnp.dot(h, w1b_ref[c], preferred_element_type=F32), 0.0)
            t = jnp.dot((a * a).astype(BF16), w2b_ref[c], preferred_element_type=F32)
            acc = t if acc is None else acc + t
        y = x + mod_ref[5:6, :] * acc
        if final:
            y = y * lax.rsqrt(jnp.mean(y * y, axis=-1, keepdims=True) + EPS) * fw_ref[...]
        return y

    tile = s - MLP_CHUNKS
    if final:
        @pl.when(jnp.logical_and(tile >= 0, tile < n_ctx_tiles))
        def _():
            oc_ref[...] = result()

        @pl.when(tile >= n_ctx_tiles)
        def _():
            ol_ref[...] = result()
    else:
        @pl.when(tile >= 0)
        def _():
            y = result()
            o_ref[...] = y
            if with_u:
                hn = _norm_mod(y, nwn_ref[...], modn_ref[1:2, :], modn_ref[0:1, :]).astype(BF16)
                u_ref[...] = jnp.dot(hn, wub_ref[...], preferred_element_type=F32)


def _mlp_call(x, mod, nw, w1, w2, layer, fw=None, next_u=None):
    tm = TM_MLP
    final, with_u = fw is not None, next_u is not None
    n_tiles, n_ctx_tiles = TOK // tm, TOK_CTX // tm
    ch = MLP_H // MLP_CHUNKS
    tile = lambda s: jnp.maximum(s - MLP_CHUNKS, 0)
    chunk = lambda s: jnp.minimum(s, MLP_CHUNKS - 1)
    mod_spec = pl.BlockSpec((None, MOD_ROWS, D_MODEL), lambda s: (_cond_row(tile(s), tm), 0, 0))
    row_spec = pl.BlockSpec((1, D_MODEL), lambda s: (0, 0))
    tok_spec = lambda width: pl.BlockSpec((tm, width), lambda s: (tile(s), 0))
    in_specs = [
        tok_spec(D_MODEL), mod_spec, row_spec,
        pl.BlockSpec((None, D_MODEL, ch), lambda s: (layer, 0, chunk(s))),
        pl.BlockSpec((None, ch, D_MODEL), lambda s: (layer, chunk(s), 0)),
    ]
    args = [x, mod, nw.reshape(1, D_MODEL), w1, w2]
    scratch = [pltpu.VMEM((MLP_CHUNKS, D_MODEL, ch), BF16), pltpu.VMEM((MLP_CHUNKS, ch, D_MODEL), BF16)]
    if final:
        in_specs.append(row_spec)
        args.append(fw.reshape(1, D_MODEL))
        out_specs = [
            pl.BlockSpec((tm, D_MODEL), lambda s: (jnp.clip(s - MLP_CHUNKS, 0, n_ctx_tiles - 1), 0)),
            pl.BlockSpec((tm, D_MODEL), lambda s: (jnp.maximum(s - MLP_CHUNKS - n_ctx_tiles, 0), 0)),
        ]
        out_shape = (jax.ShapeDtypeStruct((TOK_CTX, D_MODEL), F32),
                     jax.ShapeDtypeStruct((TOK - TOK_CTX, D_MODEL), F32))
    elif with_u:
        mod_n, nw_n, w_n, layer_n = next_u
        in_specs += [mod_spec, row_spec, pl.BlockSpec((None, D_MODEL, S5_WIDTH), lambda s: (layer_n, 0, 0))]
        args += [mod_n, nw_n.reshape(1, D_MODEL), w_n]
        out_specs = [tok_spec(D_MODEL), tok_spec(S5_WIDTH)]
        out_shape = (jax.ShapeDtypeStruct((TOK, D_MODEL), F32), jax.ShapeDtypeStruct((TOK, S5_WIDTH), F32))
        scratch.append(pltpu.VMEM((D_MODEL, S5_WIDTH), BF16))
    else:
        out_specs = tok_spec(D_MODEL)
        out_shape = jax.ShapeDtypeStruct((TOK, D_MODEL), F32)
    return pl.pallas_call(
        functools.partial(_mlp_kernel, final=final, with_u=with_u, n_ctx_tiles=n_ctx_tiles),
        grid=(MLP_CHUNKS + n_tiles,),
        in_specs=in_specs,
        out_specs=out_specs,
        out_shape=out_shape,
        scratch_shapes=scratch,
        compiler_params=pltpu.CompilerParams(
            dimension_semantics=("arbitrary",), vmem_limit_bytes=VMEM_BIG),
        name="mlp_final" if final else "mlp",
    )(*args)


def _resid_kernel(a_ref, w_ref, x_ref, mod_ref, o_ref, wb_ref):
    @pl.when(pl.program_id(0) == 0)
    def _():
        wb_ref[...] = w_ref[...].astype(BF16)

    y = jnp.dot(a_ref[...], wb_ref[...], preferred_element_type=F32)
    o_ref[...] = x_ref[...] + mod_ref[2:3, :] * y


def _resid_call(a, w, x, mod, layer):
    tm = TM_DENSE
    k = a.shape[1]
    return pl.pallas_call(
        _resid_kernel,
        grid=(TOK // tm,),
        in_specs=[
            pl.BlockSpec((tm, k), lambda i: (i, 0)),
            pl.BlockSpec((None, k, D_MODEL), lambda i: (layer, 0, 0)),
            pl.BlockSpec((tm, D_MODEL), lambda i: (i, 0)),
            pl.BlockSpec((None, MOD_ROWS, D_MODEL), lambda i: (_cond_row(i, tm), 0, 0)),
        ],
        out_specs=pl.BlockSpec((tm, D_MODEL), lambda i: (i, 0)),
        out_shape=jax.ShapeDtypeStruct((TOK, D_MODEL), F32),
        scratch_shapes=[pltpu.VMEM((k, D_MODEL), BF16)],
        compiler_params=pltpu.CompilerParams(
            dimension_semantics=("arbitrary",), vmem_limit_bytes=VMEM_BIG),
        name="out_proj_resid",
    )(a, w, x, mod)


def _disc_kernel(lr_ref, li_ref, ls_ref, br_ref, bi_ref, abr_ref, abi_ref, bbr_ref, bbi_ref):
    lr, li = lr_ref[...], li_ref[...]
    dt = jnp.exp(ls_ref[...])
    mag = jnp.exp(lr * dt)
    abr = mag * jnp.cos(li * dt)
    abi = mag * jnp.sin(li * dt)
    den = lr * lr + li * li
    fr = ((abr - 1.0) * lr + abi * li) / den
    fi = (abi * lr - (abr - 1.0) * li) / den
    abr_ref[...] = abr
    abi_ref[...] = abi
    for c in range(S5_GROUP_CH):
        br, bi = br_ref[c], bi_ref[c]
        bbr_ref[c] = fr * br - fi * bi
        bbi_ref[c] = fr * bi + fi * br


def _disc_call(lam_re, lam_im, log_step, b_re, b_im):
    r = lam_re.shape[0]
    small = jax.ShapeDtypeStruct((r, S5_STATE), F32)
    big = jax.ShapeDtypeStruct((S5_GROUP_CH, r, S5_STATE), F32)
    return pl.pallas_call(_disc_kernel, out_shape=(small, small, big, big), name="s5_discretise")(
        lam_re, lam_im, log_step, b_re, b_im)


def _s5_kernel(*refs, ns, steps, aliased):
    uf_ref, ub_ref, wb_ref, wc_ref, a_ref, h0_ref = refs[:6]
    yf_ref, yb_ref, hfin_ref, utm_ref, ytm_ref, bu_ref, xs_ref, st_ref = refs[6 + (2 if aliased else 0):]
    c = pl.program_id(1)
    slabs_per_k = 2 * S5_QUARTERS
    u_refs = (uf_ref, ub_ref)
    y_refs = (yf_ref, yb_ref)

    def seq_rows(b):
        return pl.ds(b, steps, stride=ns)

    @pl.when(c == 0)
    def _():
        st_ref[...] = h0_ref[...]

    for d in range(2):
        for k in range(S5_KBLK):
            for b in range(ns):
                utm_ref[d, k, seq_rows(b), :] = u_refs[d][b, :, k * 128:(k + 1) * 128]

    for k in range(S5_KBLK):
        for d in range(2):
            res = jnp.dot(utm_ref[d, k].astype(BF16), wb_ref[d, k], preferred_element_type=F32)
            for s in range(slabs_per_k):
                bu_ref[k % 2, d, s] = res[:, s * S5_SLAB:(s + 1) * S5_SLAB]

        chains = [(d, q) for d in range(2) for q in range(S5_QUARTERS)]
        coef = {dq: (a_ref[dq[0], k, 0:ns, dq[1] * S5_SLAB:(dq[1] + 1) * S5_SLAB],
                     a_ref[dq[0], k, 0:ns, S5_BLK + dq[1] * S5_SLAB:S5_BLK + (dq[1] + 1) * S5_SLAB])
                for dq in chains}
        state = {dq: (st_ref[dq[0], k, 0:ns, dq[1] * S5_SLAB:(dq[1] + 1) * S5_SLAB],
                      st_ref[dq[0], k, 0:ns, S5_BLK + dq[1] * S5_SLAB:S5_BLK + (dq[1] + 1) * S5_SLAB])
                 for dq in chains}
        for i in range(steps):
            for d, q in chains:
                t = i if d == 0 else steps - 1 - i
                tile = slice(t * ns, (t + 1) * ns)
                (ar, ai), (xr, xi) = coef[d, q], state[d, q]
                xr, xi = (ar * xr - ai * xi + bu_ref[k % 2, d, q, tile, :],
                          ar * xi + ai * xr + bu_ref[k % 2, d, S5_QUARTERS + q, tile, :])
                xs_ref[k % 2, d, q, tile, :] = xr
                xs_ref[k % 2, d, S5_QUARTERS + q, tile, :] = xi
                state[d, q] = (xr, xi)
        for d, q in chains:
            st_ref[d, k, 0:ns, q * S5_SLAB:(q + 1) * S5_SLAB] = state[d, q][0]
            st_ref[d, k, 0:ns, S5_BLK + q * S5_SLAB:S5_BLK + (q + 1) * S5_SLAB] = state[d, q][1]

        for d in range(2):
            xk = jnp.concatenate([xs_ref[k % 2, d, s] for s in range(slabs_per_k)], axis=1).astype(BF16)
            ytm_ref[d, k] = jnp.dot(xk, wc_ref[d, k], preferred_element_type=F32)

    for d in range(2):
        for k in range(S5_KBLK):
            for b in range(ns):
                y_refs[d][b, :, k * 128:(k + 1) * 128] = ytm_ref[d, k, seq_rows(b), :]

    @pl.when(c == pl.num_programs(1) - 1)
    def _():
        hfin_ref[...] = st_ref[...]


def _s5_call(u, wb, wc, a_bar, h0, latent, y_prev=None):
    if latent:
        length, ns, blk0 = L_LAT, N_LAT, TOK_CTX // (N_LAT * L_LAT)
    else:
        length, ns, blk0 = L_CTX, SUBLANES, 0
    steps = S5_ROWS // ns
    ng, nc = h0.shape[0], length // steps
    ublk = (ns, steps, S5_WIDTH)
    stblk = (None, 2, S5_KBLK, SUBLANES, 2 * S5_BLK)
    fwd = lambda g, c: (blk0 + g, c, 0)
    bwd = lambda g, c: (blk0 + g, nc - 1 - c, 0)
    const4 = lambda g, c: (0, 0, 0, 0)
    in_specs = [
        pl.BlockSpec(ublk, fwd),
        pl.BlockSpec(ublk, bwd),
        pl.BlockSpec((2, S5_KBLK, 128, 2 * S5_BLK), const4),
        pl.BlockSpec((2, S5_KBLK, 2 * S5_BLK, 128), const4),
        pl.BlockSpec((2, S5_KBLK, SUBLANES, 2 * S5_BLK), const4),
        pl.BlockSpec(stblk, lambda g, c: (g, 0, 0, 0, 0)),
    ]
    u3 = u.reshape(TOK // length, length, S5_WIDTH)
    args = [u3, u3, wb, wc, a_bar, h0]
    aliases = {}
    if y_prev is not None:
        in_specs += [pl.BlockSpec(memory_space=pl.ANY)] * 2
        args += [y.reshape(TOK // length, length, S5_WIDTH) for y in y_prev]
        aliases = {6: 0, 7: 1}
    y_shape = jax.ShapeDtypeStruct((TOK // length, length, S5_WIDTH), F32)
    yf, yb, hfin = pl.pallas_call(
        functools.partial(_s5_kernel, ns=ns, steps=steps, aliased=y_prev is not None),
        grid=(ng, nc),
        in_specs=in_specs,
        out_specs=[
            pl.BlockSpec(ublk, fwd),
            pl.BlockSpec(ublk, bwd),
            pl.BlockSpec(stblk, lambda g, c: (g, 0, 0, 0, 0)),
        ],
        out_shape=(y_shape, y_shape, jax.ShapeDtypeStruct(h0.shape, F32)),
        scratch_shapes=[
            pltpu.VMEM((2, S5_KBLK, S5_ROWS, S5_SLAB), F32),
            pltpu.VMEM((2, S5_KBLK, S5_ROWS, S5_SLAB), F32),
            pltpu.VMEM((2, 2, 2 * S5_QUARTERS, S5_ROWS, S5_SLAB), F32),
            pltpu.VMEM((2, 2, 2 * S5_QUARTERS, S5_ROWS, S5_SLAB), F32),
            pltpu.VMEM((2, S5_KBLK, SUBLANES, 2 * S5_BLK), F32),
        ],
        input_output_aliases=aliases,
        compiler_params=pltpu.CompilerParams(
            dimension_semantics=("parallel", "arbitrary"), vmem_limit_bytes=VMEM_BIG),
        name="s5_scan_latent" if latent else "s5_scan_context",
    )(*args)
    return yf.reshape(TOK, S5_WIDTH), yb.reshape(TOK, S5_WIDTH), hfin


def _s5_weights(ab_re, ab_im, bb_re, bb_im, c_re, c_im):
    eye = jnp.eye(S5_GPB, dtype=F32)

    def in_side(bb):
        t = bb.reshape(S5_GROUP_CH, 2, S5_KBLK, S5_GPB, S5_STATE)
        t = jnp.einsum('cdkgp,gh->dkgchp', t, eye)
        return t.reshape(2, S5_KBLK, S5_GPB * S5_GROUP_CH, S5_BLK)

    def out_side(cc):
        t = cc.reshape(2, S5_KBLK, S5_GPB, S5_GROUP_CH, S5_STATE)
        t = jnp.einsum('dkgcp,gh->dkgphc', t, eye)
        return t.reshape(2, S5_KBLK, S5_BLK, S5_GPB * S5_GROUP_CH)

    wb = jnp.concatenate([in_side(bb_re), in_side(bb_im)], axis=-1).astype(BF16)
    wc = jnp.concatenate([out_side(c_re), -out_side(c_im)], axis=-2).astype(BF16)
    a = jnp.concatenate([ab_re.reshape(2, S5_KBLK, 1, S5_BLK), ab_im.reshape(2, S5_KBLK, 1, S5_BLK)], axis=-1)
    a = jnp.broadcast_to(a, (2, S5_KBLK, SUBLANES, 2 * S5_BLK))
    return wb, wc, a


def _s5_state_in(s_re, s_im):
    def blk(s):
        return s.reshape(-1, 2, S5_KBLK, S5_BLK).transpose(1, 2, 0, 3)
    h = jnp.concatenate([blk(s_re), blk(s_im)], axis=-1)
    h = jnp.pad(h, ((0, 0), (0, 0), (0, SUBLANES - h.shape[2]), (0, 0)))
    return h[None]


def _s5_state_out(h):
    def unblk(s):
        return s.transpose(0, 3, 1, 2, 4).reshape(-1, 2, S5_GROUPS, S5_STATE)
    return unblk(h[..., :S5_BLK]), unblk(h[..., S5_BLK:])


def _gelu_tanh(x):
    return x * (0.5 * (1.0 + jnp.tanh(math.sqrt(2.0 / math.pi) * (x + 0.044715 * (x * x * x)))))


def _mix_kernel(yf_ref, yb_ref, u_ref, x_ref, xp_ref, xn_ref, mod_ref, nw_ref, wbg_ref, wcg_ref, wv_ref,
                d_ref, gw_ref, gb_ref, cw_ref, cb_ref, ow_ref, o_ref, wbcv_ref, gwb_ref, owb_ref):
    tm = TM_MIX
    ext = tm + 2 * HALO

    @pl.when(pl.program_id(0) == 0)
    def _():
        wbcv_ref[:, 0:CONV_W] = wbg_ref[...].astype(BF16)
        wbcv_ref[:, CONV_W:2 * CONV_W] = wcg_ref[...].astype(BF16)
        wbcv_ref[:, 2 * CONV_W:3 * CONV_W] = wv_ref[...].astype(BF16)
        gwb_ref[...] = gw_ref[...].astype(BF16)
        owb_ref[...] = ow_ref[...].astype(BF16)

    y = yf_ref[...] + yb_ref[...] + d_ref[...] * u_ref[...]
    z = _gelu_tanh(y)
    gate = jax.nn.sigmoid(jnp.dot(z.astype(BF16), gwb_ref[...], preferred_element_type=F32) + gb_ref[...])
    a_out = z * gate

    x = x_ref[...]
    xe = jnp.concatenate([xp_ref[...], x, xn_ref[...]], axis=0)
    h = _norm_mod(xe, nw_ref[...], mod_ref[1:2, :], mod_ref[0:1, :]).astype(BF16)
    cv = jnp.dot(h, wbcv_ref[:, CONV_W:3 * CONV_W], preferred_element_type=F32)
    bg = jnp.dot(h[HALO:HALO + tm, :], wbcv_ref[:, 0:CONV_W], preferred_element_type=F32)
    pe = cv[:, 0:CONV_W] * cv[:, CONV_W:2 * CONV_W]
    p = pe[HALO:HALO + tm, :]
    local = lax.broadcasted_iota(jnp.int32, (tm, 1), 0)
    row = pl.program_id(0) * tm + local
    seq_len = jnp.where(row < TOK_CTX, L_CTX, L_LAT)
    pos = jnp.bitwise_and(row, seq_len - 1)
    p_prev = jnp.where(pos == 0, 0.0, pltpu.roll(pe, 1, 0)[HALO:HALO + tm, :])
    p_next = jnp.where(pos == seq_len - 1, 0.0, pltpu.roll(pe, ext - 1, 0)[HALO:HALO + tm, :])
    conv = cw_ref[0:1, :] * p_prev + cw_ref[1:2, :] * p + cw_ref[2:3, :] * p_next + cb_ref[...]
    b_out = bg * conv

    out = (jnp.dot(a_out.astype(BF16), owb_ref[0:S5_WIDTH, :], preferred_element_type=F32)
           + jnp.dot(b_out.astype(BF16), owb_ref[S5_WIDTH:, :], preferred_element_type=F32))
    o_ref[...] = x + mod_ref[2:3, :] * out


def _mix_call(yf, yb, u, x, mod, nw, in_w, s5_d, glu_w, glu_b, conv_w, conv_b, out_w, layer):
    tm = TM_MIX
    hb = tm // HALO
    last_halo = TOK // HALO - 1
    tok_blk = lambda width: pl.BlockSpec((tm, width), lambda i: (i, 0))
    row1 = lambda width: pl.BlockSpec((1, width), lambda i: (0, 0))
    once = dict(pipeline_mode=pl.Buffered(1))
    w_col = lambda col: pl.BlockSpec((None, D_MODEL, CONV_W), lambda i: (layer, 0, col), **once)
    return pl.pallas_call(
        _mix_kernel,
        grid=(TOK // tm,),
        in_specs=[
            tok_blk(S5_WIDTH), tok_blk(S5_WIDTH), tok_blk(S5_WIDTH), tok_blk(D_MODEL),
            pl.BlockSpec((HALO, D_MODEL), lambda i: (jnp.maximum(i * hb - 1, 0), 0)),
            pl.BlockSpec((HALO, D_MODEL), lambda i: (jnp.minimum((i + 1) * hb, last_halo), 0)),
            pl.BlockSpec((None, MOD_ROWS, D_MODEL), lambda i: (_cond_row(i, tm), 0, 0)),
            row1(D_MODEL),
            w_col(1), w_col(2), w_col(3),
            row1(CONV_W),
            pl.BlockSpec((None, S5_WIDTH, S5_WIDTH), lambda i: (layer, 0, 0), **once),
            row1(CONV_W),
            pl.BlockSpec((MOD_ROWS, CONV_W), lambda i: (0, 0)),
            row1(CONV_W),
            pl.BlockSpec((None, D_MODEL, D_MODEL), lambda i: (layer, 0, 0), **once),
        ],
        out_specs=pl.BlockSpec((tm, D_MODEL), lambda i: (i, 0)),
        out_shape=jax.ShapeDtypeStruct((TOK, D_MODEL), F32),
        scratch_shapes=[pltpu.VMEM((D_MODEL, 3 * CONV_W), BF16), pltpu.VMEM((S5_WIDTH, S5_WIDTH), BF16),
                        pltpu.VMEM((D_MODEL, D_MODEL), BF16)],
        compiler_params=pltpu.CompilerParams(
            dimension_semantics=("arbitrary",), vmem_limit_bytes=VMEM_BIG),
        name="glu_conv_out",
    )(yf, yb, u, x, x, x, mod, nw.reshape(1, D_MODEL), in_w, in_w, in_w,
      s5_d.reshape(1, -1), glu_w, glu_b.reshape(1, -1),
      jnp.pad(conv_w, ((0, MOD_ROWS - conv_w.shape[0]), (0, 0))), conv_b.reshape(1, -1), out_w)


def _log_sigmoid(x):
    return jnp.minimum(x, 0.0) - jnp.log1p(jnp.exp(-jnp.abs(x)))


def _ret_kernel(*refs, latent, aliased):
    q_ref, k_ref, v_ref, g_ref, gam_ref, gn_ref = refs[:6]
    n_in = 6
    if latent:
        s0_ref = refs[6]
        n_in = 7
    n_in += aliased
    o_ref, sfin_ref, sc_ref, kv_ref, sk_ref = refs[n_in:]
    seq_chunks = RET_NC if latent else L_CTX // RET_C

    lg_f = _log_sigmoid(gam_ref[0, 0:1, :])
    lg_b = _log_sigmoid(gam_ref[1, 0:1, :])
    ii = lax.broadcasted_iota(jnp.int32, (RET_C, RET_C), 0)
    jj = lax.broadcasted_iota(jnp.int32, (RET_C, RET_C), 1)
    diff = (ii - jj).astype(F32)
    lgf_k, lgb_k = lg_f[:, :RET_DK], lg_b[:, :RET_DK]
    intra = (jnp.where(diff >= 0, jnp.exp(lgf_k * jnp.maximum(diff, 0.0)), 0.0)
             + jnp.where(diff <= 0, jnp.exp(lgb_k * jnp.maximum(-diff, 0.0)), 0.0))
    tk = lax.broadcasted_iota(jnp.int32, (RET_C, RET_DK), 0).astype(F32)
    tv = lax.broadcasted_iota(jnp.int32, (RET_C, RET_DV), 0).astype(F32)
    kdec_f = jnp.exp(lgf_k * (RET_C - 1.0 - tk))
    kdec_b = jnp.exp(lgb_k * tk)
    qdec_f = jnp.exp(lg_f * (tv + 1.0))
    qdec_b = jnp.exp(lg_b * (RET_C - tv))
    cd_f = jnp.exp(lg_f * RET_C)
    cd_b = jnp.exp(lg_b * RET_C)

    def swap_mid(s):
        qd = RET_DK // 4
        return jnp.concatenate([s[0:qd], s[2 * qd:3 * qd], s[qd:2 * qd], s[3 * qd:]], axis=0)

    tn_dims = (((0,), (0,)), ((), ()))
    nt_dims = (((1,), (1,)), ((), ()))

    def prep(c, _):
        r = pl.ds(pl.multiple_of(c * RET_C, RET_C), RET_C)
        k = k_ref[r, :]
        scores = lax.dot_general(q_ref[r, :], k.astype(BF16), nt_dims, preferred_element_type=F32) * intra
        sc_ref[c] = scores.astype(BF16)
        kd = jnp.concatenate([(k * kdec_f).astype(BF16), (k * kdec_b).astype(BF16)], axis=1)
        kv_ref[c] = lax.dot_general(kd, v_ref[r, :], tn_dims, preferred_element_type=F32)
        return 0

    lax.fori_loop(0, RET_NC, prep, 0, unroll=RET_UNROLL)

    zeros = jnp.zeros((RET_DK, RET_DV), F32)
    s = swap_mid(s0_ref[0]) if latent else zeros
    for c in range(RET_NC):
        if c % seq_chunks == 0 and not (latent and c == 0):
            s = zeros
        sk_ref[c, :, 0:RET_DV] = s.astype(BF16)
        s = cd_f * s + kv_ref[c, 0:RET_DK, :]
        if (c + 1) % seq_chunks == 0:
            sfin_ref[c // seq_chunks, 0] = swap_mid(s)
    s = swap_mid(s0_ref[1]) if latent else zeros
    for c in reversed(range(RET_NC)):
        if (c + 1) % seq_chunks == 0 and not (latent and c == RET_NC - 1):
            s = zeros
        sk_ref[c, :, RET_DV:2 * RET_DV] = s.astype(BF16)
        s = cd_b * s + kv_ref[c, RET_DK:2 * RET_DK, :]
        if c % seq_chunks == 0:
            sfin_ref[c // seq_chunks, 1] = swap_mid(s)

    def emit(c, _):
        r = pl.ds(pl.multiple_of(c * RET_C, RET_C), RET_C)
        cross = jnp.dot(q_ref[r, :], sk_ref[c], preferred_element_type=F32)
        o = (jnp.dot(sc_ref[c], v_ref[r, :], preferred_element_type=F32)
             + cross[:, 0:RET_DV] * qdec_f + cross[:, RET_DV:2 * RET_DV] * qdec_b)
        o = o * lax.rsqrt(jnp.mean(o * o, axis=-1, keepdims=True) + EPS) * gn_ref[...]
        o_ref[r, :] = (_silu(g_ref[r, :]) * o).astype(o_ref.dtype)
        return 0

    lax.fori_loop(0, RET_NC, emit, 0, unroll=RET_UNROLL)


def _ret_call(proj, gam, gn_w, latent, layer, s0=None, o_prev=None, s_prev=None):
    nblk = TOK_CTX // RET_ROWS
    blk0 = nblk if latent else 0
    seqs = 1 if latent else RET_ROWS // L_CTX
    proj_q, proj_kg, proj_v = proj
    gcol = RET_QK // RET_DV
    in_specs = [
        pl.BlockSpec((RET_ROWS, RET_DK), lambda b, h: (blk0 + b, h)),
        pl.BlockSpec((RET_ROWS, RET_DK), lambda b, h: (blk0 + b, h)),
        pl.BlockSpec((RET_ROWS, RET_DV), lambda b, h: (blk0 + b, h)),
        pl.BlockSpec((RET_ROWS, RET_DV), lambda b, h: (blk0 + b, gcol + h)),
        pl.BlockSpec((None, 2, SUBLANES, RET_DV), lambda b, h: (h, 0, 0, 0)),
        pl.BlockSpec((1, RET_DV), lambda b, h: (0, h)),
    ]
    args = [proj_q, proj_kg, proj_v, proj_kg, gam, gn_w.reshape(1, RET_V)]
    aliases = {}
    if latent:
        in_specs += [
            pl.BlockSpec((None, None, 2, None, RET_DK, RET_DV), lambda b, h: (b, layer, 0, h, 0, 0)),
            pl.BlockSpec(memory_space=pl.ANY),
        ]
        args += [s0, o_prev]
        aliases = {len(args) - 1: 0}
        s_shape = jax.ShapeDtypeStruct((N_LAT, 1, 2, RET_H, RET_DK, RET_DV), F32)
        s_layer = 0
    else:
        s_shape = jax.ShapeDtypeStruct((N_CTX, DEPTH // 2, 2, RET_H, RET_DK, RET_DV), F32)
        s_layer = layer
        if s_prev is not None:
            in_specs += [pl.BlockSpec(memory_space=pl.ANY)]
            args += [s_prev]
            aliases = {len(args) - 1: 1}
    return pl.pallas_call(
        functools.partial(_ret_kernel, latent=latent, aliased=len(aliases)),
        grid=(nblk, RET_H),
        in_specs=in_specs,
        out_specs=[
            pl.BlockSpec((RET_ROWS, RET_DV), lambda b, h: (blk0 + b, h)),
            pl.BlockSpec((seqs, None, 2, None, RET_DK, RET_DV), lambda b, h: (b, s_layer, 0, h, 0, 0)),
        ],
        out_shape=(jax.ShapeDtypeStruct((TOK, RET_V), BF16), s_shape),
        scratch_shapes=[
            pltpu.VMEM((RET_NC, RET_C, RET_C), BF16),
            pltpu.VMEM((RET_NC, 2 * RET_DK, RET_DV), F32),
            pltpu.VMEM((RET_NC, RET_DK, 2 * RET_DV), BF16),
        ],
        input_output_aliases=aliases,
        compiler_params=pltpu.CompilerParams(
            dimension_semantics=("parallel", "arbitrary"), vmem_limit_bytes=VMEM_MID),
        name="retention_latent" if latent else "retention_context",
    )(*args)


def _rope_tables():
    n_freq = RET_DK // 4
    t = jnp.arange(L_LAT)
    row = (t // GRID_W).astype(F32)
    col = (t % GRID_W).astype(F32)
    inv_freq = jnp.power(ROPE_BASE, -jnp.arange(n_freq, dtype=F32) / n_freq)
    ar, ac = row[:, None] * inv_freq, col[:, None] * inv_freq
    cos = jnp.concatenate([jnp.cos(ar), jnp.cos(ac), jnp.cos(ar), jnp.cos(ac)], axis=-1)
    sin = jnp.concatenate([-jnp.sin(ar), -jnp.sin(ac), jnp.sin(ar), jnp.sin(ac)], axis=-1)
    cos = jnp.concatenate([jnp.ones((TOK_CTX, RET_DK), F32)] + [cos] * N_LAT, axis=0)
    sin = jnp.concatenate([jnp.zeros((TOK_CTX, RET_DK), F32)] + [sin] * N_LAT, axis=0)
    return cos, sin


def kernel(x_prompt, x_sample, state_s5_re, state_s5_im, state_ret, c, c_ctx, norm1_w, norm2_w, ada_w, ada_b,
           hy_in_w, hy_out_w, s5_lam_re, s5_lam_im, s5_log_step, s5_b_re, s5_b_im, s5_c_re, s5_c_im,
           s5_d, s5_glu_w, s5_glu_b, conv_w, conv_b, ret_in_w, ret_out_w, ret_gamma_logit, ret_gn_w,
           mlp_w1, mlp_w2, final_norm_w):
    x = (x_prompt.reshape(TOK_CTX, D_MODEL), x_sample.reshape(TOK - TOK_CTX, D_MODEL))

    cond = jnp.concatenate([c_ctx[None, :], c], axis=0)
    cond = jnp.pad(cond, ((0, MOD_ROWS - N_COND), (0, 0)))
    mod_all = _ada_call(cond, ada_w, ada_b)
    mod_all = mod_all[:, :N_COND].reshape(DEPTH, N_COND, N_MOD, D_MODEL)
    mod_all = jnp.pad(mod_all, ((0, 0), (0, 0), (0, MOD_ROWS - N_MOD), (0, 0)))

    n_s5 = s5_lam_re.shape[0]
    rows = n_s5 * 2 * S5_GROUPS
    ab_re, ab_im, bb_re, bb_im = _disc_call(
        s5_lam_re.reshape(rows, S5_STATE), s5_lam_im.reshape(rows, S5_STATE),
        jnp.broadcast_to(s5_log_step.reshape(rows, 1), (rows, S5_STATE)),
        jnp.moveaxis(s5_b_re, -1, 0).reshape(S5_GROUP_CH, rows, S5_STATE),
        jnp.moveaxis(s5_b_im, -1, 0).reshape(S5_GROUP_CH, rows, S5_STATE))
    ab_re = ab_re.reshape(n_s5, 2, S5_GROUPS, S5_STATE)
    ab_im = ab_im.reshape(n_s5, 2, S5_GROUPS, S5_STATE)
    bb_re = bb_re.reshape(S5_GROUP_CH, n_s5, 2, S5_GROUPS, S5_STATE)
    bb_im = bb_im.reshape(S5_GROUP_CH, n_s5, 2, S5_GROUPS, S5_STATE)

    rope = _rope_tables()
    zero_state = jnp.zeros((N_CTX // SUBLANES, 2, S5_KBLK, SUBLANES, 2 * S5_BLK), F32)

    new_re, new_im, new_ret = [], [], None
    y_prompt = y_sample = u_next = None
    for i in range(DEPTH):
        j = i // 2
        mod = mod_all[i]
        if i % 2 == 0:
            if u_next is None:
                u, x = _uproj_call(x, mod, norm1_w[i], hy_in_w, j)
            else:
                u = u_next
            wb, wc, a_bar = _s5_weights(ab_re[j], ab_im[j], bb_re[:, j], bb_im[:, j], s5_c_re[j], s5_c_im[j])
            yf, yb, h_ctx = _s5_call(u, wb, wc, a_bar, zero_state, latent=False)
            yf, yb, _ = _s5_call(u, wb, wc, a_bar, _s5_state_in(state_s5_re[:, j], state_s5_im[:, j]),
                                 latent=True, y_prev=(yf, yb))
            x = _mix_call(yf, yb, u, x, mod, norm1_w[i], hy_in_w, s5_d[j], s5_glu_w, s5_glu_b[j],
                          conv_w[j], conv_b[j], hy_out_w, j)
            s_re, s_im = _s5_state_out(h_ctx)
            new_re.append(s_re)
            new_im.append(s_im)
        else:
            proj = _in_call(x, mod, norm1_w[i], ret_in_w, j, rope)
            gam = jnp.broadcast_to(ret_gamma_logit[j].T[:, :, None, None], (RET_H, 2, SUBLANES, RET_DV))
            o, new_ret = _ret_call(proj, gam, ret_gn_w[j], latent=False, layer=j, s_prev=new_ret)
            o, _ = _ret_call(proj, gam, ret_gn_w[j], latent=True, layer=j, s0=state_ret, o_prev=o)
            x = _resid_call(o, ret_out_w, x, mod, j)
        if i + 1 < DEPTH and (i + 1) % 2 == 0:
            x, u_next = _mlp_call(x, mod, norm2_w[i], mlp_w1, mlp_w2, i,
                                  next_u=(mod_all[i + 1], norm1_w[i + 1], hy_in_w, (i + 1) // 2))
        elif i + 1 < DEPTH:
            x = _mlp_call(x, mod, norm2_w[i], mlp_w1, mlp_w2, i)
        else:
            y_prompt, y_sample = _mlp_call(x, mod, norm2_w[i], mlp_w1, mlp_w2, i, fw=final_norm_w)

    return (y_prompt.reshape(N_CTX, L_CTX, D_MODEL), y_sample.reshape(N_LAT, L_LAT, D_MODEL),
            jnp.stack(new_re, 1), jnp.stack(new_im, 1), new_ret)
```
